```python
import math
import jax, jax.numpy as jnp
from jax import lax
import numpy as np

D_MODEL = 1024
BATCH = 16
SEQ = 2048
DEPTH = 2

GRID_W = 64
CTX_LEN = 256
F32 = jnp.float32
EPS = 1e-6
HEAD_DIM = 64
D_MIX = D_MODEL
W_LRU = D_MIX // 4
W_RWKV = D_MIX // 4
W_MLSTM = D_MIX // 4
W_HYENA = D_MIX - W_LRU - W_RWKV - W_MLSTM
GROUP_WIDTHS = (W_LRU, W_RWKV, W_MLSTM, W_HYENA)
H_LRU = W_LRU // HEAD_DIM
H_RWKV = W_RWKV // HEAD_DIM
H_MLSTM = W_MLSTM // HEAD_DIM
CONV_W = 4
LRU_C = 8.0
RWKV_LORA_W = 32
RWKV_LORA_A = 32
RWKV_LORA_G = 64
RWKV_GN_EPS = 64e-5
MLSTM_CHUNK = 64
HYENA_ORDER = 2
HYENA_SHORT = 3
HYENA_EMB = 33
HYENA_HID = 64
HYENA_TARGET = 1e-2
HYENA_FAST = 0.3
HYENA_SLOW = 1.5
PEER_HEADS = 8
PEER_NKEYS = 128
PEER_EXPERTS = PEER_NKEYS * PEER_NKEYS
PEER_TOPK = 16
PEER_DQ = 256
PEER_BLOCK = 128
IN_SPLITS = (W_LRU, W_LRU, W_RWKV, W_RWKV, W_RWKV, RWKV_LORA_W, RWKV_LORA_A, RWKV_LORA_G, W_MLSTM, W_MLSTM, W_MLSTM, W_MLSTM, 4 * H_MLSTM, W_HYENA, W_HYENA, W_HYENA)
D_IN = sum(IN_SPLITS)

kernel_name = 'hybrid_prefix_dit_lru_rwkv_mlstm_hyena_peer'


def rmsnorm(x, g):
    xf = x.astype(F32)
    y = xf * lax.rsqrt(jnp.mean(xf * xf, axis=-1, keepdims=True) + EPS)
    return (y * g.astype(F32)).astype(x.dtype)


def modulate(h, shift, scale):
    return h * (1 + scale) + shift


def flip_if(a, d, axis=1):
    return jnp.flip(a, axis) if d == 1 else a


def split_cols(z):
    offs = np.cumsum(IN_SPLITS)[:-1].tolist()
    return jnp.split(z, offs, axis=-1)


def dwconv(x, w, b, pad_left):
    K = w.shape[0]
    y = lax.conv_general_dilated(x, w[:, None, :].astype(x.dtype), (1,), [(pad_left, K - 1 - pad_left)],
                                 dimension_numbers=('NWC', 'WIO', 'NWC'), feature_group_count=x.shape[-1])
    return y + b.astype(x.dtype)


def token_shift(x, mu):
    prev = jnp.pad(x, ((0, 0), (1, 0), (0, 0)))[:, :-1]
    nxt = jnp.pad(x, ((0, 0), (0, 1), (0, 0)))[:, 1:]
    return x + mu[0] * (prev - x) + mu[1] * (nxt - x)


def split_heads(t, h):
    Bn, L, W = t.shape
    return t.reshape(Bn, L, h, W // h)


def to_colmajor(a):
    Bn, L, C = a.shape
    rows = L // GRID_W
    return a.reshape(Bn, rows, GRID_W, C).transpose(0, 2, 1, 3).reshape(Bn, L, C)


def from_colmajor(a):
    Bn, L, C = a.shape
    rows = L // GRID_W
    return a.reshape(Bn, GRID_W, rows, C).transpose(0, 2, 1, 3).reshape(Bn, L, C)


def linear_recurrence(a, b, h0):
    def comb(l, r):
        return (l[0] * r[0], r[0] * l[1] + r[1])
    A, H = lax.associative_scan(comb, (a, b), axis=1)
    return H + A * h0[:, None, :]


def rglru_scan(xc, p, d, h0):
    Bn, L, W = xc.shape
    xh = xc.reshape(Bn, L, H_LRU, HEAD_DIM)
    r = jax.nn.sigmoid(jnp.einsum('blhi,hij->blhj', xh, p['lru_wr'][d]).reshape(Bn, L, W) + p['lru_br'][d])
    i = jax.nn.sigmoid(jnp.einsum('blhi,hij->blhj', xh, p['lru_wi'][d]).reshape(Bn, L, W) + p['lru_bi'][d])
    log_a = -LRU_C * r * jax.nn.softplus(-p['lru_lam'][d])
    b = jnp.sqrt(-jnp.expm1(2.0 * log_a)) * (i * xc)
    return linear_recurrence(jnp.exp(log_a), b, h0)


def mixer_rglru(f_c, f_l, p, need_ctx):
    (x_c, g_c), (x_l, g_l) = f_c, f_l
    xc_c = dwconv(x_c, p['lru_conv_w'], p['lru_conv_b'], CONV_W // 2).astype(F32)
    xc_l = dwconv(x_l, p['lru_conv_w'], p['lru_conv_b'], CONV_W // 2).astype(F32)
    h0 = jnp.zeros((xc_l.shape[0], W_LRU), F32)
    hs_c, hs_l = [], []
    for d in range(2):
        hc = rglru_scan(flip_if(xc_c, d), p, d, h0)
        hl = rglru_scan(flip_if(xc_l, d), p, d, hc[:, -1])
        hs_c.append(flip_if(hc, d))
        hs_l.append(flip_if(hl, d))
    y_l = jax.nn.gelu(g_l.astype(F32)) * (hs_l[0] + hs_l[1])
    y_c = jax.nn.gelu(g_c.astype(F32)) * (hs_c[0] + hs_c[1]) if need_ctx else None
    return y_c, y_l


def rwkv_shift(f, p):
    r, k, v, zw, za, zg = [t.astype(F32) for t in f]
    mu = p['rwkv_mu']
    return (token_shift(r, mu[0]), token_shift(k, mu[1]), token_shift(v, mu[2]), zw, za, zg)


def rwkv_dir_inputs(f, p, d):
    r, k, v, zw, za, _ = f
    w_log = -jax.nn.softplus(-(p['rwkv_w0'][d] + jnp.tanh(zw) @ p['rwkv_w2'][d])) - 0.5
    decay = jnp.exp(-jnp.exp(w_log))
    a = jax.nn.sigmoid(p['rwkv_a0'][d] + za @ p['rwkv_a2'][d])
    kk = split_heads(k * p['rwkv_kk'], H_RWKV)
    kk = kk / jnp.maximum(jnp.linalg.norm(kk, axis=-1, keepdims=True), 1e-12)
    kd = k * (1 + (a - 1) * p['rwkv_ka'])
    return [split_heads(r, H_RWKV), split_heads(decay, H_RWKV), split_heads(kd, H_RWKV),
            split_heads(v, H_RWKV), kk, kk * split_heads(a, H_RWKV)]


def rwkv7_scan(ins, s0):
    def step(S, inp):
        r, w, k, v, kk, kka = inp
        sa = jnp.einsum('bhvk,bhk->bhv', S, kk)
        S = S * w[:, :, None, :] - sa[..., None] * kka[:, :, None, :] + v[..., None] * k[:, :, None, :]
        return S, jnp.einsum('bhvk,bhk->bhv', S, r)
    S, ys = lax.scan(step, s0, tuple(jnp.moveaxis(t, 1, 0) for t in ins))
    return jnp.moveaxis(ys, 0, 1), S


def rwkv_bonus(ins, p):
    r, _, kd, v = ins[:4]
    return jnp.sum(r * kd * p['rwkv_rk'], axis=-1, keepdims=True) * v


def rwkv_out(y, bonus, zg, p):
    Bn, L, H, N = y.shape
    mu = jnp.mean(y, axis=-1, keepdims=True)
    var = jnp.mean(jnp.square(y - mu), axis=-1, keepdims=True)
    yn = ((y - mu) * lax.rsqrt(var + RWKV_GN_EPS)).reshape(Bn, L, H * N) * p['rwkv_ln_g'] + p['rwkv_ln_b']
    g = jax.nn.sigmoid(zg) @ p['rwkv_g2']
    return (yn + bonus.reshape(Bn, L, H * N)) * g


def mixer_rwkv(f_c, f_l, p, need_ctx):
    fc, fl = rwkv_shift(f_c, p), rwkv_shift(f_l, p)
    s0 = jnp.zeros((fl[0].shape[0], H_RWKV, HEAD_DIM, HEAD_DIM), F32)
    y_c, y_l = [], []
    for d in range(2):
        ic, il = rwkv_dir_inputs(fc, p, d), rwkv_dir_inputs(fl, p, d)
        oc, s_ctx = rwkv7_scan([flip_if(t, d) for t in ic], s0)
        ol, _ = rwkv7_scan([flip_if(t, d) for t in il], s_ctx)
        y_l.append((flip_if(ol, d), rwkv_bonus(il, p)))
        if need_ctx:
            y_c.append((flip_if(oc, d), rwkv_bonus(ic, p)))
    out_l = rwkv_out(y_l[0][0] + y_l[1][0], y_l[0][1] + y_l[1][1], fl[5], p)
    out_c = rwkv_out(y_c[0][0] + y_c[1][0], y_c[0][1] + y_c[1][1], fc[5], p) if need_ctx else None
    return out_c, out_l


def mlstm_prep(f, p):
    q, k, v, o, gz = f
    Bn, L, _ = q.shape
    qk = jax.nn.silu(dwconv(jnp.concatenate([q, k], axis=-1), p['mlstm_conv_w'], p['mlstm_conv_b'], CONV_W // 2)).astype(F32)
    q, k = jnp.split(qk, 2, axis=-1)
    heads = lambda t: t.reshape(Bn, L, H_MLSTM, HEAD_DIM).transpose(0, 2, 1, 3)
    gates = (gz.astype(F32).reshape(Bn, L, 2, 2, H_MLSTM) + p['mlstm_gate_b']).transpose(2, 3, 0, 4, 1)
    return heads(q) * HEAD_DIM ** -0.5, heads(k), heads(v.astype(F32)), o.astype(F32), gates


def mlstm_chunkwise(q, k, v, ig, lf, state):
    Bn, H, L, N = q.shape
    T = MLSTM_CHUNK
    nc = L // T
    ch = lambda a: jnp.moveaxis(a.reshape((Bn, H, nc, T) + a.shape[3:]), 2, 0)
    mask = jnp.tril(jnp.ones((T, T), dtype=bool))

    def step(carry, inp):
        C, n, m = carry
        qc, kc, vc, ic, fc = inp
        b = jnp.cumsum(fc, axis=-1)
        logd = jnp.where(mask, b[..., :, None] - b[..., None, :] + ic[..., None, :], -jnp.inf)
        inter = b + m[..., None]
        mt = jnp.maximum(inter, jnp.max(logd, axis=-1))
        s = jnp.einsum('bhtn,bhsn->bhts', qc, kc) * jnp.exp(logd - mt[..., None])
        e_inter = jnp.exp(inter - mt)
        num = jnp.einsum('bhts,bhsn->bhtn', s, vc) + e_inter[..., None] * jnp.einsum('bhvk,bhtk->bhtv', C, qc)
        den = jnp.sum(s, axis=-1) + e_inter * jnp.einsum('bhk,bhtk->bht', n, qc)
        h = num / jnp.maximum(jnp.abs(den), jnp.exp(-mt))[..., None]
        bT = b[..., -1]
        logw = bT[..., None] - b + ic
        m_new = jnp.maximum(bT + m, jnp.max(logw, axis=-1))
        wgt = jnp.exp(logw - m_new[..., None])
        dec = jnp.exp(bT + m - m_new)
        C = dec[..., None, None] * C + jnp.einsum('bhs,bhsv,bhsk->bhvk', wgt, vc, kc)
        n = dec[..., None] * n + jnp.einsum('bhs,bhsk->bhk', wgt, kc)
        return (C, n, m_new), h

    state, hs = lax.scan(step, state, (ch(q), ch(k), ch(v), ch(ig), ch(lf)))
    return jnp.moveaxis(hs, 0, 2).reshape(Bn, H, L, N), state


def mixer_mlstm(f_c, f_l, p, need_ctx):
    f_l = [to_colmajor(t) for t in f_l]
    qc, kc, vc, oc, gc = mlstm_prep(f_c, p)
    ql, kl, vl, ol, gl = mlstm_prep(f_l, p)
    Bn = ql.shape[0]
    st0 = (jnp.zeros((Bn, H_MLSTM, HEAD_DIM, HEAD_DIM), F32), jnp.zeros((Bn, H_MLSTM, HEAD_DIM), F32),
           jnp.zeros((Bn, H_MLSTM), F32))
    h_c, h_l = [], []
    for d in range(2):
        hc, st = mlstm_chunkwise(flip_if(qc, d, 2), flip_if(kc, d, 2), flip_if(vc, d, 2), flip_if(gc[d, 0], d, 2),
                                 jax.nn.log_sigmoid(flip_if(gc[d, 1], d, 2)), st0)
        hl, _ = mlstm_chunkwise(flip_if(ql, d, 2), flip_if(kl, d, 2), flip_if(vl, d, 2), flip_if(gl[d, 0], d, 2),
                                jax.nn.log_sigmoid(flip_if(gl[d, 1], d, 2)), st)
        h_c.append(flip_if(hc, d, 2))
        h_l.append(flip_if(hl, d, 2))
    merge_heads = lambda h: h.transpose(0, 2, 1, 3).reshape(h.shape[0], h.shape[2], W_MLSTM)
    y_l = from_colmajor(jax.nn.sigmoid(ol) * merge_heads(h_l[0] + h_l[1]))
    y_c = jax.nn.sigmoid(oc) * merge_heads(h_c[0] + h_c[1]) if need_ctx else None
    return y_c, y_l


def hyena_spectrum(L, p):
    pos = jnp.arange(L, dtype=F32)
    t = pos / (L - 1)
    bands = (HYENA_EMB - 1) // 2
    freqs = jnp.linspace(1e-4, bands - 1, bands, dtype=F32)
    ang = (2 * math.pi / L) * pos[:, None] * freqs[None, :]
    z = jnp.concatenate([t[:, None], jnp.cos(ang), -jnp.sin(ang)], axis=-1)
    h = jnp.sin(p['hy_freq'][0] * (z @ p['hy_w1'] + p['hy_b1']))
    h = jnp.sin(p['hy_freq'][1] * (h @ p['hy_w2'] + p['hy_b2']))
    h = (h @ p['hy_w3']).astype(F32).reshape(L, HYENA_ORDER, 2, W_HYENA)
    deltas = jnp.abs(jnp.linspace(math.log(HYENA_TARGET) / HYENA_SLOW, math.log(HYENA_TARGET) / HYENA_FAST,
                                  W_HYENA, dtype=F32))
    h = h * jnp.exp(-t[:, None, None, None] * deltas)
    fwd, bwd = h[:, :, 0], h[:, :, 1]
    two = jnp.concatenate([fwd, jnp.zeros_like(fwd[:1]), jnp.flip(bwd[1:], axis=0)], axis=0)
    two = two / (jnp.sum(jnp.abs(two), axis=0, keepdims=True) + EPS)
    return jnp.fft.rfft(two, axis=0)


def long_conv(u, spec, bias):
    L = u.shape[1]
    y = jnp.fft.irfft(jnp.fft.rfft(u, n=2 * L, axis=1) * spec, n=2 * L, axis=1)[:, :L]
    return y + u * bias


def mixer_hyena(f, p):
    u = dwconv(jnp.concatenate(f, axis=-1), p['hy_conv_w'], p['hy_conv_b'], HYENA_SHORT // 2).astype(F32)
    v, x1, x2 = jnp.split(u, 3, axis=-1)
    spec = hyena_spectrum(u.shape[1], p)
    z = x1 * long_conv(v, spec[:, 0], p['hy_bias'][0])
    return x2 * long_conv(z, spec[:, 1], p['hy_bias'][1])


def peer_ffn(h, p):
    Bn, L, D = h.shape
    wq, keys, u_tab, v_tab = p['peer_wq'], p['peer_keys'].astype(F32), p['peer_u'], p['peer_v']

    def token_block(xb):
        T = xb.shape[0]
        q = (xb @ wq).astype(F32).reshape(T, PEER_HEADS, 2, PEER_DQ // 2)
        s = jnp.einsum('thcq,hckq->thck', q, keys)
        sv, si = lax.top_k(s, PEER_TOPK)
        cand = (sv[:, :, 0, :, None] + sv[:, :, 1, None, :]).reshape(T, PEER_HEADS, PEER_TOPK * PEER_TOPK)
        cidx = (si[:, :, 0, :, None] * PEER_NKEYS + si[:, :, 1, None, :]).reshape(T, PEER_HEADS, PEER_TOPK * PEER_TOPK)
        top_s, top_j = lax.top_k(cand, PEER_TOPK)
        eidx = jnp.take_along_axis(cidx, top_j, axis=-1)
        gate = jax.nn.softmax(top_s, axis=-1)
        act = jax.nn.gelu(jnp.einsum('td,thkd->thk', xb, u_tab[eidx]).astype(F32))
        return jnp.einsum('thk,thkd->td', (gate * act).astype(xb.dtype), v_tab[eidx])

    out = lax.map(token_block, h.reshape(-1, PEER_BLOCK, D))
    return out.reshape(Bn, L, D).astype(h.dtype)


def merge_groups(ys, g, dtype):
    outs, off = [], 0
    for y, w in zip(ys, GROUP_WIDTHS):
        outs.append(rmsnorm(y, g[off:off + w]).astype(dtype))
        off += w
    return jnp.concatenate(outs, axis=-1)


def token_mixers(h_c, h_l, p, need_ctx):
    zc = split_cols(h_c @ p['w_in'])
    zl = split_cols(h_l @ p['w_in'])
    a_c, a_l = mixer_rglru(zc[0:2], zl[0:2], p, need_ctx)
    b_c, b_l = mixer_rwkv(zc[2:8], zl[2:8], p, need_ctx)
    m_c, m_l = mixer_mlstm(zc[8:13], zl[8:13], p, need_ctx)
    d_l = mixer_hyena(zl[13:16], p)
    o_l = merge_groups([a_l, b_l, m_l, d_l], p['grp_g'], h_l.dtype) @ p['w_out']
    if not need_ctx:
        return None, o_l
    d_c = mixer_hyena(zc[13:16], p)
    o_c = merge_groups([a_c, b_c, m_c, d_c], p['grp_g'], h_c.dtype) @ p['w_out']
    return o_c, o_l


def trunk_layer(x_l, x_c, c, c_ctx, p, need_ctx):
    Bn = c.shape[0]
    mod_l = (jax.nn.silu(c) @ p['ada_w'] + p['ada_b']).reshape(Bn, 6, 1, D_MODEL)
    mod_c = (jax.nn.silu(c_ctx) @ p['ada_w'] + p['ada_b']).reshape(6, 1, 1, D_MODEL)
    h_l = modulate(rmsnorm(x_l, p['norm1_g']), mod_l[:, 0], mod_l[:, 1])
    h_c = modulate(rmsnorm(x_c, p['norm1_g']), mod_c[0], mod_c[1])
    o_c, o_l = token_mixers(h_c, h_l, p, need_ctx)
    x_l = x_l + mod_l[:, 2] * o_l
    x_l = x_l + mod_l[:, 5] * peer_ffn(modulate(rmsnorm(x_l, p['norm2_g']), mod_l[:, 3], mod_l[:, 4]), p)
    if need_ctx:
        x_c = x_c + mod_c[2] * o_c
        x_c = x_c + mod_c[5] * peer_ffn(modulate(rmsnorm(x_c, p['norm2_g']), mod_c[3], mod_c[4]), p)
    return x_l, x_c


def setup_inputs(seed: int = 0) -> dict:
    key = jax.random.key(seed)
    ks = iter(jax.random.split(key, 64))

    def nrm(shape, s):
        return s * jax.random.normal(next(ks), shape, F32)

    def gain(shape):
        return 1.0 + nrm(shape, 0.02)

    L_ = DEPTH
    u_lam = jax.random.uniform(next(ks), (L_, 2, W_LRU), F32, 0.9, 0.999)
    s_lam = u_lam ** (1.0 / LRU_C)
    w0 = jnp.linspace(-6.0, -1.0, W_RWKV, dtype=F32) + nrm((L_, 2, W_RWKV), 0.1)
    gate_b = jnp.stack([nrm((L_, 2, H_MLSTM), 0.1),
                        jnp.linspace(3.0, 6.0, H_MLSTM, dtype=F32) + nrm((L_, 2, H_MLSTM), 0.1)], axis=2)
    return dict(
        x=nrm((BATCH, SEQ, D_MODEL), 1.0),
        c=nrm((BATCH, D_MODEL), 1.0),
        ctx=nrm((BATCH, CTX_LEN, D_MODEL), 1.0),
        c_ctx=nrm((D_MODEL,), 1.0),
        ada_w=nrm((L_, D_MODEL, 6 * D_MODEL), 0.5 * D_MODEL ** -0.5),
        ada_b=nrm((L_, 6 * D_MODEL), 0.01),
        norm1_g=gain((L_, D_MODEL)),
        norm2_g=gain((L_, D_MODEL)),
        w_in=nrm((L_, D_MODEL, D_IN), D_MODEL ** -0.5),
        w_out=nrm((L_, D_MIX, D_MODEL), D_MIX ** -0.5),
        grp_g=gain((L_, D_MIX)),
        lru_conv_w=nrm((L_, CONV_W, W_LRU), CONV_W ** -0.5),
        lru_conv_b=nrm((L_, W_LRU), 0.01),
        lru_wr=nrm((L_, 2, H_LRU, HEAD_DIM, HEAD_DIM), HEAD_DIM ** -0.5),
        lru_br=nrm((L_, 2, W_LRU), 0.01),
        lru_wi=nrm((L_, 2, H_LRU, HEAD_DIM, HEAD_DIM), HEAD_DIM ** -0.5),
        lru_bi=nrm((L_, 2, W_LRU), 0.01),
        lru_lam=jnp.log(s_lam) - jnp.log1p(-s_lam),
        rwkv_mu=jax.random.uniform(next(ks), (L_, 3, 2, W_RWKV), F32, 0.0, 0.4),
        rwkv_w0=w0,
        rwkv_w2=nrm((L_, 2, RWKV_LORA_W, W_RWKV), 0.5 * RWKV_LORA_W ** -0.5),
        rwkv_a0=nrm((L_, 2, W_RWKV), 0.1),
        rwkv_a2=nrm((L_, 2, RWKV_LORA_A, W_RWKV), 0.5 * RWKV_LORA_A ** -0.5),
        rwkv_g2=nrm((L_, RWKV_LORA_G, W_RWKV), RWKV_LORA_G ** -0.5),
        rwkv_kk=1.0 + nrm((L_, W_RWKV), 0.1),
        rwkv_ka=1.0 + nrm((L_, W_RWKV), 0.1),
        rwkv_rk=nrm((L_, H_RWKV, HEAD_DIM), 0.1),
        rwkv_ln_g=gain((L_, W_RWKV)),
        rwkv_ln_b=nrm((L_, W_RWKV), 0.01),
        mlstm_conv_w=nrm((L_, CONV_W, 2 * W_MLSTM), CONV_W ** -0.5),
        mlstm_conv_b=nrm((L_, 2 * W_MLSTM), 0.01),
        mlstm_gate_b=gate_b,
        hy_conv_w=nrm((L_, HYENA_SHORT, 3 * W_HYENA), HYENA_SHORT ** -0.5),
        hy_conv_b=nrm((L_, 3 * W_HYENA), 0.01),
        hy_w1=nrm((L_, HYENA_EMB, HYENA_HID), HYENA_EMB ** -0.5),
        hy_b1=nrm((L_, HYENA_HID), 0.01),
        hy_w2=nrm((L_, HYENA_HID, HYENA_HID), HYENA_HID ** -0.5),
        hy_b2=nrm((L_, HYENA_HID), 0.01),
        hy_w3=nrm((L_, HYENA_HID, HYENA_ORDER * 2 * W_HYENA), HYENA_HID ** -0.5),
        hy_freq=1.0 + nrm((L_, 2, HYENA_HID), 0.1),
        hy_bias=nrm((L_, HYENA_ORDER, W_HYENA), 0.1),
        peer_wq=nrm((L_, D_MODEL, PEER_HEADS * PEER_DQ), D_MODEL ** -0.5),
        peer_keys=nrm((L_, PEER_HEADS, 2, PEER_NKEYS, PEER_DQ // 2), (PEER_DQ // 2) ** -0.5),
        peer_u=nrm((L_, PEER_EXPERTS, D_MODEL), D_MODEL ** -0.5),
        peer_v=nrm((L_, PEER_EXPERTS, D_MODEL), PEER_HEADS ** -0.5),
        final_g=gain((D_MODEL,)),
    )


def reference(x, c, ctx, c_ctx, ada_w, ada_b, norm1_g, norm2_g, w_in, w_out, grp_g,
              lru_conv_w, lru_conv_b, lru_wr, lru_br, lru_wi, lru_bi, lru_lam,
              rwkv_mu, rwkv_w0, rwkv_w2, rwkv_a0, rwkv_a2, rwkv_g2, rwkv_kk, rwkv_ka, rwkv_rk, rwkv_ln_g, rwkv_ln_b,
              mlstm_conv_w, mlstm_conv_b, mlstm_gate_b,
              hy_conv_w, hy_conv_b, hy_w1, hy_b1, hy_w2, hy_b2, hy_w3, hy_freq, hy_bias,
              peer_wq, peer_keys, peer_u, peer_v, final_g):
    x_l, x_c = x, ctx
    for i in range(DEPTH):
        p = dict(ada_w=ada_w[i], ada_b=ada_b[i], norm1_g=norm1_g[i], norm2_g=norm2_g[i], w_in=w_in[i],
                 w_out=w_out[i], grp_g=grp_g[i],
                 lru_conv_w=lru_conv_w[i], lru_conv_b=lru_conv_b[i], lru_wr=lru_wr[i], lru_br=lru_br[i],
                 lru_wi=lru_wi[i], lru_bi=lru_bi[i], lru_lam=lru_lam[i],
                 rwkv_mu=rwkv_mu[i], rwkv_w0=rwkv_w0[i], rwkv_w2=rwkv_w2[i], rwkv_a0=rwkv_a0[i],
                 rwkv_a2=rwkv_a2[i], rwkv_g2=rwkv_g2[i], rwkv_kk=rwkv_kk[i], rwkv_ka=rwkv_ka[i],
                 rwkv_rk=rwkv_rk[i], rwkv_ln_g=rwkv_ln_g[i], rwkv_ln_b=rwkv_ln_b[i],
                 mlstm_conv_w=mlstm_conv_w[i], mlstm_conv_b=mlstm_conv_b[i], mlstm_gate_b=mlstm_gate_b[i],
                 hy_conv_w=hy_conv_w[i], hy_conv_b=hy_conv_b[i], hy_w1=hy_w1[i], hy_b1=hy_b1[i],
                 hy_w2=hy_w2[i], hy_b2=hy_b2[i], hy_w3=hy_w3[i], hy_freq=hy_freq[i], hy_bias=hy_bias[i],
                 peer_wq=peer_wq[i], peer_keys=peer_keys[i], peer_u=peer_u[i], peer_v=peer_v[i])
        x_l, x_c = trunk_layer(x_l, x_c, c, c_ctx, p, i < DEPTH - 1)
    return rmsnorm(x_l, final_g)
```

```python
import math
import jax, jax.numpy as jnp
from jax import lax
import numpy as np
from jax.experimental import pallas as pl
from jax.experimental.pallas import tpu as pltpu

D_MODEL = 1024
BATCH = 16
SEQ = 2048
DEPTH = 2

GRID_W = 64
CTX_LEN = 256
F32 = jnp.float32
EPS = 1e-6
HEAD_DIM = 64
D_MIX = D_MODEL
W_LRU = D_MIX // 4
W_RWKV = D_MIX // 4
W_MLSTM = D_MIX // 4
W_HYENA = D_MIX - W_LRU - W_RWKV - W_MLSTM
GROUP_WIDTHS = (W_LRU, W_RWKV, W_MLSTM, W_HYENA)
H_LRU = W_LRU // HEAD_DIM
H_RWKV = W_RWKV // HEAD_DIM
H_MLSTM = W_MLSTM // HEAD_DIM
CONV_W = 4
LRU_C = 8.0
RWKV_LORA_W = 32
RWKV_LORA_A = 32
RWKV_LORA_G = 64
RWKV_GN_EPS = 64e-5
MLSTM_CHUNK = 64
HYENA_ORDER = 2
HYENA_SHORT = 3
HYENA_EMB = 33
HYENA_HID = 64
HYENA_TARGET = 1e-2
HYENA_FAST = 0.3
HYENA_SLOW = 1.5
PEER_HEADS = 8
PEER_NKEYS = 128
PEER_EXPERTS = PEER_NKEYS * PEER_NKEYS
PEER_TOPK = 16
PEER_DQ = 256
PEER_BLOCK = 128
IN_SPLITS = (W_LRU, W_LRU, W_RWKV, W_RWKV, W_RWKV, RWKV_LORA_W, RWKV_LORA_A, RWKV_LORA_G, W_MLSTM, W_MLSTM, W_MLSTM, W_MLSTM, 4 * H_MLSTM, W_HYENA, W_HYENA, W_HYENA)
D_IN = sum(IN_SPLITS)


def rmsnorm(x, g):
    xf = x.astype(F32)
    y = xf * lax.rsqrt(jnp.mean(xf * xf, axis=-1, keepdims=True) + EPS)
    return (y * g.astype(F32)).astype(x.dtype)


def modulate(h, shift, scale):
    return h * (1 + scale) + shift


def flip_if(a, d, axis=1):
    return jnp.flip(a, axis) if d == 1 else a


def split_cols(z):
    offs = np.cumsum(IN_SPLITS)[:-1].tolist()
    return jnp.split(z, offs, axis=-1)


def dwconv(x, w, b, pad_left):
    K = w.shape[0]
    y = lax.conv_general_dilated(x, w[:, None, :].astype(x.dtype), (1,), [(pad_left, K - 1 - pad_left)],
                                 dimension_numbers=('NWC', 'WIO', 'NWC'), feature_group_count=x.shape[-1])
    return y + b.astype(x.dtype)


def token_shift(x, mu):
    prev = jnp.pad(x, ((0, 0), (1, 0), (0, 0)))[:, :-1]
    nxt = jnp.pad(x, ((0, 0), (0, 1), (0, 0)))[:, 1:]
    return x + mu[0] * (prev - x) + mu[1] * (nxt - x)


def split_heads(t, h):
    Bn, L, W = t.shape
    return t.reshape(Bn, L, h, W // h)


def to_colmajor(a):
    Bn, L, C = a.shape
    rows = L // GRID_W
    return a.reshape(Bn, rows, GRID_W, C).transpose(0, 2, 1, 3).reshape(Bn, L, C)


def from_colmajor(a):
    Bn, L, C = a.shape
    rows = L // GRID_W
    return a.reshape(Bn, GRID_W, rows, C).transpose(0, 2, 1, 3).reshape(Bn, L, C)


def linear_recurrence(a, b, h0):
    def comb(l, r):
        return (l[0] * r[0], r[0] * l[1] + r[1])
    A, H = lax.associative_scan(comb, (a, b), axis=1)
    return H + A * h0[:, None, :]


def rglru_scan(xc, p, d, h0):
    Bn, L, W = xc.shape
    xh = xc.reshape(Bn, L, H_LRU, HEAD_DIM)
    r = jax.nn.sigmoid(jnp.einsum('blhi,hij->blhj', xh, p['lru_wr'][d]).reshape(Bn, L, W) + p['lru_br'][d])
    i = jax.nn.sigmoid(jnp.einsum('blhi,hij->blhj', xh, p['lru_wi'][d]).reshape(Bn, L, W) + p['lru_bi'][d])
    log_a = -LRU_C * r * jax.nn.softplus(-p['lru_lam'][d])
    b = jnp.sqrt(-jnp.expm1(2.0 * log_a)) * (i * xc)
    return linear_recurrence(jnp.exp(log_a), b, h0)


def mixer_rglru(f_c, f_l, p, need_ctx):
    (x_c, g_c), (x_l, g_l) = f_c, f_l
    xc_c = dwconv(x_c, p['lru_conv_w'], p['lru_conv_b'], CONV_W // 2).astype(F32)
    xc_l = dwconv(x_l, p['lru_conv_w'], p['lru_conv_b'], CONV_W // 2).astype(F32)
    h0 = jnp.zeros((xc_l.shape[0], W_LRU), F32)
    hs_c, hs_l = [], []
    for d in range(2):
        hc = rglru_scan(flip_if(xc_c, d), p, d, h0)
        hl = rglru_scan(flip_if(xc_l, d), p, d, hc[:, -1])
        hs_c.append(flip_if(hc, d))
        hs_l.append(flip_if(hl, d))
    y_l = jax.nn.gelu(g_l.astype(F32)) * (hs_l[0] + hs_l[1])
    y_c = jax.nn.gelu(g_c.astype(F32)) * (hs_c[0] + hs_c[1]) if need_ctx else None
    return y_c, y_l


def rwkv_shift(f, p):
    r, k, v, zw, za, zg = [t.astype(F32) for t in f]
    mu = p['rwkv_mu']
    return (token_shift(r, mu[0]), token_shift(k, mu[1]), token_shift(v, mu[2]), zw, za, zg)


def rwkv_dir_inputs(f, p, d):
    r, k, v, zw, za, _ = f
    w_log = -jax.nn.softplus(-(p['rwkv_w0'][d] + jnp.tanh(zw) @ p['rwkv_w2'][d])) - 0.5
    decay = jnp.exp(-jnp.exp(w_log))
    a = jax.nn.sigmoid(p['rwkv_a0'][d] + za @ p['rwkv_a2'][d])
    kk = split_heads(k * p['rwkv_kk'], H_RWKV)
    kk = kk / jnp.maximum(jnp.linalg.norm(kk, axis=-1, keepdims=True), 1e-12)
    kd = k * (1 + (a - 1) * p['rwkv_ka'])
    return [split_heads(r, H_RWKV), split_heads(decay, H_RWKV), split_heads(kd, H_RWKV),
            split_heads(v, H_RWKV), kk, kk * split_heads(a, H_RWKV)]


def rwkv7_scan(ins, s0):
    def step(S, inp):
        r, w, k, v, kk, kka = inp
        sa = jnp.einsum('bhvk,bhk->bhv', S, kk)
        S = S * w[:, :, None, :] - sa[..., None] * kka[:, :, None, :] + v[..., None] * k[:, :, None, :]
        return S, jnp.einsum('bhvk,bhk->bhv', S, r)
    S, ys = lax.scan(step, s0, tuple(jnp.moveaxis(t, 1, 0) for t in ins))
    return jnp.moveaxis(ys, 0, 1), S


def rwkv_bonus(ins, p):
    r, _, kd, v = ins[:4]
    return jnp.sum(r * kd * p['rwkv_rk'], axis=-1, keepdims=True) * v


def rwkv_out(y, bonus, zg, p):
    Bn, L, H, N = y.shape
    mu = jnp.mean(y, axis=-1, keepdims=True)
    var = jnp.mean(jnp.square(y - mu), axis=-1, keepdims=True)
    yn = ((y - mu) * lax.rsqrt(var + RWKV_GN_EPS)).reshape(Bn, L, H * N) * p['rwkv_ln_g'] + p['rwkv_ln_b']
    g = jax.nn.sigmoid(zg) @ p['rwkv_g2']
    return (yn + bonus.reshape(Bn, L, H * N)) * g


def mixer_rwkv(f_c, f_l, p, need_ctx):
    fc, fl = rwkv_shift(f_c, p), rwkv_shift(f_l, p)
    s0 = jnp.zeros((fl[0].shape[0], H_RWKV, HEAD_DIM, HEAD_DIM), F32)
    y_c, y_l = [], []
    for d in range(2):
        ic, il = rwkv_dir_inputs(fc, p, d), rwkv_dir_inputs(fl, p, d)
        oc, s_ctx = rwkv7_scan([flip_if(t, d) for t in ic], s0)
        ol, _ = rwkv7_scan([flip_if(t, d) for t in il], s_ctx)
        y_l.append((flip_if(ol, d), rwkv_bonus(il, p)))
        if need_ctx:
            y_c.append((flip_if(oc, d), rwkv_bonus(ic, p)))
    out_l = rwkv_out(y_l[0][0] + y_l[1][0], y_l[0][1] + y_l[1][1], fl[5], p)
    out_c = rwkv_out(y_c[0][0] + y_c[1][0], y_c[0][1] + y_c[1][1], fc[5], p) if need_ctx else None
    return out_c, out_l


def mlstm_prep(f, p):
    q, k, v, o, gz = f
    Bn, L, _ = q.shape
    qk = jax.nn.silu(dwconv(jnp.concatenate([q, k], axis=-1), p['mlstm_conv_w'], p['mlstm_conv_b'], CONV_W // 2)).astype(F32)
    q, k = jnp.split(qk, 2, axis=-1)
    heads = lambda t: t.reshape(Bn, L, H_MLSTM, HEAD_DIM).transpose(0, 2, 1, 3)
    gates = (gz.astype(F32).reshape(Bn, L, 2, 2, H_MLSTM) + p['mlstm_gate_b']).transpose(2, 3, 0, 4, 1)
    return heads(q) * HEAD_DIM ** -0.5, heads(k), heads(v.astype(F32)), o.astype(F32), gates


def mlstm_chunkwise(q, k, v, ig, lf, state):
    Bn, H, L, N = q.shape
    T = MLSTM_CHUNK
    nc = L // T
    ch = lambda a: jnp.moveaxis(a.reshape((Bn, H, nc, T) + a.shape[3:]), 2, 0)
    mask = jnp.tril(jnp.ones((T, T), dtype=bool))

    def step(carry, inp):
        C, n, m = carry
        qc, kc, vc, ic, fc = inp
        b = jnp.cumsum(fc, axis=-1)
        logd = jnp.where(mask, b[..., :, None] - b[..., None, :] + ic[..., None, :], -jnp.inf)
        inter = b + m[..., None]
        mt = jnp.maximum(inter, jnp.max(logd, axis=-1))
        s = jnp.einsum('bhtn,bhsn->bhts', qc, kc) * jnp.exp(logd - mt[..., None])
        e_inter = jnp.exp(inter - mt)
        num = jnp.einsum('bhts,bhsn->bhtn', s, vc) + e_inter[..., None] * jnp.einsum('bhvk,bhtk->bhtv', C, qc)
        den = jnp.sum(s, axis=-1) + e_inter * jnp.einsum('bhk,bhtk->bht', n, qc)
        h = num / jnp.maximum(jnp.abs(den), jnp.exp(-mt))[..., None]
        bT = b[..., -1]
        logw = bT[..., None] - b + ic
        m_new = jnp.maximum(bT + m, jnp.max(logw, axis=-1))
        wgt = jnp.exp(logw - m_new[..., None])
        dec = jnp.exp(bT + m - m_new)
        C = dec[..., None, None] * C + jnp.einsum('bhs,bhsv,bhsk->bhvk', wgt, vc, kc)
        n = dec[..., None] * n + jnp.einsum('bhs,bhsk->bhk', wgt, kc)
        return (C, n, m_new), h

    state, hs = lax.scan(step, state, (ch(q), ch(k), ch(v), ch(ig), ch(lf)))
    return jnp.moveaxis(hs, 0, 2).reshape(Bn, H, L, N), state


def mixer_mlstm(f_c, f_l, p, need_ctx):
    f_l = [to_colmajor(t) for t in f_l]
    qc, kc, vc, oc, gc = mlstm_prep(f_c, p)
    ql, kl, vl, ol, gl = mlstm_prep(f_l, p)
    Bn = ql.shape[0]
    st0 = (jnp.zeros((Bn, H_MLSTM, HEAD_DIM, HEAD_DIM), F32), jnp.zeros((Bn, H_MLSTM, HEAD_DIM), F32),
           jnp.zeros((Bn, H_MLSTM), F32))
    h_c, h_l = [], []
    for d in range(2):
        hc, st = mlstm_chunkwise(flip_if(qc, d, 2), flip_if(kc, d, 2), flip_if(vc, d, 2), flip_if(gc[d, 0], d, 2),
                                 jax.nn.log_sigmoid(flip_if(gc[d, 1], d, 2)), st0)
        hl, _ = mlstm_chunkwise(flip_if(ql, d, 2), flip_if(kl, d, 2), flip_if(vl, d, 2), flip_if(gl[d, 0], d, 2),
                                jax.nn.log_sigmoid(flip_if(gl[d, 1], d, 2)), st)
        h_c.append(flip_if(hc, d, 2))
        h_l.append(flip_if(hl, d, 2))
    merge_heads = lambda h: h.transpose(0, 2, 1, 3).reshape(h.shape[0], h.shape[2], W_MLSTM)
    y_l = from_colmajor(jax.nn.sigmoid(ol) * merge_heads(h_l[0] + h_l[1]))
    y_c = jax.nn.sigmoid(oc) * merge_heads(h_c[0] + h_c[1]) if need_ctx else None
    return y_c, y_l


def hyena_spectrum(L, p):
    pos = jnp.arange(L, dtype=F32)
    t = pos / (L - 1)
    bands = (HYENA_EMB - 1) // 2
    freqs = jnp.linspace(1e-4, bands - 1, bands, dtype=F32)
    ang = (2 * math.pi / L) * pos[:, None] * freqs[None, :]
    z = jnp.concatenate([t[:, None], jnp.cos(ang), -jnp.sin(ang)], axis=-1)
    h = jnp.sin(p['hy_freq'][0] * (z @ p['hy_w1'] + p['hy_b1']))
    h = jnp.sin(p['hy_freq'][1] * (h @ p['hy_w2'] + p['hy_b2']))
    h = (h @ p['hy_w3']).astype(F32).reshape(L, HYENA_ORDER, 2, W_HYENA)
    deltas = jnp.abs(jnp.linspace(math.log(HYENA_TARGET) / HYENA_SLOW, math.log(HYENA_TARGET) / HYENA_FAST,
                                  W_HYENA, dtype=F32))
    h = h * jnp.exp(-t[:, None, None, None] * deltas)
    fwd, bwd = h[:, :, 0], h[:, :, 1]
    two = jnp.concatenate([fwd, jnp.zeros_like(fwd[:1]), jnp.flip(bwd[1:], axis=0)], axis=0)
    two = two / (jnp.sum(jnp.abs(two), axis=0, keepdims=True) + EPS)
    return jnp.fft.rfft(two, axis=0)


def long_conv(u, spec, bias):
    L = u.shape[1]
    y = jnp.fft.irfft(jnp.fft.rfft(u, n=2 * L, axis=1) * spec, n=2 * L, axis=1)[:, :L]
    return y + u * bias


def mixer_hyena(f, p):
    u = dwconv(jnp.concatenate(f, axis=-1), p['hy_conv_w'], p['hy_conv_b'], HYENA_SHORT // 2).astype(F32)
    v, x1, x2 = jnp.split(u, 3, axis=-1)
    spec = hyena_spectrum(u.shape[1], p)
    z = x1 * long_conv(v, spec[:, 0], p['hy_bias'][0])
    return x2 * long_conv(z, spec[:, 1], p['hy_bias'][1])


def peer_ffn(h, p):
    Bn, L, D = h.shape
    wq, keys, u_tab, v_tab = p['peer_wq'], p['peer_keys'].astype(F32), p['peer_u'], p['peer_v']

    def token_block(xb):
        T = xb.shape[0]
        q = (xb @ wq).astype(F32).reshape(T, PEER_HEADS, 2, PEER_DQ // 2)
        s = jnp.einsum('thcq,hckq->thck', q, keys)
        sv, si = lax.top_k(s, PEER_TOPK)
        cand = (sv[:, :, 0, :, None] + sv[:, :, 1, None, :]).reshape(T, PEER_HEADS, PEER_TOPK * PEER_TOPK)
        cidx = (si[:, :, 0, :, None] * PEER_NKEYS + si[:, :, 1, None, :]).reshape(T, PEER_HEADS, PEER_TOPK * PEER_TOPK)
        top_s, top_j = lax.top_k(cand, PEER_TOPK)
        eidx = jnp.take_along_axis(cidx, top_j, axis=-1)
        gate = jax.nn.softmax(top_s, axis=-1)
        act = jax.nn.gelu(jnp.einsum('td,thkd->thk', xb, u_tab[eidx]).astype(F32))
        return jnp.einsum('thk,thkd->td', (gate * act).astype(xb.dtype), v_tab[eidx])

    out = lax.map(token_block, h.reshape(-1, PEER_BLOCK, D))
    return out.reshape(Bn, L, D).astype(h.dtype)


def merge_groups(ys, g, dtype):
    outs, off = [], 0
    for y, w in zip(ys, GROUP_WIDTHS):
        outs.append(rmsnorm(y, g[off:off + w]).astype(dtype))
        off += w
    return jnp.concatenate(outs, axis=-1)


def token_mixers(h_c, h_l, p, need_ctx):
    zc = split_cols(h_c @ p['w_in'])
    zl = split_cols(h_l @ p['w_in'])
    a_c, a_l = mixer_rglru(zc[0:2], zl[0:2], p, need_ctx)
    b_c, b_l = mixer_rwkv(zc[2:8], zl[2:8], p, need_ctx)
    m_c, m_l = mixer_mlstm(zc[8:13], zl[8:13], p, need_ctx)
    d_l = mixer_hyena(zl[13:16], p)
    o_l = merge_groups([a_l, b_l, m_l, d_l], p['grp_g'], h_l.dtype) @ p['w_out']
    if not need_ctx:
        return None, o_l
    d_c = mixer_hyena(zc[13:16], p)
    o_c = merge_groups([a_c, b_c, m_c, d_c], p['grp_g'], h_c.dtype) @ p['w_out']
    return o_c, o_l


def trunk_layer(x_l, x_c, c, c_ctx, p, need_ctx):
    Bn = c.shape[0]
    mod_l = (jax.nn.silu(c) @ p['ada_w'] + p['ada_b']).reshape(Bn, 6, 1, D_MODEL)
    mod_c = (jax.nn.silu(c_ctx) @ p['ada_w'] + p['ada_b']).reshape(6, 1, 1, D_MODEL)
    h_l = modulate(rmsnorm(x_l, p['norm1_g']), mod_l[:, 0], mod_l[:, 1])
    h_c = modulate(rmsnorm(x_c, p['norm1_g']), mod_c[0], mod_c[1])
    o_c, o_l = token_mixers(h_c, h_l, p, need_ctx)
    x_l = x_l + mod_l[:, 2] * o_l
    x_l = x_l + mod_l[:, 5] * peer_ffn(modulate(rmsnorm(x_l, p['norm2_g']), mod_l[:, 3], mod_l[:, 4]), p)
    if need_ctx:
        x_c = x_c + mod_c[2] * o_c
        x_c = x_c + mod_c[5] * peer_ffn(modulate(rmsnorm(x_c, p['norm2_g']), mod_c[3], mod_c[4]), p)
    return x_l, x_c


def _final_norm_body(x_ref, g_ref, o_ref):
    xf = x_ref[...]
    y = xf * lax.rsqrt(jnp.mean(xf * xf, axis=-1, keepdims=True) + EPS)
    o_ref[...] = y * g_ref[...]


def final_rmsnorm(x, g):
    Bn, L, D = x.shape
    rows = Bn * L
    tile = 1024
    out = pl.pallas_call(
        _final_norm_body,
        grid=(rows // tile,),
        in_specs=[pl.BlockSpec((tile, D), lambda i: (i, 0)), pl.BlockSpec((1, D), lambda i: (0, 0))],
        out_specs=pl.BlockSpec((tile, D), lambda i: (i, 0)),
        out_shape=jax.ShapeDtypeStruct((rows, D), x.dtype),
        name="final_rmsnorm",
    )(x.reshape(rows, D), g.reshape(1, D))
    return out.reshape(Bn, L, D)


def kernel(x, c, ctx, c_ctx, ada_w, ada_b, norm1_g, norm2_g, w_in, w_out, grp_g,
           lru_conv_w, lru_conv_b, lru_wr, lru_br, lru_wi, lru_bi, lru_lam,
           rwkv_mu, rwkv_w0, rwkv_w2, rwkv_a0, rwkv_a2, rwkv_g2, rwkv_kk, rwkv_ka, rwkv_rk, rwkv_ln_g, rwkv_ln_b,
           mlstm_conv_w, mlstm_conv_b, mlstm_gate_b,
           hy_conv_w, hy_conv_b, hy_w1, hy_b1, hy_w2, hy_b2, hy_w3, hy_freq, hy_bias,
           peer_wq, peer_keys, peer_u, peer_v, final_g):
    x_l, x_c = x, ctx
    for i in range(DEPTH):
        p = dict(ada_w=ada_w[i], ada_b=ada_b[i], norm1_g=norm1_g[i], norm2_g=norm2_g[i], w_in=w_in[i],
                 w_out=w_out[i], grp_g=grp_g[i],
                 lru_conv_w=lru_conv_w[i], lru_conv_b=lru_conv_b[i], lru_wr=lru_wr[i], lru_br=lru_br[i],
                 lru_wi=lru_wi[i], lru_bi=lru_bi[i], lru_lam=lru_lam[i],
                 rwkv_mu=rwkv_mu[i], rwkv_w0=rwkv_w0[i], rwkv_w2=rwkv_w2[i], rwkv_a0=rwkv_a0[i],
                 rwkv_a2=rwkv_a2[i], rwkv_g2=rwkv_g2[i], rwkv_kk=rwkv_kk[i], rwkv_ka=rwkv_ka[i],
                 rwkv_rk=rwkv_rk[i], rwkv_ln_g=rwkv_ln_g[i], rwkv_ln_b=rwkv_ln_b[i],
                 mlstm_conv_w=mlstm_conv_w[i], mlstm_conv_b=mlstm_conv_b[i], mlstm_gate_b=mlstm_gate_b[i],
                 hy_conv_w=hy_conv_w[i], hy_conv_b=hy_conv_b[i], hy_w1=hy_w1[i], hy_b1=hy_b1[i],
                 hy_w2=hy_w2[i], hy_b2=hy_b2[i], hy_w3=hy_w3[i], hy_freq=hy_freq[i], hy_bias=hy_bias[i],
                 peer_wq=peer_wq[i], peer_keys=peer_keys[i], peer_u=peer_u[i], peer_v=peer_v[i])
        x_l, x_c = trunk_layer(x_l, x_c, c, c_ctx, p, i < DEPTH - 1)
    return final_rmsnorm(x_l, final_g)
```

```python
import math
import jax, jax.numpy as jnp
from jax import lax
import numpy as np
from jax.experimental import pallas as pl
from jax.experimental.pallas import tpu as pltpu
from jax.experimental.pallas import tpu_sc as plsc

D_MODEL = 1024
BATCH = 16
SEQ = 2048
DEPTH = 2

GRID_W = 64
CTX_LEN = 256
F32 = jnp.float32
EPS = 1e-6
HEAD_DIM = 64
D_MIX = D_MODEL
W_LRU = D_MIX // 4
W_RWKV = D_MIX // 4
W_MLSTM = D_MIX // 4
W_HYENA = D_MIX - W_LRU - W_RWKV - W_MLSTM
GROUP_WIDTHS = (W_LRU, W_RWKV, W_MLSTM, W_HYENA)
H_LRU = W_LRU // HEAD_DIM
H_RWKV = W_RWKV // HEAD_DIM
H_MLSTM = W_MLSTM // HEAD_DIM
CONV_W = 4
LRU_C = 8.0
RWKV_LORA_W = 32
RWKV_LORA_A = 32
RWKV_LORA_G = 64
RWKV_GN_EPS = 64e-5
MLSTM_CHUNK = 64
HYENA_ORDER = 2
HYENA_SHORT = 3
HYENA_EMB = 33
HYENA_HID = 64
HYENA_TARGET = 1e-2
HYENA_FAST = 0.3
HYENA_SLOW = 1.5
PEER_HEADS = 8
PEER_NKEYS = 128
PEER_EXPERTS = PEER_NKEYS * PEER_NKEYS
PEER_TOPK = 16
PEER_DQ = 256
PEER_BLOCK = 128
IN_SPLITS = (W_LRU, W_LRU, W_RWKV, W_RWKV, W_RWKV, RWKV_LORA_W, RWKV_LORA_A, RWKV_LORA_G, W_MLSTM, W_MLSTM, W_MLSTM, W_MLSTM, 4 * H_MLSTM, W_HYENA, W_HYENA, W_HYENA)
D_IN = sum(IN_SPLITS)


def rmsnorm(x, g):
    xf = x.astype(F32)
    y = xf * lax.rsqrt(jnp.mean(xf * xf, axis=-1, keepdims=True) + EPS)
    return (y * g.astype(F32)).astype(x.dtype)


def modulate(h, shift, scale):
    return h * (1 + scale) + shift


def flip_if(a, d, axis=1):
    return jnp.flip(a, axis) if d == 1 else a


def split_cols(z):
    offs = np.cumsum(IN_SPLITS)[:-1].tolist()
    return jnp.split(z, offs, axis=-1)


def dwconv(x, w, b, pad_left):
    K = w.shape[0]
    y = lax.conv_general_dilated(x, w[:, None, :].astype(x.dtype), (1,), [(pad_left, K - 1 - pad_left)],
                                 dimension_numbers=('NWC', 'WIO', 'NWC'), feature_group_count=x.shape[-1])
    return y + b.astype(x.dtype)


def token_shift(x, mu):
    prev = jnp.pad(x, ((0, 0), (1, 0), (0, 0)))[:, :-1]
    nxt = jnp.pad(x, ((0, 0), (0, 1), (0, 0)))[:, 1:]
    return x + mu[0] * (prev - x) + mu[1] * (nxt - x)


def split_heads(t, h):
    Bn, L, W = t.shape
    return t.reshape(Bn, L, h, W // h)


def to_colmajor(a):
    Bn, L, C = a.shape
    rows = L // GRID_W
    return a.reshape(Bn, rows, GRID_W, C).transpose(0, 2, 1, 3).reshape(Bn, L, C)


def from_colmajor(a):
    Bn, L, C = a.shape
    rows = L // GRID_W
    return a.reshape(Bn, GRID_W, rows, C).transpose(0, 2, 1, 3).reshape(Bn, L, C)


def linear_recurrence(a, b, h0):
    def comb(l, r):
        return (l[0] * r[0], r[0] * l[1] + r[1])
    A, H = lax.associative_scan(comb, (a, b), axis=1)
    return H + A * h0[:, None, :]


def rglru_scan(xc, p, d, h0):
    Bn, L, W = xc.shape
    xh = xc.reshape(Bn, L, H_LRU, HEAD_DIM)
    r = jax.nn.sigmoid(jnp.einsum('blhi,hij->blhj', xh, p['lru_wr'][d]).reshape(Bn, L, W) + p['lru_br'][d])
    i = jax.nn.sigmoid(jnp.einsum('blhi,hij->blhj', xh, p['lru_wi'][d]).reshape(Bn, L, W) + p['lru_bi'][d])
    log_a = -LRU_C * r * jax.nn.softplus(-p['lru_lam'][d])
    b = jnp.sqrt(-jnp.expm1(2.0 * log_a)) * (i * xc)
    return linear_recurrence(jnp.exp(log_a), b, h0)


def mixer_rglru(f_c, f_l, p, need_ctx):
    (x_c, g_c), (x_l, g_l) = f_c, f_l
    xc_c = dwconv(x_c, p['lru_conv_w'], p['lru_conv_b'], CONV_W // 2).astype(F32)
    xc_l = dwconv(x_l, p['lru_conv_w'], p['lru_conv_b'], CONV_W // 2).astype(F32)
    h0 = jnp.zeros((xc_l.shape[0], W_LRU), F32)
    hs_c, hs_l = [], []
    for d in range(2):
        hc = rglru_scan(flip_if(xc_c, d), p, d, h0)
        hl = rglru_scan(flip_if(xc_l, d), p, d, hc[:, -1])
        hs_c.append(flip_if(hc, d))
        hs_l.append(flip_if(hl, d))
    y_l = jax.nn.gelu(g_l.astype(F32)) * (hs_l[0] + hs_l[1])
    y_c = jax.nn.gelu(g_c.astype(F32)) * (hs_c[0] + hs_c[1]) if need_ctx else None
    return y_c, y_l


def rwkv_shift(f, p):
    r, k, v, zw, za, zg = [t.astype(F32) for t in f]
    mu = p['rwkv_mu']
    return (token_shift(r, mu[0]), token_shift(k, mu[1]), token_shift(v, mu[2]), zw, za, zg)


def rwkv_dir_inputs(f, p, d):
    r, k, v, zw, za, _ = f
    w_log = -jax.nn.softplus(-(p['rwkv_w0'][d] + jnp.tanh(zw) @ p['rwkv_w2'][d])) - 0.5
    decay = jnp.exp(-jnp.exp(w_log))
    a = jax.nn.sigmoid(p['rwkv_a0'][d] + za @ p['rwkv_a2'][d])
    kk = split_heads(k * p['rwkv_kk'], H_RWKV)
    kk = kk / jnp.maximum(jnp.linalg.norm(kk, axis=-1, keepdims=True), 1e-12)
    kd = k * (1 + (a - 1) * p['rwkv_ka'])
    return [split_heads(r, H_RWKV), split_heads(decay, H_RWKV), split_heads(kd, H_RWKV),
            split_heads(v, H_RWKV), kk, kk * split_heads(a, H_RWKV)]


def rwkv7_scan(ins, s0):
    def step(S, inp):
        r, w, k, v, kk, kka = inp
        sa = jnp.einsum('bhvk,bhk->bhv', S, kk)
        S = S * w[:, :, None, :] - sa[..., None] * kka[:, :, None, :] + v[..., None] * k[:, :, None, :]
        return S, jnp.einsum('bhvk,bhk->bhv', S, r)
    S, ys = lax.scan(step, s0, tuple(jnp.moveaxis(t, 1, 0) for t in ins))
    return jnp.moveaxis(ys, 0, 1), S


def rwkv_bonus(ins, p):
    r, _, kd, v = ins[:4]
    return jnp.sum(r * kd * p['rwkv_rk'], axis=-1, keepdims=True) * v


def rwkv_out(y, bonus, zg, p):
    Bn, L, H, N = y.shape
    mu = jnp.mean(y, axis=-1, keepdims=True)
    var = jnp.mean(jnp.square(y - mu), axis=-1, keepdims=True)
    yn = ((y - mu) * lax.rsqrt(var + RWKV_GN_EPS)).reshape(Bn, L, H * N) * p['rwkv_ln_g'] + p['rwkv_ln_b']
    g = jax.nn.sigmoid(zg) @ p['rwkv_g2']
    return (yn + bonus.reshape(Bn, L, H * N)) * g


def mixer_rwkv(f_c, f_l, p, need_ctx):
    fc, fl = rwkv_shift(f_c, p), rwkv_shift(f_l, p)
    s0 = jnp.zeros((fl[0].shape[0], H_RWKV, HEAD_DIM, HEAD_DIM), F32)
    y_c, y_l = [], []
    for d in range(2):
        ic, il = rwkv_dir_inputs(fc, p, d), rwkv_dir_inputs(fl, p, d)
        oc, s_ctx = rwkv7_scan([flip_if(t, d) for t in ic], s0)
        ol, _ = rwkv7_scan([flip_if(t, d) for t in il], s_ctx)
        y_l.append((flip_if(ol, d), rwkv_bonus(il, p)))
        if need_ctx:
            y_c.append((flip_if(oc, d), rwkv_bonus(ic, p)))
    out_l = rwkv_out(y_l[0][0] + y_l[1][0], y_l[0][1] + y_l[1][1], fl[5], p)
    out_c = rwkv_out(y_c[0][0] + y_c[1][0], y_c[0][1] + y_c[1][1], fc[5], p) if need_ctx else None
    return out_c, out_l


def mlstm_prep(f, p):
    q, k, v, o, gz = f
    Bn, L, _ = q.shape
    qk = jax.nn.silu(dwconv(jnp.concatenate([q, k], axis=-1), p['mlstm_conv_w'], p['mlstm_conv_b'], CONV_W // 2)).astype(F32)
    q, k = jnp.split(qk, 2, axis=-1)
    heads = lambda t: t.reshape(Bn, L, H_MLSTM, HEAD_DIM).transpose(0, 2, 1, 3)
    gates = (gz.astype(F32).reshape(Bn, L, 2, 2, H_MLSTM) + p['mlstm_gate_b']).transpose(2, 3, 0, 4, 1)
    return heads(q) * HEAD_DIM ** -0.5, heads(k), heads(v.astype(F32)), o.astype(F32), gates


def mlstm_chunkwise(q, k, v, ig, lf, state):
    Bn, H, L, N = q.shape
    T = MLSTM_CHUNK
    nc = L // T
    ch = lambda a: jnp.moveaxis(a.reshape((Bn, H, nc, T) + a.shape[3:]), 2, 0)
    mask = jnp.tril(jnp.ones((T, T), dtype=bool))

    def step(carry, inp):
        C, n, m = carry
        qc, kc, vc, ic, fc = inp
        b = jnp.cumsum(fc, axis=-1)
        logd = jnp.where(mask, b[..., :, None] - b[..., None, :] + ic[..., None, :], -jnp.inf)
        inter = b + m[..., None]
        mt = jnp.maximum(inter, jnp.max(logd, axis=-1))
        s = jnp.einsum('bhtn,bhsn->bhts', qc, kc) * jnp.exp(logd - mt[..., None])
        e_inter = jnp.exp(inter - mt)
        num = jnp.einsum('bhts,bhsn->bhtn', s, vc) + e_inter[..., None] * jnp.einsum('bhvk,bhtk->bhtv', C, qc)
        den = jnp.sum(s, axis=-1) + e_inter * jnp.einsum('bhk,bhtk->bht', n, qc)
        h = num / jnp.maximum(jnp.abs(den), jnp.exp(-mt))[..., None]
        bT = b[..., -1]
        logw = bT[..., None] - b + ic
        m_new = jnp.maximum(bT + m, jnp.max(logw, axis=-1))
        wgt = jnp.exp(logw - m_new[..., None])
        dec = jnp.exp(bT + m - m_new)
        C = dec[..., None, None] * C + jnp.einsum('bhs,bhsv,bhsk->bhvk', wgt, vc, kc)
        n = dec[..., None] * n + jnp.einsum('bhs,bhsk->bhk', wgt, kc)
        return (C, n, m_new), h

    state, hs = lax.scan(step, state, (ch(q), ch(k), ch(v), ch(ig), ch(lf)))
    return jnp.moveaxis(hs, 0, 2).reshape(Bn, H, L, N), state


def mixer_mlstm(f_c, f_l, p, need_ctx):
    f_l = [to_colmajor(t) for t in f_l]
    qc, kc, vc, oc, gc = mlstm_prep(f_c, p)
    ql, kl, vl, ol, gl = mlstm_prep(f_l, p)
    Bn = ql.shape[0]
    st0 = (jnp.zeros((Bn, H_MLSTM, HEAD_DIM, HEAD_DIM), F32), jnp.zeros((Bn, H_MLSTM, HEAD_DIM), F32),
           jnp.zeros((Bn, H_MLSTM), F32))
    h_c, h_l = [], []
    for d in range(2):
        hc, st = mlstm_chunkwise(flip_if(qc, d, 2), flip_if(kc, d, 2), flip_if(vc, d, 2), flip_if(gc[d, 0], d, 2),
                                 jax.nn.log_sigmoid(flip_if(gc[d, 1], d, 2)), st0)
        hl, _ = mlstm_chunkwise(flip_if(ql, d, 2), flip_if(kl, d, 2), flip_if(vl, d, 2), flip_if(gl[d, 0], d, 2),
                                jax.nn.log_sigmoid(flip_if(gl[d, 1], d, 2)), st)
        h_c.append(flip_if(hc, d, 2))
        h_l.append(flip_if(hl, d, 2))
    merge_heads = lambda h: h.transpose(0, 2, 1, 3).reshape(h.shape[0], h.shape[2], W_MLSTM)
    y_l = from_colmajor(jax.nn.sigmoid(ol) * merge_heads(h_l[0] + h_l[1]))
    y_c = jax.nn.sigmoid(oc) * merge_heads(h_c[0] + h_c[1]) if need_ctx else None
    return y_c, y_l


def hyena_spectrum(L, p):
    pos = jnp.arange(L, dtype=F32)
    t = pos / (L - 1)
    bands = (HYENA_EMB - 1) // 2
    freqs = jnp.linspace(1e-4, bands - 1, bands, dtype=F32)
    ang = (2 * math.pi / L) * pos[:, None] * freqs[None, :]
    z = jnp.concatenate([t[:, None], jnp.cos(ang), -jnp.sin(ang)], axis=-1)
    h = jnp.sin(p['hy_freq'][0] * (z @ p['hy_w1'] + p['hy_b1']))
    h = jnp.sin(p['hy_freq'][1] * (h @ p['hy_w2'] + p['hy_b2']))
    h = (h @ p['hy_w3']).astype(F32).reshape(L, HYENA_ORDER, 2, W_HYENA)
    deltas = jnp.abs(jnp.linspace(math.log(HYENA_TARGET) / HYENA_SLOW, math.log(HYENA_TARGET) / HYENA_FAST,
                                  W_HYENA, dtype=F32))
    h = h * jnp.exp(-t[:, None, None, None] * deltas)
    fwd, bwd = h[:, :, 0], h[:, :, 1]
    two = jnp.concatenate([fwd, jnp.zeros_like(fwd[:1]), jnp.flip(bwd[1:], axis=0)], axis=0)
    two = two / (jnp.sum(jnp.abs(two), axis=0, keepdims=True) + EPS)
    return jnp.fft.rfft(two, axis=0)


def long_conv(u, spec, bias):
    L = u.shape[1]
    y = jnp.fft.irfft(jnp.fft.rfft(u, n=2 * L, axis=1) * spec, n=2 * L, axis=1)[:, :L]
    return y + u * bias


def mixer_hyena(f, p):
    u = dwconv(jnp.concatenate(f, axis=-1), p['hy_conv_w'], p['hy_conv_b'], HYENA_SHORT // 2).astype(F32)
    v, x1, x2 = jnp.split(u, 3, axis=-1)
    spec = hyena_spectrum(u.shape[1], p)
    z = x1 * long_conv(v, spec[:, 0], p['hy_bias'][0])
    return x2 * long_conv(z, spec[:, 1], p['hy_bias'][1])


PEER_SEL = PEER_HEADS * PEER_TOPK
PEER_HALF = D_MODEL // 2
PEER_CHUNK = 4096
PEER_TT = 16
SC_GATHER_ROWS = 64


def pack_bf16_halves(tab):
    bits = lax.bitcast_convert_type(tab.astype(jnp.bfloat16), jnp.uint16).astype(jnp.uint32)
    half = tab.shape[1] // 2
    return bits[:, :half] | (bits[:, half:] << 16)


def sc_gather_pair(u_tab, v_tab, idx):
    n = idx.shape[0]
    W = u_tab.shape[1]
    info = plsc.get_sparse_core_info()
    nc, ns = info.num_cores, info.num_subcores
    per_w = n // (nc * ns)
    G = SC_GATHER_ROWS
    assert per_w * nc * ns == n and per_w % G == 0
    nsteps = per_w // G
    mesh = plsc.VectorSubcoreMesh(core_axis_name="c", subcore_axis_name="s")
    out = jax.ShapeDtypeStruct((n, W), u_tab.dtype)

    def body(u_hbm, v_hbm, idx_hbm, uo_hbm, vo_hbm, idx_v, urows, vrows, usem, vsem):
        wid = lax.axis_index("s") * nc + lax.axis_index("c")
        base = wid * per_w
        pltpu.sync_copy(idx_hbm.at[pl.ds(base, per_w)], idx_v)

        @pl.loop(0, nsteps)
        def _(i):
            off = pl.multiple_of(i * G, G)
            ids = idx_v.at[pl.ds(off, G)]
            cu = pltpu.async_copy(u_hbm.at[ids], urows, usem)
            cv = pltpu.async_copy(v_hbm.at[ids], vrows, vsem)
            cu.wait()
            pltpu.sync_copy(urows, uo_hbm.at[pl.ds(base + off, G)])
            cv.wait()
            pltpu.sync_copy(vrows, vo_hbm.at[pl.ds(base + off, G)])

    fn = pl.kernel(body, out_type=(out, out), mesh=mesh,
                   scratch_types=[pltpu.VMEM((per_w,), jnp.int32),
                                  pltpu.VMEM((G, W), u_tab.dtype), pltpu.VMEM((G, W), u_tab.dtype),
                                  pltpu.SemaphoreType.DMA, pltpu.SemaphoreType.DMA],
                   name="peer_sc_gather")
    return fn(u_tab, v_tab, idx)


def _unpack_halves(w):
    lo = lax.bitcast_convert_type(w << 16, F32)
    hi = lax.bitcast_convert_type(w & jnp.uint32(0xFFFF0000), F32)
    return lo, hi


def _peer_apply_body(x_ref, g_ref, ug_ref, vg_ref, o_ref):
    TT = x_ref.shape[0]
    gpad = jnp.concatenate([g_ref[...], jnp.zeros((PEER_SEL - TT, PEER_SEL), F32)], axis=0)
    gT = gpad.T
    for t in range(TT):
        rows = pl.ds(t * PEER_SEL, PEER_SEL)
        ulo, uhi = _unpack_halves(ug_ref[rows, :])
        xl = x_ref[pl.ds(t, 1), 0:PEER_HALF]
        xh = x_ref[pl.ds(t, 1), PEER_HALF:D_MODEL]
        dots = jnp.sum(ulo * xl + uhi * xh, axis=-1, keepdims=True)
        w = jax.nn.gelu(dots) * gT[:, t:t + 1]
        vlo, vhi = _unpack_halves(vg_ref[rows, :])
        o_ref[pl.ds(t, 1), 0:PEER_HALF] = jnp.sum(vlo * w, axis=0, keepdims=True)
        o_ref[pl.ds(t, 1), PEER_HALF:D_MODEL] = jnp.sum(vhi * w, axis=0, keepdims=True)


def peer_apply(x, gate, ug, vg):
    T = x.shape[0]
    TT = PEER_TT
    return pl.pallas_call(
        _peer_apply_body,
        grid=(T // TT,),
        in_specs=[pl.BlockSpec((TT, D_MODEL), lambda i: (i, 0)),
                  pl.BlockSpec((TT, PEER_SEL), lambda i: (i, 0)),
                  pl.BlockSpec((TT * PEER_SEL, PEER_HALF), lambda i: (i, 0)),
                  pl.BlockSpec((TT * PEER_SEL, PEER_HALF), lambda i: (i, 0))],
        out_specs=pl.BlockSpec((TT, D_MODEL), lambda i: (i, 0)),
        out_shape=jax.ShapeDtypeStruct((T, D_MODEL), F32),
        compiler_params=pltpu.CompilerParams(dimension_semantics=("parallel",),
                                             vmem_limit_bytes=40 * 1024 * 1024),
        name="peer_apply",
    )(x, gate, ug, vg)


def peer_route(xf, p):
    N = xf.shape[0]
    wq, keys = p['peer_wq'], p['peer_keys'].astype(F32)
    q = (xf @ wq).astype(F32).reshape(N, PEER_HEADS, 2, PEER_DQ // 2)
    s = jnp.einsum('thcq,hckq->thck', q, keys)
    sv, si = lax.top_k(s, PEER_TOPK)
    cand = (sv[:, :, 0, :, None] + sv[:, :, 1, None, :]).reshape(N, PEER_HEADS, PEER_TOPK * PEER_TOPK)
    cidx = (si[:, :, 0, :, None] * PEER_NKEYS + si[:, :, 1, None, :]).reshape(N, PEER_HEADS, PEER_TOPK * PEER_TOPK)
    top_s, top_j = lax.top_k(cand, PEER_TOPK)
    eidx = jnp.take_along_axis(cidx, top_j, axis=-1)
    gate = jax.nn.softmax(top_s, axis=-1)
    return eidx.reshape(N, PEER_SEL).astype(jnp.int32), gate.reshape(N, PEER_SEL)


def peer_ffn(h, p):
    Bn, L, D = h.shape
    N = Bn * L
    xf = h.reshape(N, D)
    eidx, gate = peer_route(xf, p)
    u_pk, v_pk = p['peer_u_pk'], p['peer_v_pk']
    nchunk = N // PEER_CHUNK

    def chunk(args):
        xb, eb, gb = args
        ug, vg = sc_gather_pair(u_pk, v_pk, eb.reshape(-1))
        return peer_apply(xb, gb, ug, vg)

    out = lax.map(chunk, (xf.reshape(nchunk, PEER_CHUNK, D), eidx.reshape(nchunk, PEER_CHUNK, PEER_SEL),
                          gate.reshape(nchunk, PEER_CHUNK, PEER_SEL)))
    return out.reshape(Bn, L, D)


def merge_groups(ys, g, dtype):
    outs, off = [], 0
    for y, w in zip(ys, GROUP_WIDTHS):
        outs.append(rmsnorm(y, g[off:off + w]).astype(dtype))
        off += w
    return jnp.concatenate(outs, axis=-1)


def token_mixers(h_c, h_l, p, need_ctx):
    zc = split_cols(h_c @ p['w_in'])
    zl = split_cols(h_l @ p['w_in'])
    a_c, a_l = mixer_rglru(zc[0:2], zl[0:2], p, need_ctx)
    b_c, b_l = mixer_rwkv(zc[2:8], zl[2:8], p, need_ctx)
    m_c, m_l = mixer_mlstm(zc[8:13], zl[8:13], p, need_ctx)
    d_l = mixer_hyena(zl[13:16], p)
    o_l = merge_groups([a_l, b_l, m_l, d_l], p['grp_g'], h_l.dtype) @ p['w_out']
    if not need_ctx:
        return None, o_l
    d_c = mixer_hyena(zc[13:16], p)
    o_c = merge_groups([a_c, b_c, m_c, d_c], p['grp_g'], h_c.dtype) @ p['w_out']
    return o_c, o_l


def trunk_layer(x_l, x_c, c, c_ctx, p, need_ctx):
    Bn = c.shape[0]
    mod_l = (jax.nn.silu(c) @ p['ada_w'] + p['ada_b']).reshape(Bn, 6, 1, D_MODEL)
    mod_c = (jax.nn.silu(c_ctx) @ p['ada_w'] + p['ada_b']).reshape(6, 1, 1, D_MODEL)
    h_l = modulate(rmsnorm(x_l, p['norm1_g']), mod_l[:, 0], mod_l[:, 1])
    h_c = modulate(rmsnorm(x_c, p['norm1_g']), mod_c[0], mod_c[1])
    o_c, o_l = token_mixers(h_c, h_l, p, need_ctx)
    x_l = x_l + mod_l[:, 2] * o_l
    x_l = x_l + mod_l[:, 5] * peer_ffn(modulate(rmsnorm(x_l, p['norm2_g']), mod_l[:, 3], mod_l[:, 4]), p)
    if need_ctx:
        x_c = x_c + mod_c[2] * o_c
        x_c = x_c + mod_c[5] * peer_ffn(modulate(rmsnorm(x_c, p['norm2_g']), mod_c[3], mod_c[4]), p)
    return x_l, x_c


def _final_norm_body(x_ref, g_ref, o_ref):
    xf = x_ref[...]
    y = xf * lax.rsqrt(jnp.mean(xf * xf, axis=-1, keepdims=True) + EPS)
    o_ref[...] = y * g_ref[...]


def final_rmsnorm(x, g):
    Bn, L, D = x.shape
    rows = Bn * L
    tile = 1024
    out = pl.pallas_call(
        _final_norm_body,
        grid=(rows // tile,),
        in_specs=[pl.BlockSpec((tile, D), lambda i: (i, 0)), pl.BlockSpec((1, D), lambda i: (0, 0))],
        out_specs=pl.BlockSpec((tile, D), lambda i: (i, 0)),
        out_shape=jax.ShapeDtypeStruct((rows, D), x.dtype),
        name="final_rmsnorm",
    )(x.reshape(rows, D), g.reshape(1, D))
    return out.reshape(Bn, L, D)


def kernel(x, c, ctx, c_ctx, ada_w, ada_b, norm1_g, norm2_g, w_in, w_out, grp_g,
           lru_conv_w, lru_conv_b, lru_wr, lru_br, lru_wi, lru_bi, lru_lam,
           rwkv_mu, rwkv_w0, rwkv_w2, rwkv_a0, rwkv_a2, rwkv_g2, rwkv_kk, rwkv_ka, rwkv_rk, rwkv_ln_g, rwkv_ln_b,
           mlstm_conv_w, mlstm_conv_b, mlstm_gate_b,
           hy_conv_w, hy_conv_b, hy_w1, hy_b1, hy_w2, hy_b2, hy_w3, hy_freq, hy_bias,
           peer_wq, peer_keys, peer_u, peer_v, final_g):
    x_l, x_c = x, ctx
    for i in range(DEPTH):
        p = dict(ada_w=ada_w[i], ada_b=ada_b[i], norm1_g=norm1_g[i], norm2_g=norm2_g[i], w_in=w_in[i],
                 w_out=w_out[i], grp_g=grp_g[i],
                 lru_conv_w=lru_conv_w[i], lru_conv_b=lru_conv_b[i], lru_wr=lru_wr[i], lru_br=lru_br[i],
                 lru_wi=lru_wi[i], lru_bi=lru_bi[i], lru_lam=lru_lam[i],
                 rwkv_mu=rwkv_mu[i], rwkv_w0=rwkv_w0[i], rwkv_w2=rwkv_w2[i], rwkv_a0=rwkv_a0[i],
                 rwkv_a2=rwkv_a2[i], rwkv_g2=rwkv_g2[i], rwkv_kk=rwkv_kk[i], rwkv_ka=rwkv_ka[i],
                 rwkv_rk=rwkv_rk[i], rwkv_ln_g=rwkv_ln_g[i], rwkv_ln_b=rwkv_ln_b[i],
                 mlstm_conv_w=mlstm_conv_w[i], mlstm_conv_b=mlstm_conv_b[i], mlstm_gate_b=mlstm_gate_b[i],
                 hy_conv_w=hy_conv_w[i], hy_conv_b=hy_conv_b[i], hy_w1=hy_w1[i], hy_b1=hy_b1[i],
                 hy_w2=hy_w2[i], hy_b2=hy_b2[i], hy_w3=hy_w3[i], hy_freq=hy_freq[i], hy_bias=hy_bias[i],
                 peer_wq=peer_wq[i], peer_keys=peer_keys[i],
                 peer_u_pk=pack_bf16_halves(peer_u[i]), peer_v_pk=pack_bf16_halves(peer_v[i]))
        x_l, x_c = trunk_layer(x_l, x_c, c, c_ctx, p, i < DEPTH - 1)
    return final_rmsnorm(x_l, final_g)
```

```python
import math
import jax, jax.numpy as jnp
from jax import lax
import numpy as np
from jax.experimental import pallas as pl
from jax.experimental.pallas import tpu as pltpu
from jax.experimental.pallas import tpu_sc as plsc

D_MODEL = 1024
BATCH = 16
SEQ = 2048
DEPTH = 2

GRID_W = 64
CTX_LEN = 256
F32 = jnp.float32
EPS = 1e-6
HEAD_DIM = 64
D_MIX = D_MODEL
W_LRU = D_MIX // 4
W_RWKV = D_MIX // 4
W_MLSTM = D_MIX // 4
W_HYENA = D_MIX - W_LRU - W_RWKV - W_MLSTM
GROUP_WIDTHS = (W_LRU, W_RWKV, W_MLSTM, W_HYENA)
H_LRU = W_LRU // HEAD_DIM
H_RWKV = W_RWKV // HEAD_DIM
H_MLSTM = W_MLSTM // HEAD_DIM
CONV_W = 4
LRU_C = 8.0
RWKV_LORA_W = 32
RWKV_LORA_A = 32
RWKV_LORA_G = 64
RWKV_GN_EPS = 64e-5
MLSTM_CHUNK = 64
HYENA_ORDER = 2
HYENA_SHORT = 3
HYENA_EMB = 33
HYENA_HID = 64
HYENA_TARGET = 1e-2
HYENA_FAST = 0.3
HYENA_SLOW = 1.5
PEER_HEADS = 8
PEER_NKEYS = 128
PEER_EXPERTS = PEER_NKEYS * PEER_NKEYS
PEER_TOPK = 16
PEER_DQ = 256
PEER_BLOCK = 128
IN_SPLITS = (W_LRU, W_LRU, W_RWKV, W_RWKV, W_RWKV, RWKV_LORA_W, RWKV_LORA_A, RWKV_LORA_G, W_MLSTM, W_MLSTM, W_MLSTM, W_MLSTM, 4 * H_MLSTM, W_HYENA, W_HYENA, W_HYENA)
D_IN = sum(IN_SPLITS)


def rmsnorm(x, g):
    xf = x.astype(F32)
    y = xf * lax.rsqrt(jnp.mean(xf * xf, axis=-1, keepdims=True) + EPS)
    return (y * g.astype(F32)).astype(x.dtype)


def modulate(h, shift, scale):
    return h * (1 + scale) + shift


def flip_if(a, d, axis=1):
    return jnp.flip(a, axis) if d == 1 else a


def split_cols(z):
    offs = np.cumsum(IN_SPLITS)[:-1].tolist()
    return jnp.split(z, offs, axis=-1)


def dwconv(x, w, b, pad_left):
    K = w.shape[0]
    y = lax.conv_general_dilated(x, w[:, None, :].astype(x.dtype), (1,), [(pad_left, K - 1 - pad_left)],
                                 dimension_numbers=('NWC', 'WIO', 'NWC'), feature_group_count=x.shape[-1])
    return y + b.astype(x.dtype)


def token_shift(x, mu):
    prev = jnp.pad(x, ((0, 0), (1, 0), (0, 0)))[:, :-1]
    nxt = jnp.pad(x, ((0, 0), (0, 1), (0, 0)))[:, 1:]
    return x + mu[0] * (prev - x) + mu[1] * (nxt - x)


def split_heads(t, h):
    Bn, L, W = t.shape
    return t.reshape(Bn, L, h, W // h)


def to_colmajor(a):
    Bn, L, C = a.shape
    rows = L // GRID_W
    return a.reshape(Bn, rows, GRID_W, C).transpose(0, 2, 1, 3).reshape(Bn, L, C)


def from_colmajor(a):
    Bn, L, C = a.shape
    rows = L // GRID_W
    return a.reshape(Bn, GRID_W, rows, C).transpose(0, 2, 1, 3).reshape(Bn, L, C)


def linear_recurrence(a, b, h0):
    def comb(l, r):
        return (l[0] * r[0], r[0] * l[1] + r[1])
    A, H = lax.associative_scan(comb, (a, b), axis=1)
    return H + A * h0[:, None, :]


def rglru_scan(xc, p, d, h0):
    Bn, L, W = xc.shape
    xh = xc.reshape(Bn, L, H_LRU, HEAD_DIM)
    r = jax.nn.sigmoid(jnp.einsum('blhi,hij->blhj', xh, p['lru_wr'][d]).reshape(Bn, L, W) + p['lru_br'][d])
    i = jax.nn.sigmoid(jnp.einsum('blhi,hij->blhj', xh, p['lru_wi'][d]).reshape(Bn, L, W) + p['lru_bi'][d])
    log_a = -LRU_C * r * jax.nn.softplus(-p['lru_lam'][d])
    b = jnp.sqrt(-jnp.expm1(2.0 * log_a)) * (i * xc)
    return linear_recurrence(jnp.exp(log_a), b, h0)


def mixer_rglru(f_c, f_l, p, need_ctx):
    (x_c, g_c), (x_l, g_l) = f_c, f_l
    xc_c = dwconv(x_c, p['lru_conv_w'], p['lru_conv_b'], CONV_W // 2).astype(F32)
    xc_l = dwconv(x_l, p['lru_conv_w'], p['lru_conv_b'], CONV_W // 2).astype(F32)
    h0 = jnp.zeros((xc_l.shape[0], W_LRU), F32)
    hs_c, hs_l = [], []
    for d in range(2):
        hc = rglru_scan(flip_if(xc_c, d), p, d, h0)
        hl = rglru_scan(flip_if(xc_l, d), p, d, hc[:, -1])
        hs_c.append(flip_if(hc, d))
        hs_l.append(flip_if(hl, d))
    y_l = jax.nn.gelu(g_l.astype(F32)) * (hs_l[0] + hs_l[1])
    y_c = jax.nn.gelu(g_c.astype(F32)) * (hs_c[0] + hs_c[1]) if need_ctx else None
    return y_c, y_l


def rwkv_shift(f, p):
    r, k, v, zw, za, zg = [t.astype(F32) for t in f]
    mu = p['rwkv_mu']
    return (token_shift(r, mu[0]), token_shift(k, mu[1]), token_shift(v, mu[2]), zw, za, zg)


def rwkv_dir_inputs(f, p, d):
    r, k, v, zw, za, _ = f
    w_log = -jax.nn.softplus(-(p['rwkv_w0'][d] + jnp.tanh(zw) @ p['rwkv_w2'][d])) - 0.5
    decay = jnp.exp(-jnp.exp(w_log))
    a = jax.nn.sigmoid(p['rwkv_a0'][d] + za @ p['rwkv_a2'][d])
    kk = split_heads(k * p['rwkv_kk'], H_RWKV)
    kk = kk / jnp.maximum(jnp.linalg.norm(kk, axis=-1, keepdims=True), 1e-12)
    kd = k * (1 + (a - 1) * p['rwkv_ka'])
    return [split_heads(r, H_RWKV), split_heads(decay, H_RWKV), split_heads(kd, H_RWKV),
            split_heads(v, H_RWKV), kk, kk * split_heads(a, H_RWKV)]


RWKV_TC = HEAD_DIM
RWKV_PAIRS = 32


def _split_bf16(x):
    hi_f = lax.bitcast_convert_type(lax.bitcast_convert_type(x, jnp.uint32) & jnp.uint32(0xFFFF0000), F32)
    return hi_f.astype(jnp.bfloat16), (x - hi_f).astype(jnp.bfloat16)


def _segsum_bcast(x, ones2):
    hi, lo = _split_bf16(x)
    return jnp.dot(jnp.concatenate([hi, lo], axis=-1), ones2, preferred_element_type=F32)


def _rwkv_body(kk_ref, w_ref, kka_ref, k_ref, v_ref, r_ref, y_ref, s_ref):
    NP, N = RWKV_PAIRS, HEAD_DIM

    @pl.when(pl.program_id(1) == 0)
    def _():
        s_ref[...] = jnp.zeros_like(s_ref)

    y_ref[...] = jnp.zeros_like(y_ref)
    lane = lax.broadcasted_iota(jnp.int32, (N, 2 * N), 1)
    row = lax.broadcasted_iota(jnp.int32, (N, 2 * N), 0)
    diag = (lane % N) == row
    seg = (lax.broadcasted_iota(jnp.int32, (4 * N, 2 * N), 0) % (2 * N)) // N == \
        lax.broadcasted_iota(jnp.int32, (4 * N, 2 * N), 1) // N
    ones2 = jnp.where(seg, 1.0, 0.0).astype(jnp.bfloat16)

    def step(t, carry):
        rowv = lambda ref: ref[:, pl.ds(t, 1), :]
        S = s_ref[...]
        P = S * rowv(kk_ref)
        Vd = jnp.where(diag, rowv(v_ref), 0.0)
        R = _segsum_bcast(jnp.concatenate([P, Vd], axis=0).reshape(2 * NP * N, 2 * N), ones2)
        R = R.reshape(2 * NP, N, 2 * N)
        S = S * rowv(w_ref) - R[:NP] * rowv(kka_ref) + R[NP:] * rowv(k_ref)
        s_ref[...] = S
        Y = _segsum_bcast((S * rowv(r_ref)).reshape(NP * N, 2 * N), ones2).reshape(NP, N, 2 * N)
        y_ref[:, 0] = jnp.where((lane % N) == t, Y, y_ref[:, 0])
        return carry

    lax.fori_loop(0, RWKV_TC, step, 0)


def rwkv_scan_pairs(kk, w, kka, k, v, r):
    NPT, L, _ = kk.shape
    NP, TC, N = RWKV_PAIRS, RWKV_TC, HEAD_DIM
    in_spec = pl.BlockSpec((NP, TC, 2 * N), lambda p, c: (p, c, 0))
    return pl.pallas_call(
        _rwkv_body,
        grid=(NPT // NP, L // TC),
        in_specs=[in_spec] * 6,
        out_specs=pl.BlockSpec((NP, 1, N, 2 * N), lambda p, c: (p, c, 0, 0)),
        out_shape=jax.ShapeDtypeStruct((NPT, L // TC, N, 2 * N), F32),
        scratch_shapes=[pltpu.VMEM((NP, N, 2 * N), F32)],
        compiler_params=pltpu.CompilerParams(dimension_semantics=("parallel", "arbitrary"),
                                             vmem_limit_bytes=40 * 1024 * 1024),
        name="rwkv7_scan",
    )(kk, w, kka, k, v, r)


def rwkv7_scan_bidir(ins_c, ins_l):
    Bn, Lc = ins_c[0][0].shape[:2]
    Ll = ins_l[0][0].shape[1]
    L = Lc + Ll

    def seq(j):
        xs = [jnp.concatenate([flip_if(ins_c[d][j], d), flip_if(ins_l[d][j], d)], axis=1) for d in range(2)]
        x = jnp.stack(xs).reshape(2, Bn, L, H_RWKV // 2, 2 * HEAD_DIM)
        return x.transpose(0, 1, 3, 2, 4).reshape(2 * Bn * (H_RWKV // 2), L, 2 * HEAD_DIM)

    r, w, k, v, kk, kka = [seq(j) for j in range(6)]
    yT = rwkv_scan_pairs(kk, w, kka, k, v, r)
    y = yT.reshape(2, Bn, H_RWKV // 2, L // HEAD_DIM, HEAD_DIM, 2, HEAD_DIM)
    y = y.transpose(0, 1, 3, 6, 2, 5, 4).reshape(2, Bn, L, H_RWKV, HEAD_DIM)
    return ([flip_if(y[d, :, :Lc], d) for d in range(2)], [flip_if(y[d, :, Lc:], d) for d in range(2)])


def rwkv_bonus(ins, p):
    r, _, kd, v = ins[:4]
    return jnp.sum(r * kd * p['rwkv_rk'], axis=-1, keepdims=True) * v


def rwkv_out(y, bonus, zg, p):
    Bn, L, H, N = y.shape
    mu = jnp.mean(y, axis=-1, keepdims=True)
    var = jnp.mean(jnp.square(y - mu), axis=-1, keepdims=True)
    yn = ((y - mu) * lax.rsqrt(var + RWKV_GN_EPS)).reshape(Bn, L, H * N) * p['rwkv_ln_g'] + p['rwkv_ln_b']
    g = jax.nn.sigmoid(zg) @ p['rwkv_g2']
    return (yn + bonus.reshape(Bn, L, H * N)) * g


def mixer_rwkv(f_c, f_l, p, need_ctx):
    fc, fl = rwkv_shift(f_c, p), rwkv_shift(f_l, p)
    ins_c = [rwkv_dir_inputs(fc, p, d) for d in range(2)]
    ins_l = [rwkv_dir_inputs(fl, p, d) for d in range(2)]
    o_c, o_l = rwkv7_scan_bidir(ins_c, ins_l)
    y_c, y_l = [], []
    for d in range(2):
        y_l.append((o_l[d], rwkv_bonus(ins_l[d], p)))
        if need_ctx:
            y_c.append((o_c[d], rwkv_bonus(ins_c[d], p)))
    out_l = rwkv_out(y_l[0][0] + y_l[1][0], y_l[0][1] + y_l[1][1], fl[5], p)
    out_c = rwkv_out(y_c[0][0] + y_c[1][0], y_c[0][1] + y_c[1][1], fc[5], p) if need_ctx else None
    return out_c, out_l


def mlstm_prep(f, p):
    q, k, v, o, gz = f
    Bn, L, _ = q.shape
    qk = jax.nn.silu(dwconv(jnp.concatenate([q, k], axis=-1), p['mlstm_conv_w'], p['mlstm_conv_b'], CONV_W // 2)).astype(F32)
    q, k = jnp.split(qk, 2, axis=-1)
    heads = lambda t: t.reshape(Bn, L, H_MLSTM, HEAD_DIM).transpose(0, 2, 1, 3)
    gates = (gz.astype(F32).reshape(Bn, L, 2, 2, H_MLSTM) + p['mlstm_gate_b']).transpose(2, 3, 0, 4, 1)
    return heads(q) * HEAD_DIM ** -0.5, heads(k), heads(v.astype(F32)), o.astype(F32), gates


def mlstm_chunkwise(q, k, v, ig, lf, state):
    Bn, H, L, N = q.shape
    T = MLSTM_CHUNK
    nc = L // T
    ch = lambda a: jnp.moveaxis(a.reshape((Bn, H, nc, T) + a.shape[3:]), 2, 0)
    mask = jnp.tril(jnp.ones((T, T), dtype=bool))

    def step(carry, inp):
        C, n, m = carry
        qc, kc, vc, ic, fc = inp
        b = jnp.cumsum(fc, axis=-1)
        logd = jnp.where(mask, b[..., :, None] - b[..., None, :] + ic[..., None, :], -jnp.inf)
        inter = b + m[..., None]
        mt = jnp.maximum(inter, jnp.max(logd, axis=-1))
        s = jnp.einsum('bhtn,bhsn->bhts', qc, kc) * jnp.exp(logd - mt[..., None])
        e_inter = jnp.exp(inter - mt)
        num = jnp.einsum('bhts,bhsn->bhtn', s, vc) + e_inter[..., None] * jnp.einsum('bhvk,bhtk->bhtv', C, qc)
        den = jnp.sum(s, axis=-1) + e_inter * jnp.einsum('bhk,bhtk->bht', n, qc)
        h = num / jnp.maximum(jnp.abs(den), jnp.exp(-mt))[..., None]
        bT = b[..., -1]
        logw = bT[..., None] - b + ic
        m_new = jnp.maximum(bT + m, jnp.max(logw, axis=-1))
        wgt = jnp.exp(logw - m_new[..., None])
        dec = jnp.exp(bT + m - m_new)
        C = dec[..., None, None] * C + jnp.einsum('bhs,bhsv,bhsk->bhvk', wgt, vc, kc)
        n = dec[..., None] * n + jnp.einsum('bhs,bhsk->bhk', wgt, kc)
        return (C, n, m_new), h

    state, hs = lax.scan(step, state, (ch(q), ch(k), ch(v), ch(ig), ch(lf)))
    return jnp.moveaxis(hs, 0, 2).reshape(Bn, H, L, N), state


def mixer_mlstm(f_c, f_l, p, need_ctx):
    f_l = [to_colmajor(t) for t in f_l]
    qc, kc, vc, oc, gc = mlstm_prep(f_c, p)
    ql, kl, vl, ol, gl = mlstm_prep(f_l, p)
    Bn = ql.shape[0]
    st0 = (jnp.zeros((Bn, H_MLSTM, HEAD_DIM, HEAD_DIM), F32), jnp.zeros((Bn, H_MLSTM, HEAD_DIM), F32),
           jnp.zeros((Bn, H_MLSTM), F32))
    h_c, h_l = [], []
    for d in range(2):
        hc, st = mlstm_chunkwise(flip_if(qc, d, 2), flip_if(kc, d, 2), flip_if(vc, d, 2), flip_if(gc[d, 0], d, 2),
                                 jax.nn.log_sigmoid(flip_if(gc[d, 1], d, 2)), st0)
        hl, _ = mlstm_chunkwise(flip_if(ql, d, 2), flip_if(kl, d, 2), flip_if(vl, d, 2), flip_if(gl[d, 0], d, 2),
                                jax.nn.log_sigmoid(flip_if(gl[d, 1], d, 2)), st)
        h_c.append(flip_if(hc, d, 2))
        h_l.append(flip_if(hl, d, 2))
    merge_heads = lambda h: h.transpose(0, 2, 1, 3).reshape(h.shape[0], h.shape[2], W_MLSTM)
    y_l = from_colmajor(jax.nn.sigmoid(ol) * merge_heads(h_l[0] + h_l[1]))
    y_c = jax.nn.sigmoid(oc) * merge_heads(h_c[0] + h_c[1]) if need_ctx else None
    return y_c, y_l


def hyena_spectrum(L, p):
    pos = jnp.arange(L, dtype=F32)
    t = pos / (L - 1)
    bands = (HYENA_EMB - 1) // 2
    freqs = jnp.linspace(1e-4, bands - 1, bands, dtype=F32)
    ang = (2 * math.pi / L) * pos[:, None] * freqs[None, :]
    z = jnp.concatenate([t[:, None], jnp.cos(ang), -jnp.sin(ang)], axis=-1)
    h = jnp.sin(p['hy_freq'][0] * (z @ p['hy_w1'] + p['hy_b1']))
    h = jnp.sin(p['hy_freq'][1] * (h @ p['hy_w2'] + p['hy_b2']))
    h = (h @ p['hy_w3']).astype(F32).reshape(L, HYENA_ORDER, 2, W_HYENA)
    deltas = jnp.abs(jnp.linspace(math.log(HYENA_TARGET) / HYENA_SLOW, math.log(HYENA_TARGET) / HYENA_FAST,
                                  W_HYENA, dtype=F32))
    h = h * jnp.exp(-t[:, None, None, None] * deltas)
    fwd, bwd = h[:, :, 0], h[:, :, 1]
    two = jnp.concatenate([fwd, jnp.zeros_like(fwd[:1]), jnp.flip(bwd[1:], axis=0)], axis=0)
    two = two / (jnp.sum(jnp.abs(two), axis=0, keepdims=True) + EPS)
    return jnp.fft.rfft(two, axis=0)


def long_conv(u, spec, bias):
    L = u.shape[1]
    y = jnp.fft.irfft(jnp.fft.rfft(u, n=2 * L, axis=1) * spec, n=2 * L, axis=1)[:, :L]
    return y + u * bias


def mixer_hyena(f, p):
    u = dwconv(jnp.concatenate(f, axis=-1), p['hy_conv_w'], p['hy_conv_b'], HYENA_SHORT // 2).astype(F32)
    v, x1, x2 = jnp.split(u, 3, axis=-1)
    spec = hyena_spectrum(u.shape[1], p)
    z = x1 * long_conv(v, spec[:, 0], p['hy_bias'][0])
    return x2 * long_conv(z, spec[:, 1], p['hy_bias'][1])


PEER_SEL = PEER_HEADS * PEER_TOPK
PEER_HALF = D_MODEL // 2
PEER_CHUNK = 4096
PEER_TT = 16
SC_GATHER_ROWS = 64


def pack_bf16_halves(tab):
    bits = lax.bitcast_convert_type(tab.astype(jnp.bfloat16), jnp.uint16).astype(jnp.uint32)
    half = tab.shape[1] // 2
    return bits[:, :half] | (bits[:, half:] << 16)


def sc_gather_pair(u_tab, v_tab, idx):
    n = idx.shape[0]
    W = u_tab.shape[1]
    info = plsc.get_sparse_core_info()
    nc, ns = info.num_cores, info.num_subcores
    per_w = n // (nc * ns)
    G = SC_GATHER_ROWS
    assert per_w * nc * ns == n and per_w % G == 0
    nsteps = per_w // G
    mesh = plsc.VectorSubcoreMesh(core_axis_name="c", subcore_axis_name="s")
    out = jax.ShapeDtypeStruct((n, W), u_tab.dtype)

    def body(u_hbm, v_hbm, idx_hbm, uo_hbm, vo_hbm, idx_v, urows, vrows, usem, vsem):
        wid = lax.axis_index("s") * nc + lax.axis_index("c")
        base = wid * per_w
        pltpu.sync_copy(idx_hbm.at[pl.ds(base, per_w)], idx_v)

        @pl.loop(0, nsteps)
        def _(i):
            off = pl.multiple_of(i * G, G)
            ids = idx_v.at[pl.ds(off, G)]
            cu = pltpu.async_copy(u_hbm.at[ids], urows, usem)
            cv = pltpu.async_copy(v_hbm.at[ids], vrows, vsem)
            cu.wait()
            pltpu.sync_copy(urows, uo_hbm.at[pl.ds(base + off, G)])
            cv.wait()
            pltpu.sync_copy(vrows, vo_hbm.at[pl.ds(base + off, G)])

    fn = pl.kernel(body, out_type=(out, out), mesh=mesh,
                   scratch_types=[pltpu.VMEM((per_w,), jnp.int32),
                                  pltpu.VMEM((G, W), u_tab.dtype), pltpu.VMEM((G, W), u_tab.dtype),
                                  pltpu.SemaphoreType.DMA, pltpu.SemaphoreType.DMA],
                   name="peer_sc_gather")
    return fn(u_tab, v_tab, idx)


def _unpack_halves(w):
    lo = lax.bitcast_convert_type(w << 16, F32)
    hi = lax.bitcast_convert_type(w & jnp.uint32(0xFFFF0000), F32)
    return lo, hi


def _peer_apply_body(x_ref, g_ref, ug_ref, vg_ref, o_ref):
    TT = x_ref.shape[0]
    gpad = jnp.concatenate([g_ref[...], jnp.zeros((PEER_SEL - TT, PEER_SEL), F32)], axis=0)
    gT = gpad.T
    for t in range(TT):
        rows = pl.ds(t * PEER_SEL, PEER_SEL)
        ulo, uhi = _unpack_halves(ug_ref[rows, :])
        xl = x_ref[pl.ds(t, 1), 0:PEER_HALF]
        xh = x_ref[pl.ds(t, 1), PEER_HALF:D_MODEL]
        dots = jnp.sum(ulo * xl + uhi * xh, axis=-1, keepdims=True)
        w = jax.nn.gelu(dots) * gT[:, t:t + 1]
        vlo, vhi = _unpack_halves(vg_ref[rows, :])
        o_ref[pl.ds(t, 1), 0:PEER_HALF] = jnp.sum(vlo * w, axis=0, keepdims=True)
        o_ref[pl.ds(t, 1), PEER_HALF:D_MODEL] = jnp.sum(vhi * w, axis=0, keepdims=True)


def peer_apply(x, gate, ug, vg):
    T = x.shape[0]
    TT = PEER_TT
    return pl.pallas_call(
        _peer_apply_body,
        grid=(T // TT,),
        in_specs=[pl.BlockSpec((TT, D_MODEL), lambda i: (i, 0)),
                  pl.BlockSpec((TT, PEER_SEL), lambda i: (i, 0)),
                  pl.BlockSpec((TT * PEER_SEL, PEER_HALF), lambda i: (i, 0)),
                  pl.BlockSpec((TT * PEER_SEL, PEER_HALF), lambda i: (i, 0))],
        out_specs=pl.BlockSpec((TT, D_MODEL), lambda i: (i, 0)),
        out_shape=jax.ShapeDtypeStruct((T, D_MODEL), F32),
        compiler_params=pltpu.CompilerParams(dimension_semantics=("parallel",),
                                             vmem_limit_bytes=40 * 1024 * 1024),
        name="peer_apply",
    )(x, gate, ug, vg)


def peer_route(xf, p):
    N = xf.shape[0]
    wq, keys = p['peer_wq'], p['peer_keys'].astype(F32)
    q = (xf @ wq).astype(F32).reshape(N, PEER_HEADS, 2, PEER_DQ // 2)
    s = jnp.einsum('thcq,hckq->thck', q, keys)
    sv, si = lax.top_k(s, PEER_TOPK)
    cand = (sv[:, :, 0, :, None] + sv[:, :, 1, None, :]).reshape(N, PEER_HEADS, PEER_TOPK * PEER_TOPK)
    cidx = (si[:, :, 0, :, None] * PEER_NKEYS + si[:, :, 1, None, :]).reshape(N, PEER_HEADS, PEER_TOPK * PEER_TOPK)
    top_s, top_j = lax.top_k(cand, PEER_TOPK)
    eidx = jnp.take_along_axis(cidx, top_j, axis=-1)
    gate = jax.nn.softmax(top_s, axis=-1)
    return eidx.reshape(N, PEER_SEL).astype(jnp.int32), gate.reshape(N, PEER_SEL)


def peer_ffn(h, p):
    Bn, L, D = h.shape
    N = Bn * L
    xf = h.reshape(N, D)
    eidx, gate = peer_route(xf, p)
    u_pk, v_pk = p['peer_u_pk'], p['peer_v_pk']
    outs = []
    for c0 in range(0, N, PEER_CHUNK):
        rows = slice(c0, c0 + PEER_CHUNK)
        ug, vg = sc_gather_pair(u_pk, v_pk, eidx[rows].reshape(-1))
        outs.append(peer_apply(xf[rows], gate[rows], ug, vg))
    return jnp.concatenate(outs, axis=0).reshape(Bn, L, D)


def merge_groups(ys, g, dtype):
    outs, off = [], 0
    for y, w in zip(ys, GROUP_WIDTHS):
        outs.append(rmsnorm(y, g[off:off + w]).astype(dtype))
        off += w
    return jnp.concatenate(outs, axis=-1)


def token_mixers(h_c, h_l, p, need_ctx):
    zc = split_cols(h_c @ p['w_in'])
    zl = split_cols(h_l @ p['w_in'])
    a_c, a_l = mixer_rglru(zc[0:2], zl[0:2], p, need_ctx)
    b_c, b_l = mixer_rwkv(zc[2:8], zl[2:8], p, need_ctx)
    m_c, m_l = mixer_mlstm(zc[8:13], zl[8:13], p, need_ctx)
    d_l = mixer_hyena(zl[13:16], p)
    o_l = merge_groups([a_l, b_l, m_l, d_l], p['grp_g'], h_l.dtype) @ p['w_out']
    if not need_ctx:
        return None, o_l
    d_c = mixer_hyena(zc[13:16], p)
    o_c = merge_groups([a_c, b_c, m_c, d_c], p['grp_g'], h_c.dtype) @ p['w_out']
    return o_c, o_l


def trunk_layer(x_l, x_c, c, c_ctx, p, need_ctx):
    Bn = c.shape[0]
    mod_l = (jax.nn.silu(c) @ p['ada_w'] + p['ada_b']).reshape(Bn, 6, 1, D_MODEL)
    mod_c = (jax.nn.silu(c_ctx) @ p['ada_w'] + p['ada_b']).reshape(6, 1, 1, D_MODEL)
    h_l = modulate(rmsnorm(x_l, p['norm1_g']), mod_l[:, 0], mod_l[:, 1])
    h_c = modulate(rmsnorm(x_c, p['norm1_g']), mod_c[0], mod_c[1])
    o_c, o_l = token_mixers(h_c, h_l, p, need_ctx)
    x_l = x_l + mod_l[:, 2] * o_l
    x_l = x_l + mod_l[:, 5] * peer_ffn(modulate(rmsnorm(x_l, p['norm2_g']), mod_l[:, 3], mod_l[:, 4]), p)
    if need_ctx:
        x_c = x_c + mod_c[2] * o_c
        x_c = x_c + mod_c[5] * peer_ffn(modulate(rmsnorm(x_c, p['norm2_g']), mod_c[3], mod_c[4]), p)
    return x_l, x_c


def _final_norm_body(x_ref, g_ref, o_ref):
    xf = x_ref[...]
    y = xf * lax.rsqrt(jnp.mean(xf * xf, axis=-1, keepdims=True) + EPS)
    o_ref[...] = y * g_ref[...]


def final_rmsnorm(x, g):
    Bn, L, D = x.shape
    rows = Bn * L
    tile = 1024
    out = pl.pallas_call(
        _final_norm_body,
        grid=(rows // tile,),
        in_specs=[pl.BlockSpec((tile, D), lambda i: (i, 0)), pl.BlockSpec((1, D), lambda i: (0, 0))],
        out_specs=pl.BlockSpec((tile, D), lambda i: (i, 0)),
        out_shape=jax.ShapeDtypeStruct((rows, D), x.dtype),
        name="final_rmsnorm",
    )(x.reshape(rows, D), g.reshape(1, D))
    return out.reshape(Bn, L, D)


def kernel(x, c, ctx, c_ctx, ada_w, ada_b, norm1_g, norm2_g, w_in, w_out, grp_g,
           lru_conv_w, lru_conv_b, lru_wr, lru_br, lru_wi, lru_bi, lru_lam,
           rwkv_mu, rwkv_w0, rwkv_w2, rwkv_a0, rwkv_a2, rwkv_g2, rwkv_kk, rwkv_ka, rwkv_rk, rwkv_ln_g, rwkv_ln_b,
           mlstm_conv_w, mlstm_conv_b, mlstm_gate_b,
           hy_conv_w, hy_conv_b, hy_w1, hy_b1, hy_w2, hy_b2, hy_w3, hy_freq, hy_bias,
           peer_wq, peer_keys, peer_u, peer_v, final_g):
    x_l, x_c = x, ctx
    for i in range(DEPTH):
        p = dict(ada_w=ada_w[i], ada_b=ada_b[i], norm1_g=norm1_g[i], norm2_g=norm2_g[i], w_in=w_in[i],
                 w_out=w_out[i], grp_g=grp_g[i],
                 lru_conv_w=lru_conv_w[i], lru_conv_b=lru_conv_b[i], lru_wr=lru_wr[i], lru_br=lru_br[i],
                 lru_wi=lru_wi[i], lru_bi=lru_bi[i], lru_lam=lru_lam[i],
                 rwkv_mu=rwkv_mu[i], rwkv_w0=rwkv_w0[i], rwkv_w2=rwkv_w2[i], rwkv_a0=rwkv_a0[i],
                 rwkv_a2=rwkv_a2[i], rwkv_g2=rwkv_g2[i], rwkv_kk=rwkv_kk[i], rwkv_ka=rwkv_ka[i],
                 rwkv_rk=rwkv_rk[i], rwkv_ln_g=rwkv_ln_g[i], rwkv_ln_b=rwkv_ln_b[i],
                 mlstm_conv_w=mlstm_conv_w[i], mlstm_conv_b=mlstm_conv_b[i], mlstm_gate_b=mlstm_gate_b[i],
                 hy_conv_w=hy_conv_w[i], hy_conv_b=hy_conv_b[i], hy_w1=hy_w1[i], hy_b1=hy_b1[i],
                 hy_w2=hy_w2[i], hy_b2=hy_b2[i], hy_w3=hy_w3[i], hy_freq=hy_freq[i], hy_bias=hy_bias[i],
                 peer_wq=peer_wq[i], peer_keys=peer_keys[i],
                 peer_u_pk=pack_bf16_halves(peer_u[i]), peer_v_pk=pack_bf16_halves(peer_v[i]))
        x_l, x_c = trunk_layer(x_l, x_c, c, c_ctx, p, i < DEPTH - 1)
    return final_rmsnorm(x_l, final_g)
```

```python
import math
import jax, jax.numpy as jnp
from jax import lax
import numpy as np
from jax.experimental import pallas as pl
from jax.experimental.pallas import tpu as pltpu
from jax.experimental.pallas import tpu_sc as plsc

D_MODEL = 1024
BATCH = 16
SEQ = 2048
DEPTH = 2

GRID_W = 64
CTX_LEN = 256
F32 = jnp.float32
EPS = 1e-6
HEAD_DIM = 64
D_MIX = D_MODEL
W_LRU = D_MIX // 4
W_RWKV = D_MIX // 4
W_MLSTM = D_MIX // 4
W_HYENA = D_MIX - W_LRU - W_RWKV - W_MLSTM
GROUP_WIDTHS = (W_LRU, W_RWKV, W_MLSTM, W_HYENA)
H_LRU = W_LRU // HEAD_DIM
H_RWKV = W_RWKV // HEAD_DIM
H_MLSTM = W_MLSTM // HEAD_DIM
CONV_W = 4
LRU_C = 8.0
RWKV_LORA_W = 32
RWKV_LORA_A = 32
RWKV_LORA_G = 64
RWKV_GN_EPS = 64e-5
MLSTM_CHUNK = 64
HYENA_ORDER = 2
HYENA_SHORT = 3
HYENA_EMB = 33
HYENA_HID = 64
HYENA_TARGET = 1e-2
HYENA_FAST = 0.3
HYENA_SLOW = 1.5
PEER_HEADS = 8
PEER_NKEYS = 128
PEER_EXPERTS = PEER_NKEYS * PEER_NKEYS
PEER_TOPK = 16
PEER_DQ = 256
PEER_BLOCK = 128
IN_SPLITS = (W_LRU, W_LRU, W_RWKV, W_RWKV, W_RWKV, RWKV_LORA_W, RWKV_LORA_A, RWKV_LORA_G, W_MLSTM, W_MLSTM, W_MLSTM, W_MLSTM, 4 * H_MLSTM, W_HYENA, W_HYENA, W_HYENA)
D_IN = sum(IN_SPLITS)


def rmsnorm(x, g):
    xf = x.astype(F32)
    y = xf * lax.rsqrt(jnp.mean(xf * xf, axis=-1, keepdims=True) + EPS)
    return (y * g.astype(F32)).astype(x.dtype)


def modulate(h, shift, scale):
    return h * (1 + scale) + shift


def flip_if(a, d, axis=1):
    return jnp.flip(a, axis) if d == 1 else a


def split_cols(z):
    offs = np.cumsum(IN_SPLITS)[:-1].tolist()
    return jnp.split(z, offs, axis=-1)


def dwconv(x, w, b, pad_left):
    K = w.shape[0]
    y = lax.conv_general_dilated(x, w[:, None, :].astype(x.dtype), (1,), [(pad_left, K - 1 - pad_left)],
                                 dimension_numbers=('NWC', 'WIO', 'NWC'), feature_group_count=x.shape[-1])
    return y + b.astype(x.dtype)


def token_shift(x, mu):
    prev = jnp.pad(x, ((0, 0), (1, 0), (0, 0)))[:, :-1]
    nxt = jnp.pad(x, ((0, 0), (0, 1), (0, 0)))[:, 1:]
    return x + mu[0] * (prev - x) + mu[1] * (nxt - x)


def split_heads(t, h):
    Bn, L, W = t.shape
    return t.reshape(Bn, L, h, W // h)


def to_colmajor(a):
    Bn, L, C = a.shape
    rows = L // GRID_W
    return a.reshape(Bn, rows, GRID_W, C).transpose(0, 2, 1, 3).reshape(Bn, L, C)


def from_colmajor(a):
    Bn, L, C = a.shape
    rows = L // GRID_W
    return a.reshape(Bn, GRID_W, rows, C).transpose(0, 2, 1, 3).reshape(Bn, L, C)


def linear_recurrence(a, b, h0):
    def comb(l, r):
        return (l[0] * r[0], r[0] * l[1] + r[1])
    A, H = lax.associative_scan(comb, (a, b), axis=1)
    return H + A * h0[:, None, :]


def rglru_scan(xc, p, d, h0):
    Bn, L, W = xc.shape
    xh = xc.reshape(Bn, L, H_LRU, HEAD_DIM)
    r = jax.nn.sigmoid(jnp.einsum('blhi,hij->blhj', xh, p['lru_wr'][d]).reshape(Bn, L, W) + p['lru_br'][d])
    i = jax.nn.sigmoid(jnp.einsum('blhi,hij->blhj', xh, p['lru_wi'][d]).reshape(Bn, L, W) + p['lru_bi'][d])
    log_a = -LRU_C * r * jax.nn.softplus(-p['lru_lam'][d])
    b = jnp.sqrt(-jnp.expm1(2.0 * log_a)) * (i * xc)
    return linear_recurrence(jnp.exp(log_a), b, h0)


def mixer_rglru(f_c, f_l, p, need_ctx):
    (x_c, g_c), (x_l, g_l) = f_c, f_l
    xc_c = dwconv(x_c, p['lru_conv_w'], p['lru_conv_b'], CONV_W // 2).astype(F32)
    xc_l = dwconv(x_l, p['lru_conv_w'], p['lru_conv_b'], CONV_W // 2).astype(F32)
    h0 = jnp.zeros((xc_l.shape[0], W_LRU), F32)
    hs_c, hs_l = [], []
    for d in range(2):
        hc = rglru_scan(flip_if(xc_c, d), p, d, h0)
        hl = rglru_scan(flip_if(xc_l, d), p, d, hc[:, -1])
        hs_c.append(flip_if(hc, d))
        hs_l.append(flip_if(hl, d))
    y_l = jax.nn.gelu(g_l.astype(F32)) * (hs_l[0] + hs_l[1])
    y_c = jax.nn.gelu(g_c.astype(F32)) * (hs_c[0] + hs_c[1]) if need_ctx else None
    return y_c, y_l


def rwkv_shift(f, p):
    r, k, v, zw, za, zg = [t.astype(F32) for t in f]
    mu = p['rwkv_mu']
    return (token_shift(r, mu[0]), token_shift(k, mu[1]), token_shift(v, mu[2]), zw, za, zg)


def rwkv_dir_inputs(f, p, d):
    r, k, v, zw, za, _ = f
    w_log = -jax.nn.softplus(-(p['rwkv_w0'][d] + jnp.tanh(zw) @ p['rwkv_w2'][d])) - 0.5
    decay = jnp.exp(-jnp.exp(w_log))
    a = jax.nn.sigmoid(p['rwkv_a0'][d] + za @ p['rwkv_a2'][d])
    kk = split_heads(k * p['rwkv_kk'], H_RWKV)
    kk = kk / jnp.maximum(jnp.linalg.norm(kk, axis=-1, keepdims=True), 1e-12)
    kd = k * (1 + (a - 1) * p['rwkv_ka'])
    return [split_heads(r, H_RWKV), split_heads(decay, H_RWKV), split_heads(kd, H_RWKV),
            split_heads(v, H_RWKV), kk, kk * split_heads(a, H_RWKV)]


RWKV_TC = HEAD_DIM
RWKV_PAIRS = 32


def _split_bf16(x):
    hi_f = lax.bitcast_convert_type(lax.bitcast_convert_type(x, jnp.uint32) & jnp.uint32(0xFFFF0000), F32)
    return hi_f.astype(jnp.bfloat16), (x - hi_f).astype(jnp.bfloat16)


def _segsum_bcast(x, ones2):
    hi, lo = _split_bf16(x)
    return jnp.dot(jnp.concatenate([hi, lo], axis=-1), ones2, preferred_element_type=F32)


def _rwkv_body(kk_ref, w_ref, kka_ref, k_ref, v_ref, r_ref, y_ref, s_ref):
    NP, N = RWKV_PAIRS, HEAD_DIM

    @pl.when(pl.program_id(1) == 0)
    def _():
        s_ref[...] = jnp.zeros_like(s_ref)

    y_ref[...] = jnp.zeros_like(y_ref)
    lane = lax.broadcasted_iota(jnp.int32, (N, 2 * N), 1)
    row = lax.broadcasted_iota(jnp.int32, (N, 2 * N), 0)
    diag = (lane % N) == row
    seg = (lax.broadcasted_iota(jnp.int32, (4 * N, 2 * N), 0) % (2 * N)) // N == \
        lax.broadcasted_iota(jnp.int32, (4 * N, 2 * N), 1) // N
    ones2 = jnp.where(seg, 1.0, 0.0).astype(jnp.bfloat16)

    def step(t, carry):
        rowv = lambda ref: ref[:, pl.ds(t, 1), :]
        S = s_ref[...]
        P = S * rowv(kk_ref)
        Vd = jnp.where(diag, rowv(v_ref), 0.0)
        R = _segsum_bcast(jnp.concatenate([P, Vd], axis=0).reshape(2 * NP * N, 2 * N), ones2)
        R = R.reshape(2 * NP, N, 2 * N)
        S = S * rowv(w_ref) - R[:NP] * rowv(kka_ref) + R[NP:] * rowv(k_ref)
        s_ref[...] = S
        Y = _segsum_bcast((S * rowv(r_ref)).reshape(NP * N, 2 * N), ones2).reshape(NP, N, 2 * N)
        y_ref[:, 0] = jnp.where((lane % N) == t, Y, y_ref[:, 0])
        return carry

    lax.fori_loop(0, RWKV_TC, step, 0)


def rwkv_scan_pairs(kk, w, kka, k, v, r):
    NPT, L, _ = kk.shape
    NP, TC, N = RWKV_PAIRS, RWKV_TC, HEAD_DIM
    in_spec = pl.BlockSpec((NP, TC, 2 * N), lambda p, c: (p, c, 0))
    return pl.pallas_call(
        _rwkv_body,
        grid=(NPT // NP, L // TC),
        in_specs=[in_spec] * 6,
        out_specs=pl.BlockSpec((NP, 1, N, 2 * N), lambda p, c: (p, c, 0, 0)),
        out_shape=jax.ShapeDtypeStruct((NPT, L // TC, N, 2 * N), F32),
        scratch_shapes=[pltpu.VMEM((NP, N, 2 * N), F32)],
        compiler_params=pltpu.CompilerParams(dimension_semantics=("parallel", "arbitrary"),
                                             vmem_limit_bytes=40 * 1024 * 1024),
        name="rwkv7_scan",
    )(kk, w, kka, k, v, r)


def rwkv7_scan_bidir(ins_c, ins_l):
    Bn, Lc = ins_c[0][0].shape[:2]
    Ll = ins_l[0][0].shape[1]
    L = Lc + Ll

    def seq(j):
        xs = [jnp.concatenate([flip_if(ins_c[d][j], d), flip_if(ins_l[d][j], d)], axis=1) for d in range(2)]
        x = jnp.stack(xs).reshape(2, Bn, L, H_RWKV // 2, 2 * HEAD_DIM)
        return x.transpose(0, 1, 3, 2, 4).reshape(2 * Bn * (H_RWKV // 2), L, 2 * HEAD_DIM)

    r, w, k, v, kk, kka = [seq(j) for j in range(6)]
    yT = rwkv_scan_pairs(kk, w, kka, k, v, r)
    y = yT.reshape(2, Bn, H_RWKV // 2, L // HEAD_DIM, HEAD_DIM, 2, HEAD_DIM)
    y = y.transpose(0, 1, 3, 6, 2, 5, 4).reshape(2, Bn, L, H_RWKV, HEAD_DIM)
    return ([flip_if(y[d, :, :Lc], d) for d in range(2)], [flip_if(y[d, :, Lc:], d) for d in range(2)])


def rwkv_bonus(ins, p):
    r, _, kd, v = ins[:4]
    return jnp.sum(r * kd * p['rwkv_rk'], axis=-1, keepdims=True) * v


def rwkv_out(y, bonus, zg, p):
    Bn, L, H, N = y.shape
    mu = jnp.mean(y, axis=-1, keepdims=True)
    var = jnp.mean(jnp.square(y - mu), axis=-1, keepdims=True)
    yn = ((y - mu) * lax.rsqrt(var + RWKV_GN_EPS)).reshape(Bn, L, H * N) * p['rwkv_ln_g'] + p['rwkv_ln_b']
    g = jax.nn.sigmoid(zg) @ p['rwkv_g2']
    return (yn + bonus.reshape(Bn, L, H * N)) * g


def mixer_rwkv(f_c, f_l, p, need_ctx):
    fc, fl = rwkv_shift(f_c, p), rwkv_shift(f_l, p)
    ins_c = [rwkv_dir_inputs(fc, p, d) for d in range(2)]
    ins_l = [rwkv_dir_inputs(fl, p, d) for d in range(2)]
    o_c, o_l = rwkv7_scan_bidir(ins_c, ins_l)
    y_c, y_l = [], []
    for d in range(2):
        y_l.append((o_l[d], rwkv_bonus(ins_l[d], p)))
        if need_ctx:
            y_c.append((o_c[d], rwkv_bonus(ins_c[d], p)))
    out_l = rwkv_out(y_l[0][0] + y_l[1][0], y_l[0][1] + y_l[1][1], fl[5], p)
    out_c = rwkv_out(y_c[0][0] + y_c[1][0], y_c[0][1] + y_c[1][1], fc[5], p) if need_ctx else None
    return out_c, out_l


def mlstm_prep(f, p):
    q, k, v, o, gz = f
    Bn, L, _ = q.shape
    qk = jax.nn.silu(dwconv(jnp.concatenate([q, k], axis=-1), p['mlstm_conv_w'], p['mlstm_conv_b'], CONV_W // 2)).astype(F32)
    q, k = jnp.split(qk, 2, axis=-1)
    heads = lambda t: t.reshape(Bn, L, H_MLSTM, HEAD_DIM).transpose(0, 2, 1, 3)
    gates = (gz.astype(F32).reshape(Bn, L, 2, 2, H_MLSTM) + p['mlstm_gate_b']).transpose(2, 3, 0, 4, 1)
    return heads(q) * HEAD_DIM ** -0.5, heads(k), heads(v.astype(F32)), o.astype(F32), gates


def mlstm_chunkwise(q, k, v, ig, lf, state):
    Bn, H, L, N = q.shape
    T = MLSTM_CHUNK
    nc = L // T
    ch = lambda a: jnp.moveaxis(a.reshape((Bn, H, nc, T) + a.shape[3:]), 2, 0)
    mask = jnp.tril(jnp.ones((T, T), dtype=bool))

    def step(carry, inp):
        C, n, m = carry
        qc, kc, vc, ic, fc = inp
        b = jnp.cumsum(fc, axis=-1)
        logd = jnp.where(mask, b[..., :, None] - b[..., None, :] + ic[..., None, :], -jnp.inf)
        inter = b + m[..., None]
        mt = jnp.maximum(inter, jnp.max(logd, axis=-1))
        s = jnp.einsum('bhtn,bhsn->bhts', qc, kc) * jnp.exp(logd - mt[..., None])
        e_inter = jnp.exp(inter - mt)
        num = jnp.einsum('bhts,bhsn->bhtn', s, vc) + e_inter[..., None] * jnp.einsum('bhvk,bhtk->bhtv', C, qc)
        den = jnp.sum(s, axis=-1) + e_inter * jnp.einsum('bhk,bhtk->bht', n, qc)
        h = num / jnp.maximum(jnp.abs(den), jnp.exp(-mt))[..., None]
        bT = b[..., -1]
        logw = bT[..., None] - b + ic
        m_new = jnp.maximum(bT + m, jnp.max(logw, axis=-1))
        wgt = jnp.exp(logw - m_new[..., None])
        dec = jnp.exp(bT + m - m_new)
        C = dec[..., None, None] * C + jnp.einsum('bhs,bhsv,bhsk->bhvk', wgt, vc, kc)
        n = dec[..., None] * n + jnp.einsum('bhs,bhsk->bhk', wgt, kc)
        return (C, n, m_new), h

    state, hs = lax.scan(step, state, (ch(q), ch(k), ch(v), ch(ig), ch(lf)))
    return jnp.moveaxis(hs, 0, 2).reshape(Bn, H, L, N), state


def mixer_mlstm(f_c, f_l, p, need_ctx):
    f_l = [to_colmajor(t) for t in f_l]
    qc, kc, vc, oc, gc = mlstm_prep(f_c, p)
    ql, kl, vl, ol, gl = mlstm_prep(f_l, p)
    Bn = ql.shape[0]
    st0 = (jnp.zeros((Bn, H_MLSTM, HEAD_DIM, HEAD_DIM), F32), jnp.zeros((Bn, H_MLSTM, HEAD_DIM), F32),
           jnp.zeros((Bn, H_MLSTM), F32))
    h_c, h_l = [], []
    for d in range(2):
        hc, st = mlstm_chunkwise(flip_if(qc, d, 2), flip_if(kc, d, 2), flip_if(vc, d, 2), flip_if(gc[d, 0], d, 2),
                                 jax.nn.log_sigmoid(flip_if(gc[d, 1], d, 2)), st0)
        hl, _ = mlstm_chunkwise(flip_if(ql, d, 2), flip_if(kl, d, 2), flip_if(vl, d, 2), flip_if(gl[d, 0], d, 2),
                                jax.nn.log_sigmoid(flip_if(gl[d, 1], d, 2)), st)
        h_c.append(flip_if(hc, d, 2))
        h_l.append(flip_if(hl, d, 2))
    merge_heads = lambda h: h.transpose(0, 2, 1, 3).reshape(h.shape[0], h.shape[2], W_MLSTM)
    y_l = from_colmajor(jax.nn.sigmoid(ol) * merge_heads(h_l[0] + h_l[1]))
    y_c = jax.nn.sigmoid(oc) * merge_heads(h_c[0] + h_c[1]) if need_ctx else None
    return y_c, y_l


def hyena_spectrum(L, p):
    pos = jnp.arange(L, dtype=F32)
    t = pos / (L - 1)
    bands = (HYENA_EMB - 1) // 2
    freqs = jnp.linspace(1e-4, bands - 1, bands, dtype=F32)
    ang = (2 * math.pi / L) * pos[:, None] * freqs[None, :]
    z = jnp.concatenate([t[:, None], jnp.cos(ang), -jnp.sin(ang)], axis=-1)
    h = jnp.sin(p['hy_freq'][0] * (z @ p['hy_w1'] + p['hy_b1']))
    h = jnp.sin(p['hy_freq'][1] * (h @ p['hy_w2'] + p['hy_b2']))
    h = (h @ p['hy_w3']).astype(F32).reshape(L, HYENA_ORDER, 2, W_HYENA)
    deltas = jnp.abs(jnp.linspace(math.log(HYENA_TARGET) / HYENA_SLOW, math.log(HYENA_TARGET) / HYENA_FAST,
                                  W_HYENA, dtype=F32))
    h = h * jnp.exp(-t[:, None, None, None] * deltas)
    fwd, bwd = h[:, :, 0], h[:, :, 1]
    two = jnp.concatenate([fwd, jnp.zeros_like(fwd[:1]), jnp.flip(bwd[1:], axis=0)], axis=0)
    two = two / (jnp.sum(jnp.abs(two), axis=0, keepdims=True) + EPS)
    return jnp.fft.rfft(two, axis=0)


def long_conv(u, spec, bias):
    L = u.shape[1]
    y = jnp.fft.irfft(jnp.fft.rfft(u, n=2 * L, axis=1) * spec, n=2 * L, axis=1)[:, :L]
    return y + u * bias


def mixer_hyena(f, p):
    u = dwconv(jnp.concatenate(f, axis=-1), p['hy_conv_w'], p['hy_conv_b'], HYENA_SHORT // 2).astype(F32)
    v, x1, x2 = jnp.split(u, 3, axis=-1)
    spec = hyena_spectrum(u.shape[1], p)
    z = x1 * long_conv(v, spec[:, 0], p['hy_bias'][0])
    return x2 * long_conv(z, spec[:, 1], p['hy_bias'][1])


PEER_SEL = PEER_HEADS * PEER_TOPK
PEER_HALF = D_MODEL // 2
PEER_CHUNK = 4096
PEER_TT = 16
SC_GATHER_ROWS = 64


def pack_bf16_halves(tab):
    bits = lax.bitcast_convert_type(tab.astype(jnp.bfloat16), jnp.uint16).astype(jnp.uint32)
    half = tab.shape[1] // 2
    return bits[:, :half] | (bits[:, half:] << 16)


def sc_gather_pair(u_tab, v_tab, idx):
    n = idx.shape[0]
    W = u_tab.shape[1]
    info = plsc.get_sparse_core_info()
    nc, ns = info.num_cores, info.num_subcores
    per_w = n // (nc * ns)
    G = SC_GATHER_ROWS
    assert per_w * nc * ns == n and per_w % G == 0
    nsteps = per_w // G
    mesh = plsc.VectorSubcoreMesh(core_axis_name="c", subcore_axis_name="s")
    out = jax.ShapeDtypeStruct((n, W), u_tab.dtype)

    def body(u_hbm, v_hbm, idx_hbm, uo_hbm, vo_hbm, idx_v, urows, vrows, usem, vsem):
        wid = lax.axis_index("s") * nc + lax.axis_index("c")
        base = wid * per_w
        pltpu.sync_copy(idx_hbm.at[pl.ds(base, per_w)], idx_v)

        @pl.loop(0, nsteps)
        def _(i):
            off = pl.multiple_of(i * G, G)
            ids = idx_v.at[pl.ds(off, G)]
            cu = pltpu.async_copy(u_hbm.at[ids], urows, usem)
            cv = pltpu.async_copy(v_hbm.at[ids], vrows, vsem)
            cu.wait()
            pltpu.sync_copy(urows, uo_hbm.at[pl.ds(base + off, G)])
            cv.wait()
            pltpu.sync_copy(vrows, vo_hbm.at[pl.ds(base + off, G)])

    fn = pl.kernel(body, out_type=(out, out), mesh=mesh,
                   scratch_types=[pltpu.VMEM((per_w,), jnp.int32),
                                  pltpu.VMEM((G, W), u_tab.dtype), pltpu.VMEM((G, W), u_tab.dtype),
                                  pltpu.SemaphoreType.DMA, pltpu.SemaphoreType.DMA],
                   name="peer_sc_gather")
    return fn(u_tab, v_tab, idx)


def _unpack_halves(w):
    lo = lax.bitcast_convert_type(w << 16, F32)
    hi = lax.bitcast_convert_type(w & jnp.uint32(0xFFFF0000), F32)
    return lo, hi


def _peer_apply_body(x_ref, g_ref, ug_ref, vg_ref, o_ref):
    TT = x_ref.shape[0]
    gpad = jnp.concatenate([g_ref[...], jnp.zeros((PEER_SEL - TT, PEER_SEL), F32)], axis=0)
    gT = gpad.T
    for t in range(TT):
        rows = pl.ds(t * PEER_SEL, PEER_SEL)
        ulo, uhi = _unpack_halves(ug_ref[rows, :])
        xl = x_ref[pl.ds(t, 1), 0:PEER_HALF]
        xh = x_ref[pl.ds(t, 1), PEER_HALF:D_MODEL]
        dots = jnp.sum(ulo * xl + uhi * xh, axis=-1, keepdims=True)
        w = jax.nn.gelu(dots) * gT[:, t:t + 1]
        vlo, vhi = _unpack_halves(vg_ref[rows, :])
        o_ref[pl.ds(t, 1), 0:PEER_HALF] = jnp.sum(vlo * w, axis=0, keepdims=True)
        o_ref[pl.ds(t, 1), PEER_HALF:D_MODEL] = jnp.sum(vhi * w, axis=0, keepdims=True)


def peer_apply(x, gate, ug, vg):
    T = x.shape[0]
    TT = PEER_TT
    return pl.pallas_call(
        _peer_apply_body,
        grid=(T // TT,),
        in_specs=[pl.BlockSpec((TT, D_MODEL), lambda i: (i, 0)),
                  pl.BlockSpec((TT, PEER_SEL), lambda i: (i, 0)),
                  pl.BlockSpec((TT * PEER_SEL, PEER_HALF), lambda i: (i, 0)),
                  pl.BlockSpec((TT * PEER_SEL, PEER_HALF), lambda i: (i, 0))],
        out_specs=pl.BlockSpec((TT, D_MODEL), lambda i: (i, 0)),
        out_shape=jax.ShapeDtypeStruct((T, D_MODEL), F32),
        compiler_params=pltpu.CompilerParams(dimension_semantics=("parallel",),
                                             vmem_limit_bytes=40 * 1024 * 1024),
        name="peer_apply",
    )(x, gate, ug, vg)


ROUTE_TT = 256


def _peer_route_body(s_ref, e_ref, g_ref, xs_ref, sv_ref, si_ref, cand_ref, cidx_ref, ts_ref):
    K, NK, TT = PEER_TOPK, PEER_NKEYS, s_ref.shape[-1]
    NEG = -jnp.inf
    xs_ref[...] = s_ref[0]
    kiota = lax.broadcasted_iota(jnp.int32, (NK, TT), 0).astype(F32)

    def half_topk(it, carry):
        for c in range(2):
            x = xs_ref[c]
            m = jnp.max(x, axis=0, keepdims=True)
            idx = jnp.min(jnp.where(x == m, kiota, float(NK)), axis=0, keepdims=True)
            xs_ref[c] = jnp.where(kiota == idx, NEG, x)
            sv_ref[c, pl.ds(it, 1), :] = m
            si_ref[c, pl.ds(it, 1), :] = idx
        return carry

    lax.fori_loop(0, K, half_topk, 0)

    for i in range(K):
        cand_ref[pl.ds(i * K, K), :] = sv_ref[0, i:i + 1, :] + sv_ref[1]
        cidx_ref[pl.ds(i * K, K), :] = si_ref[0, i:i + 1, :] * float(NK) + si_ref[1]
    piota = lax.broadcasted_iota(jnp.int32, (K * K, TT), 0).astype(F32)

    def pair_topk(it, carry):
        x = cand_ref[...]
        m = jnp.max(x, axis=0, keepdims=True)
        pos = jnp.min(jnp.where(x == m, piota, float(K * K)), axis=0, keepdims=True)
        sel = piota == pos
        cand_ref[...] = jnp.where(sel, NEG, x)
        ts_ref[pl.ds(it, 1), :] = m
        e_ref[0, pl.ds(it, 1), :] = jnp.max(jnp.where(sel, cidx_ref[...], -1.0), axis=0,
                                            keepdims=True).astype(jnp.int32)
        return carry

    lax.fori_loop(0, K, pair_topk, 0)
    ts = ts_ref[...]
    ex = jnp.exp(ts - ts[0:1, :])
    g_ref[0] = ex / jnp.sum(ex, axis=0, keepdims=True)


def peer_route_topk(sT):
    H, _, NK, N = sT.shape
    TT, K = ROUTE_TT, PEER_TOPK
    out_spec = pl.BlockSpec((1, K, TT), lambda h, i: (h, 0, i))
    return pl.pallas_call(
        _peer_route_body,
        grid=(H, N // TT),
        in_specs=[pl.BlockSpec((1, 2, NK, TT), lambda h, i: (h, 0, 0, i))],
        out_specs=[out_spec, out_spec],
        out_shape=[jax.ShapeDtypeStruct((H, K, N), jnp.int32), jax.ShapeDtypeStruct((H, K, N), F32)],
        scratch_shapes=[pltpu.VMEM((2, NK, TT), F32), pltpu.VMEM((2, K, TT), F32), pltpu.VMEM((2, K, TT), F32),
                        pltpu.VMEM((K * K, TT), F32), pltpu.VMEM((K * K, TT), F32), pltpu.VMEM((K, TT), F32)],
        compiler_params=pltpu.CompilerParams(dimension_semantics=("parallel", "parallel")),
        name="peer_route_topk",
    )(sT)


def peer_route(xf, p):
    N = xf.shape[0]
    wq, keys = p['peer_wq'], p['peer_keys'].astype(F32)
    q = (xf @ wq).astype(F32).reshape(N, PEER_HEADS, 2, PEER_DQ // 2)
    sT = jnp.einsum('thcq,hckq->hckt', q, keys)
    eT, gT = peer_route_topk(sT)
    return (eT.transpose(2, 0, 1).reshape(N, PEER_SEL), gT.transpose(2, 0, 1).reshape(N, PEER_SEL))


def peer_ffn(h, p):
    Bn, L, D = h.shape
    N = Bn * L
    xf = h.reshape(N, D)
    eidx, gate = peer_route(xf, p)
    u_pk, v_pk = p['peer_u_pk'], p['peer_v_pk']
    outs = []
    for c0 in range(0, N, PEER_CHUNK):
        rows = slice(c0, c0 + PEER_CHUNK)
        ug, vg = sc_gather_pair(u_pk, v_pk, eidx[rows].reshape(-1))
        outs.append(peer_apply(xf[rows], gate[rows], ug, vg))
    return jnp.concatenate(outs, axis=0).reshape(Bn, L, D)


def merge_groups(ys, g, dtype):
    outs, off = [], 0
    for y, w in zip(ys, GROUP_WIDTHS):
        outs.append(rmsnorm(y, g[off:off + w]).astype(dtype))
        off += w
    return jnp.concatenate(outs, axis=-1)


def token_mixers(h_c, h_l, p, need_ctx):
    zc = split_cols(h_c @ p['w_in'])
    zl = split_cols(h_l @ p['w_in'])
    a_c, a_l = mixer_rglru(zc[0:2], zl[0:2], p, need_ctx)
    b_c, b_l = mixer_rwkv(zc[2:8], zl[2:8], p, need_ctx)
    m_c, m_l = mixer_mlstm(zc[8:13], zl[8:13], p, need_ctx)
    d_l = mixer_hyena(zl[13:16], p)
    o_l = merge_groups([a_l, b_l, m_l, d_l], p['grp_g'], h_l.dtype) @ p['w_out']
    if not need_ctx:
        return None, o_l
    d_c = mixer_hyena(zc[13:16], p)
    o_c = merge_groups([a_c, b_c, m_c, d_c], p['grp_g'], h_c.dtype) @ p['w_out']
    return o_c, o_l


def trunk_layer(x_l, x_c, c, c_ctx, p, need_ctx):
    Bn = c.shape[0]
    mod_l = (jax.nn.silu(c) @ p['ada_w'] + p['ada_b']).reshape(Bn, 6, 1, D_MODEL)
    mod_c = (jax.nn.silu(c_ctx) @ p['ada_w'] + p['ada_b']).reshape(6, 1, 1, D_MODEL)
    h_l = modulate(rmsnorm(x_l, p['norm1_g']), mod_l[:, 0], mod_l[:, 1])
    h_c = modulate(rmsnorm(x_c, p['norm1_g']), mod_c[0], mod_c[1])
    o_c, o_l = token_mixers(h_c, h_l, p, need_ctx)
    x_l = x_l + mod_l[:, 2] * o_l
    x_l = x_l + mod_l[:, 5] * peer_ffn(modulate(rmsnorm(x_l, p['norm2_g']), mod_l[:, 3], mod_l[:, 4]), p)
    if need_ctx:
        x_c = x_c + mod_c[2] * o_c
        x_c = x_c + mod_c[5] * peer_ffn(modulate(rmsnorm(x_c, p['norm2_g']), mod_c[3], mod_c[4]), p)
    return x_l, x_c


def _final_norm_body(x_ref, g_ref, o_ref):
    xf = x_ref[...]
    y = xf * lax.rsqrt(jnp.mean(xf * xf, axis=-1, keepdims=True) + EPS)
    o_ref[...] = y * g_ref[...]


def final_rmsnorm(x, g):
    Bn, L, D = x.shape
    rows = Bn * L
    tile = 1024
    out = pl.pallas_call(
        _final_norm_body,
        grid=(rows // tile,),
        in_specs=[pl.BlockSpec((tile, D), lambda i: (i, 0)), pl.BlockSpec((1, D), lambda i: (0, 0))],
        out_specs=pl.BlockSpec((tile, D), lambda i: (i, 0)),
        out_shape=jax.ShapeDtypeStruct((rows, D), x.dtype),
        name="final_rmsnorm",
    )(x.reshape(rows, D), g.reshape(1, D))
    return out.reshape(Bn, L, D)


def kernel(x, c, ctx, c_ctx, ada_w, ada_b, norm1_g, norm2_g, w_in, w_out, grp_g,
           lru_conv_w, lru_conv_b, lru_wr, lru_br, lru_wi, lru_bi, lru_lam,
           rwkv_mu, rwkv_w0, rwkv_w2, rwkv_a0, rwkv_a2, rwkv_g2, rwkv_kk, rwkv_ka, rwkv_rk, rwkv_ln_g, rwkv_ln_b,
           mlstm_conv_w, mlstm_conv_b, mlstm_gate_b,
           hy_conv_w, hy_conv_b, hy_w1, hy_b1, hy_w2, hy_b2, hy_w3, hy_freq, hy_bias,
           peer_wq, peer_keys, peer_u, peer_v, final_g):
    x_l, x_c = x, ctx
    for i in range(DEPTH):
        p = dict(ada_w=ada_w[i], ada_b=ada_b[i], norm1_g=norm1_g[i], norm2_g=norm2_g[i], w_in=w_in[i],
                 w_out=w_out[i], grp_g=grp_g[i],
                 lru_conv_w=lru_conv_w[i], lru_conv_b=lru_conv_b[i], lru_wr=lru_wr[i], lru_br=lru_br[i],
                 lru_wi=lru_wi[i], lru_bi=lru_bi[i], lru_lam=lru_lam[i],
                 rwkv_mu=rwkv_mu[i], rwkv_w0=rwkv_w0[i], rwkv_w2=rwkv_w2[i], rwkv_a0=rwkv_a0[i],
                 rwkv_a2=rwkv_a2[i], rwkv_g2=rwkv_g2[i], rwkv_kk=rwkv_kk[i], rwkv_ka=rwkv_ka[i],
                 rwkv_rk=rwkv_rk[i], rwkv_ln_g=rwkv_ln_g[i], rwkv_ln_b=rwkv_ln_b[i],
                 mlstm_conv_w=mlstm_conv_w[i], mlstm_conv_b=mlstm_conv_b[i], mlstm_gate_b=mlstm_gate_b[i],
                 hy_conv_w=hy_conv_w[i], hy_conv_b=hy_conv_b[i], hy_w1=hy_w1[i], hy_b1=hy_b1[i],
                 hy_w2=hy_w2[i], hy_b2=hy_b2[i], hy_w3=hy_w3[i], hy_freq=hy_freq[i], hy_bias=hy_bias[i],
                 peer_wq=peer_wq[i], peer_keys=peer_keys[i],
                 peer_u_pk=pack_bf16_halves(peer_u[i]), peer_v_pk=pack_bf16_halves(peer_v[i]))
        x_l, x_c = trunk_layer(x_l, x_c, c, c_ctx, p, i < DEPTH - 1)
    return final_rmsnorm(x_l, final_g)
```

```python
import functools
import math
import jax, jax.numpy as jnp
from jax import lax
import numpy as np
from jax.experimental import pallas as pl
from jax.experimental.pallas import tpu as pltpu
from jax.experimental.pallas import tpu_sc as plsc

D_MODEL = 1024
BATCH = 16
SEQ = 2048
DEPTH = 2

GRID_W = 64
CTX_LEN = 256
F32 = jnp.float32
EPS = 1e-6
HEAD_DIM = 64
D_MIX = D_MODEL
W_LRU = D_MIX // 4
W_RWKV = D_MIX // 4
W_MLSTM = D_MIX // 4
W_HYENA = D_MIX - W_LRU - W_RWKV - W_MLSTM
GROUP_WIDTHS = (W_LRU, W_RWKV, W_MLSTM, W_HYENA)
H_LRU = W_LRU // HEAD_DIM
H_RWKV = W_RWKV // HEAD_DIM
H_MLSTM = W_MLSTM // HEAD_DIM
CONV_W = 4
LRU_C = 8.0
RWKV_LORA_W = 32
RWKV_LORA_A = 32
RWKV_LORA_G = 64
RWKV_GN_EPS = 64e-5
MLSTM_CHUNK = 64
HYENA_ORDER = 2
HYENA_SHORT = 3
HYENA_EMB = 33
HYENA_HID = 64
HYENA_TARGET = 1e-2
HYENA_FAST = 0.3
HYENA_SLOW = 1.5
PEER_HEADS = 8
PEER_NKEYS = 128
PEER_EXPERTS = PEER_NKEYS * PEER_NKEYS
PEER_TOPK = 16
PEER_DQ = 256
PEER_BLOCK = 128
IN_SPLITS = (W_LRU, W_LRU, W_RWKV, W_RWKV, W_RWKV, RWKV_LORA_W, RWKV_LORA_A, RWKV_LORA_G, W_MLSTM, W_MLSTM, W_MLSTM, W_MLSTM, 4 * H_MLSTM, W_HYENA, W_HYENA, W_HYENA)
D_IN = sum(IN_SPLITS)


def rmsnorm(x, g):
    xf = x.astype(F32)
    y = xf * lax.rsqrt(jnp.mean(xf * xf, axis=-1, keepdims=True) + EPS)
    return (y * g.astype(F32)).astype(x.dtype)


def modulate(h, shift, scale):
    return h * (1 + scale) + shift


def flip_if(a, d, axis=1):
    return jnp.flip(a, axis) if d == 1 else a


def split_cols(z):
    offs = np.cumsum(IN_SPLITS)[:-1].tolist()
    return jnp.split(z, offs, axis=-1)


def dwconv(x, w, b, pad_left):
    K = w.shape[0]
    y = lax.conv_general_dilated(x, w[:, None, :].astype(x.dtype), (1,), [(pad_left, K - 1 - pad_left)],
                                 dimension_numbers=('NWC', 'WIO', 'NWC'), feature_group_count=x.shape[-1])
    return y + b.astype(x.dtype)


def token_shift(x, mu):
    prev = jnp.pad(x, ((0, 0), (1, 0), (0, 0)))[:, :-1]
    nxt = jnp.pad(x, ((0, 0), (0, 1), (0, 0)))[:, 1:]
    return x + mu[0] * (prev - x) + mu[1] * (nxt - x)


def split_heads(t, h):
    Bn, L, W = t.shape
    return t.reshape(Bn, L, h, W // h)


def to_colmajor(a):
    Bn, L, C = a.shape
    rows = L // GRID_W
    return a.reshape(Bn, rows, GRID_W, C).transpose(0, 2, 1, 3).reshape(Bn, L, C)


def from_colmajor(a):
    Bn, L, C = a.shape
    rows = L // GRID_W
    return a.reshape(Bn, GRID_W, rows, C).transpose(0, 2, 1, 3).reshape(Bn, L, C)


LRU_TC = 256


def _expm1(z):
    u = jnp.exp(z)
    return jnp.where(u == 1.0, z, (u - 1.0) * z / jnp.where(u == 1.0, 1.0, jnp.log(u)))


def _lru_body(x_ref, h0_ref, wr_ref, wi_ref, br_ref, bi_ref, sp_ref, h_ref, hT_ref, st_ref, *, reverse):
    TC, W = x_ref.shape[1], x_ref.shape[2]

    @pl.when(pl.program_id(1) == 0)
    def _():
        st_ref[...] = h0_ref[0]

    x = x_ref[0]
    r = jax.nn.sigmoid(jnp.dot(x, wr_ref[...], preferred_element_type=F32) + br_ref[...])
    i = jax.nn.sigmoid(jnp.dot(x, wi_ref[...], preferred_element_type=F32) + bi_ref[...])
    log_a = -LRU_C * r * sp_ref[...]
    a = jnp.exp(log_a)
    b = jnp.sqrt(-_expm1(2.0 * log_a)) * (i * x)
    row = lax.broadcasted_iota(jnp.int32, (TC, W), 0)
    s = 1
    while s < TC:
        if reverse:
            keep = row < TC - s
            a_sh = jnp.where(keep, pltpu.roll(a, TC - s, 0), 1.0)
            b_sh = jnp.where(keep, pltpu.roll(b, TC - s, 0), 0.0)
        else:
            keep = row >= s
            a_sh = jnp.where(keep, pltpu.roll(a, s, 0), 1.0)
            b_sh = jnp.where(keep, pltpu.roll(b, s, 0), 0.0)
        b = b + a * b_sh
        a = a * a_sh
        s *= 2
    h = b + a * st_ref[...]
    h_ref[0] = h
    last = h[0:1, :] if reverse else h[TC - 1:TC, :]
    st_ref[...] = last
    hT_ref[0] = last


def lru_scan_dir(xc, h0, wr_bd, wi_bd, br, bi, sp, reverse):
    Bn, L, W = xc.shape
    TC = min(LRU_TC, L)
    n = L // TC
    tmap = (lambda b, c: (b, n - 1 - c, 0)) if reverse else (lambda b, c: (b, c, 0))
    wspec = pl.BlockSpec((W, W), lambda b, c: (0, 0))
    vspec = pl.BlockSpec((1, W), lambda b, c: (0, 0))
    sspec = pl.BlockSpec((1, 1, W), lambda b, c: (b, 0, 0))
    return pl.pallas_call(
        functools.partial(_lru_body, reverse=reverse),
        grid=(Bn, n),
        in_specs=[pl.BlockSpec((1, TC, W), tmap), sspec, wspec, wspec, vspec, vspec, vspec],
        out_specs=[pl.BlockSpec((1, TC, W), tmap), sspec],
        out_shape=[jax.ShapeDtypeStruct((Bn, L, W), F32), jax.ShapeDtypeStruct((Bn, 1, W), F32)],
        scratch_shapes=[pltpu.VMEM((1, W), F32)],
        compiler_params=pltpu.CompilerParams(dimension_semantics=("parallel", "arbitrary")),
        name="rglru_scan",
    )(xc, h0, wr_bd, wi_bd, br, bi, sp)


def head_block_diag(w):
    D2, H, N, _ = w.shape
    return jnp.einsum('dhij,hg->dhigj', w, jnp.eye(H, dtype=w.dtype)).reshape(D2, H * N, H * N)


def mixer_rglru(f_c, f_l, p, need_ctx):
    (x_c, g_c), (x_l, g_l) = f_c, f_l
    xc_c = dwconv(x_c, p['lru_conv_w'], p['lru_conv_b'], CONV_W // 2).astype(F32)
    xc_l = dwconv(x_l, p['lru_conv_w'], p['lru_conv_b'], CONV_W // 2).astype(F32)
    h0 = jnp.zeros((xc_l.shape[0], 1, W_LRU), F32)
    wr, wi = head_block_diag(p['lru_wr']), head_block_diag(p['lru_wi'])
    sp = jax.nn.softplus(-p['lru_lam'])
    hs_c, hs_l = [], []
    for d in range(2):
        gates = (wr[d], wi[d], p['lru_br'][d][None], p['lru_bi'][d][None], sp[d][None])
        hc, st = lru_scan_dir(xc_c, h0, *gates, reverse=(d == 1))
        hl, _ = lru_scan_dir(xc_l, st, *gates, reverse=(d == 1))
        hs_c.append(hc)
        hs_l.append(hl)
    y_l = jax.nn.gelu(g_l.astype(F32)) * (hs_l[0] + hs_l[1])
    y_c = jax.nn.gelu(g_c.astype(F32)) * (hs_c[0] + hs_c[1]) if need_ctx else None
    return y_c, y_l


def rwkv_shift(f, p):
    r, k, v, zw, za, zg = [t.astype(F32) for t in f]
    mu = p['rwkv_mu']
    return (token_shift(r, mu[0]), token_shift(k, mu[1]), token_shift(v, mu[2]), zw, za, zg)


def rwkv_dir_inputs(f, p, d):
    r, k, v, zw, za, _ = f
    w_log = -jax.nn.softplus(-(p['rwkv_w0'][d] + jnp.tanh(zw) @ p['rwkv_w2'][d])) - 0.5
    decay = jnp.exp(-jnp.exp(w_log))
    a = jax.nn.sigmoid(p['rwkv_a0'][d] + za @ p['rwkv_a2'][d])
    kk = split_heads(k * p['rwkv_kk'], H_RWKV)
    kk = kk / jnp.maximum(jnp.linalg.norm(kk, axis=-1, keepdims=True), 1e-12)
    kd = k * (1 + (a - 1) * p['rwkv_ka'])
    return [split_heads(r, H_RWKV), split_heads(decay, H_RWKV), split_heads(kd, H_RWKV),
            split_heads(v, H_RWKV), kk, kk * split_heads(a, H_RWKV)]


RWKV_TC = HEAD_DIM
RWKV_PAIRS = 32


def _split_bf16(x):
    hi_f = lax.bitcast_convert_type(lax.bitcast_convert_type(x, jnp.uint32) & jnp.uint32(0xFFFF0000), F32)
    return hi_f.astype(jnp.bfloat16), (x - hi_f).astype(jnp.bfloat16)


def _segsum_bcast(x, ones2):
    hi, lo = _split_bf16(x)
    return jnp.dot(jnp.concatenate([hi, lo], axis=-1), ones2, preferred_element_type=F32)


def _rwkv_body(kk_ref, w_ref, kka_ref, k_ref, v_ref, r_ref, y_ref, s_ref):
    NP, N = RWKV_PAIRS, HEAD_DIM

    @pl.when(pl.program_id(1) == 0)
    def _():
        s_ref[...] = jnp.zeros_like(s_ref)

    y_ref[...] = jnp.zeros_like(y_ref)
    lane = lax.broadcasted_iota(jnp.int32, (N, 2 * N), 1)
    row = lax.broadcasted_iota(jnp.int32, (N, 2 * N), 0)
    diag = (lane % N) == row
    seg = (lax.broadcasted_iota(jnp.int32, (4 * N, 2 * N), 0) % (2 * N)) // N == \
        lax.broadcasted_iota(jnp.int32, (4 * N, 2 * N), 1) // N
    ones2 = jnp.where(seg, 1.0, 0.0).astype(jnp.bfloat16)

    seg4 = lax.broadcasted_iota(jnp.int32, (4 * N, 4 * N), 0) // N == \
        lax.broadcasted_iota(jnp.int32, (4 * N, 4 * N), 1) // N
    ones4 = jnp.where(seg4, 1.0, 0.0).astype(jnp.bfloat16)
    H = NP // 2

    def segsum_bcast_1pass(x):
        xb = x.astype(jnp.bfloat16)
        lhs = jnp.concatenate([xb[:H].reshape(H * N, 2 * N), xb[H:].reshape(H * N, 2 * N)], axis=-1)
        out = jnp.dot(lhs, ones4, preferred_element_type=F32)
        return jnp.concatenate([out[:, :2 * N].reshape(H, N, 2 * N), out[:, 2 * N:].reshape(H, N, 2 * N)], axis=0)

    def step(t, carry):
        rowv = lambda ref: ref[:, pl.ds(t, 1), :]
        S = s_ref[...]
        SA = _segsum_bcast((S * rowv(kk_ref)).reshape(NP * N, 2 * N), ones2).reshape(NP, N, 2 * N)
        VB = segsum_bcast_1pass(jnp.where(diag, rowv(v_ref), 0.0))
        S = S * rowv(w_ref) - SA * rowv(kka_ref) + VB * rowv(k_ref)
        s_ref[...] = S
        Y = segsum_bcast_1pass(S * rowv(r_ref))
        y_ref[:, 0] = jnp.where((lane % N) == t, Y, y_ref[:, 0])
        return carry

    lax.fori_loop(0, RWKV_TC, step, 0)


def rwkv_scan_pairs(kk, w, kka, k, v, r):
    NPT, L, _ = kk.shape
    NP, TC, N = RWKV_PAIRS, RWKV_TC, HEAD_DIM
    in_spec = pl.BlockSpec((NP, TC, 2 * N), lambda p, c: (p, c, 0))
    return pl.pallas_call(
        _rwkv_body,
        grid=(NPT // NP, L // TC),
        in_specs=[in_spec] * 6,
        out_specs=pl.BlockSpec((NP, 1, N, 2 * N), lambda p, c: (p, c, 0, 0)),
        out_shape=jax.ShapeDtypeStruct((NPT, L // TC, N, 2 * N), F32),
        scratch_shapes=[pltpu.VMEM((NP, N, 2 * N), F32)],
        compiler_params=pltpu.CompilerParams(dimension_semantics=("parallel", "arbitrary"),
                                             vmem_limit_bytes=40 * 1024 * 1024),
        name="rwkv7_scan",
    )(kk, w, kka, k, v, r)


def rwkv7_scan_bidir(ins_c, ins_l):
    Bn, Lc = ins_c[0][0].shape[:2]
    Ll = ins_l[0][0].shape[1]
    L = Lc + Ll

    def seq(j):
        xs = [jnp.concatenate([flip_if(ins_c[d][j], d), flip_if(ins_l[d][j], d)], axis=1) for d in range(2)]
        x = jnp.stack(xs).reshape(2, Bn, L, H_RWKV // 2, 2 * HEAD_DIM)
        return x.transpose(0, 1, 3, 2, 4).reshape(2 * Bn * (H_RWKV // 2), L, 2 * HEAD_DIM)

    r, w, k, v, kk, kka = [seq(j) for j in range(6)]
    yT = rwkv_scan_pairs(kk, w, kka, k, v, r)
    y = yT.reshape(2, Bn, H_RWKV // 2, L // HEAD_DIM, HEAD_DIM, 2, HEAD_DIM)
    y = y.transpose(0, 1, 3, 6, 2, 5, 4).reshape(2, Bn, L, H_RWKV, HEAD_DIM)
    return ([flip_if(y[d, :, :Lc], d) for d in range(2)], [flip_if(y[d, :, Lc:], d) for d in range(2)])


def rwkv_bonus(ins, p):
    r, _, kd, v = ins[:4]
    return jnp.sum(r * kd * p['rwkv_rk'], axis=-1, keepdims=True) * v


def rwkv_out(y, bonus, zg, p):
    Bn, L, H, N = y.shape
    mu = jnp.mean(y, axis=-1, keepdims=True)
    var = jnp.mean(jnp.square(y - mu), axis=-1, keepdims=True)
    yn = ((y - mu) * lax.rsqrt(var + RWKV_GN_EPS)).reshape(Bn, L, H * N) * p['rwkv_ln_g'] + p['rwkv_ln_b']
    g = jax.nn.sigmoid(zg) @ p['rwkv_g2']
    return (yn + bonus.reshape(Bn, L, H * N)) * g


def mixer_rwkv(f_c, f_l, p, need_ctx):
    fc, fl = rwkv_shift(f_c, p), rwkv_shift(f_l, p)
    ins_c = [rwkv_dir_inputs(fc, p, d) for d in range(2)]
    ins_l = [rwkv_dir_inputs(fl, p, d) for d in range(2)]
    o_c, o_l = rwkv7_scan_bidir(ins_c, ins_l)
    y_c, y_l = [], []
    for d in range(2):
        y_l.append((o_l[d], rwkv_bonus(ins_l[d], p)))
        if need_ctx:
            y_c.append((o_c[d], rwkv_bonus(ins_c[d], p)))
    out_l = rwkv_out(y_l[0][0] + y_l[1][0], y_l[0][1] + y_l[1][1], fl[5], p)
    out_c = rwkv_out(y_c[0][0] + y_c[1][0], y_c[0][1] + y_c[1][1], fc[5], p) if need_ctx else None
    return out_c, out_l


def mlstm_prep(f, p):
    q, k, v, o, gz = f
    Bn, L, _ = q.shape
    qk = jax.nn.silu(dwconv(jnp.concatenate([q, k], axis=-1), p['mlstm_conv_w'], p['mlstm_conv_b'], CONV_W // 2)).astype(F32)
    q, k = jnp.split(qk, 2, axis=-1)
    heads = lambda t: t.reshape(Bn, L, H_MLSTM, HEAD_DIM).transpose(0, 2, 1, 3)
    gates = (gz.astype(F32).reshape(Bn, L, 2, 2, H_MLSTM) + p['mlstm_gate_b']).transpose(2, 3, 0, 4, 1)
    return heads(q) * HEAD_DIM ** -0.5, heads(k), heads(v.astype(F32)), o.astype(F32), gates


def mlstm_chunkwise(q, k, v, ig, lf, state):
    Bn, H, L, N = q.shape
    T = MLSTM_CHUNK
    nc = L // T
    ch = lambda a: jnp.moveaxis(a.reshape((Bn, H, nc, T) + a.shape[3:]), 2, 0)
    mask = jnp.tril(jnp.ones((T, T), dtype=bool))

    def step(carry, inp):
        C, n, m = carry
        qc, kc, vc, ic, fc = inp
        b = jnp.cumsum(fc, axis=-1)
        logd = jnp.where(mask, b[..., :, None] - b[..., None, :] + ic[..., None, :], -jnp.inf)
        inter = b + m[..., None]
        mt = jnp.maximum(inter, jnp.max(logd, axis=-1))
        s = jnp.einsum('bhtn,bhsn->bhts', qc, kc) * jnp.exp(logd - mt[..., None])
        e_inter = jnp.exp(inter - mt)
        num = jnp.einsum('bhts,bhsn->bhtn', s, vc) + e_inter[..., None] * jnp.einsum('bhvk,bhtk->bhtv', C, qc)
        den = jnp.sum(s, axis=-1) + e_inter * jnp.einsum('bhk,bhtk->bht', n, qc)
        h = num / jnp.maximum(jnp.abs(den), jnp.exp(-mt))[..., None]
        bT = b[..., -1]
        logw = bT[..., None] - b + ic
        m_new = jnp.maximum(bT + m, jnp.max(logw, axis=-1))
        wgt = jnp.exp(logw - m_new[..., None])
        dec = jnp.exp(bT + m - m_new)
        C = dec[..., None, None] * C + jnp.einsum('bhs,bhsv,bhsk->bhvk', wgt, vc, kc)
        n = dec[..., None] * n + jnp.einsum('bhs,bhsk->bhk', wgt, kc)
        return (C, n, m_new), h

    state, hs = lax.scan(step, state, (ch(q), ch(k), ch(v), ch(ig), ch(lf)))
    return jnp.moveaxis(hs, 0, 2).reshape(Bn, H, L, N), state


def mixer_mlstm(f_c, f_l, p, need_ctx):
    f_l = [to_colmajor(t) for t in f_l]
    qc, kc, vc, oc, gc = mlstm_prep(f_c, p)
    ql, kl, vl, ol, gl = mlstm_prep(f_l, p)
    Bn = ql.shape[0]
    st0 = (jnp.zeros((Bn, H_MLSTM, HEAD_DIM, HEAD_DIM), F32), jnp.zeros((Bn, H_MLSTM, HEAD_DIM), F32),
           jnp.zeros((Bn, H_MLSTM), F32))
    h_c, h_l = [], []
    for d in range(2):
        hc, st = mlstm_chunkwise(flip_if(qc, d, 2), flip_if(kc, d, 2), flip_if(vc, d, 2), flip_if(gc[d, 0], d, 2),
                                 jax.nn.log_sigmoid(flip_if(gc[d, 1], d, 2)), st0)
        hl, _ = mlstm_chunkwise(flip_if(ql, d, 2), flip_if(kl, d, 2), flip_if(vl, d, 2), flip_if(gl[d, 0], d, 2),
                                jax.nn.log_sigmoid(flip_if(gl[d, 1], d, 2)), st)
        h_c.append(flip_if(hc, d, 2))
        h_l.append(flip_if(hl, d, 2))
    merge_heads = lambda h: h.transpose(0, 2, 1, 3).reshape(h.shape[0], h.shape[2], W_MLSTM)
    y_l = from_colmajor(jax.nn.sigmoid(ol) * merge_heads(h_l[0] + h_l[1]))
    y_c = jax.nn.sigmoid(oc) * merge_heads(h_c[0] + h_c[1]) if need_ctx else None
    return y_c, y_l


def hyena_spectrum(L, p):
    pos = jnp.arange(L, dtype=F32)
    t = pos / (L - 1)
    bands = (HYENA_EMB - 1) // 2
    freqs = jnp.linspace(1e-4, bands - 1, bands, dtype=F32)
    ang = (2 * math.pi / L) * pos[:, None] * freqs[None, :]
    z = jnp.concatenate([t[:, None], jnp.cos(ang), -jnp.sin(ang)], axis=-1)
    h = jnp.sin(p['hy_freq'][0] * (z @ p['hy_w1'] + p['hy_b1']))
    h = jnp.sin(p['hy_freq'][1] * (h @ p['hy_w2'] + p['hy_b2']))
    h = (h @ p['hy_w3']).astype(F32).reshape(L, HYENA_ORDER, 2, W_HYENA)
    deltas = jnp.abs(jnp.linspace(math.log(HYENA_TARGET) / HYENA_SLOW, math.log(HYENA_TARGET) / HYENA_FAST,
                                  W_HYENA, dtype=F32))
    h = h * jnp.exp(-t[:, None, None, None] * deltas)
    fwd, bwd = h[:, :, 0], h[:, :, 1]
    two = jnp.concatenate([fwd, jnp.zeros_like(fwd[:1]), jnp.flip(bwd[1:], axis=0)], axis=0)
    two = two / (jnp.sum(jnp.abs(two), axis=0, keepdims=True) + EPS)
    return jnp.fft.rfft(two, axis=0)


def long_conv(u, spec, bias):
    L = u.shape[1]
    y = jnp.fft.irfft(jnp.fft.rfft(u, n=2 * L, axis=1) * spec, n=2 * L, axis=1)[:, :L]
    return y + u * bias


def mixer_hyena(f, p):
    u = dwconv(jnp.concatenate(f, axis=-1), p['hy_conv_w'], p['hy_conv_b'], HYENA_SHORT // 2).astype(F32)
    v, x1, x2 = jnp.split(u, 3, axis=-1)
    spec = hyena_spectrum(u.shape[1], p)
    z = x1 * long_conv(v, spec[:, 0], p['hy_bias'][0])
    return x2 * long_conv(z, spec[:, 1], p['hy_bias'][1])


PEER_SEL = PEER_HEADS * PEER_TOPK
PEER_HALF = D_MODEL // 2
PEER_CHUNK = 4096
PEER_TT = 16
SC_GATHER_ROWS = 64


def pack_bf16_halves(tab):
    bits = lax.bitcast_convert_type(tab.astype(jnp.bfloat16), jnp.uint16).astype(jnp.uint32)
    half = tab.shape[1] // 2
    return bits[:, :half] | (bits[:, half:] << 16)


def sc_gather_pair(u_tab, v_tab, idx):
    n = idx.shape[0]
    W = u_tab.shape[1]
    info = plsc.get_sparse_core_info()
    nc, ns = info.num_cores, info.num_subcores
    per_w = n // (nc * ns)
    G = SC_GATHER_ROWS
    assert per_w * nc * ns == n and per_w % G == 0
    nsteps = per_w // G
    mesh = plsc.VectorSubcoreMesh(core_axis_name="c", subcore_axis_name="s")
    out = jax.ShapeDtypeStruct((n, W), u_tab.dtype)

    def body(u_hbm, v_hbm, idx_hbm, uo_hbm, vo_hbm, idx_v, urows, vrows, usem, vsem):
        wid = lax.axis_index("s") * nc + lax.axis_index("c")
        base = wid * per_w
        pltpu.sync_copy(idx_hbm.at[pl.ds(base, per_w)], idx_v)

        @pl.loop(0, nsteps)
        def _(i):
            off = pl.multiple_of(i * G, G)
            ids = idx_v.at[pl.ds(off, G)]
            cu = pltpu.async_copy(u_hbm.at[ids], urows, usem)
            cv = pltpu.async_copy(v_hbm.at[ids], vrows, vsem)
            cu.wait()
            pltpu.sync_copy(urows, uo_hbm.at[pl.ds(base + off, G)])
            cv.wait()
            pltpu.sync_copy(vrows, vo_hbm.at[pl.ds(base + off, G)])

    fn = pl.kernel(body, out_type=(out, out), mesh=mesh,
                   scratch_types=[pltpu.VMEM((per_w,), jnp.int32),
                                  pltpu.VMEM((G, W), u_tab.dtype), pltpu.VMEM((G, W), u_tab.dtype),
                                  pltpu.SemaphoreType.DMA, pltpu.SemaphoreType.DMA],
                   name="peer_sc_gather")
    return fn(u_tab, v_tab, idx)


def _unpack_halves(w):
    lo = lax.bitcast_convert_type(w << 16, F32)
    hi = lax.bitcast_convert_type(w & jnp.uint32(0xFFFF0000), F32)
    return lo, hi


def _peer_apply_body(x_ref, g_ref, ug_ref, vg_ref, o_ref):
    TT = x_ref.shape[0]
    gpad = jnp.concatenate([g_ref[...], jnp.zeros((PEER_SEL - TT, PEER_SEL), F32)], axis=0)
    gT = gpad.T
    for t in range(TT):
        rows = pl.ds(t * PEER_SEL, PEER_SEL)
        ulo, uhi = _unpack_halves(ug_ref[rows, :])
        xl = x_ref[pl.ds(t, 1), 0:PEER_HALF]
        xh = x_ref[pl.ds(t, 1), PEER_HALF:D_MODEL]
        dots = jnp.sum(ulo * xl + uhi * xh, axis=-1, keepdims=True)
        w = jax.nn.gelu(dots) * gT[:, t:t + 1]
        vlo, vhi = _unpack_halves(vg_ref[rows, :])
        o_ref[pl.ds(t, 1), 0:PEER_HALF] = jnp.sum(vlo * w, axis=0, keepdims=True)
        o_ref[pl.ds(t, 1), PEER_HALF:D_MODEL] = jnp.sum(vhi * w, axis=0, keepdims=True)


def peer_apply(x, gate, ug, vg):
    T = x.shape[0]
    TT = PEER_TT
    return pl.pallas_call(
        _peer_apply_body,
        grid=(T // TT,),
        in_specs=[pl.BlockSpec((TT, D_MODEL), lambda i: (i, 0)),
                  pl.BlockSpec((TT, PEER_SEL), lambda i: (i, 0)),
                  pl.BlockSpec((TT * PEER_SEL, PEER_HALF), lambda i: (i, 0)),
                  pl.BlockSpec((TT * PEER_SEL, PEER_HALF), lambda i: (i, 0))],
        out_specs=pl.BlockSpec((TT, D_MODEL), lambda i: (i, 0)),
        out_shape=jax.ShapeDtypeStruct((T, D_MODEL), F32),
        compiler_params=pltpu.CompilerParams(dimension_semantics=("parallel",),
                                             vmem_limit_bytes=40 * 1024 * 1024),
        name="peer_apply",
    )(x, gate, ug, vg)


ROUTE_TT = 256


def _peer_route_body(s_ref, e_ref, g_ref, xs_ref, sv_ref, si_ref, cand_ref, cidx_ref, ts_ref):
    K, NK, TT = PEER_TOPK, PEER_NKEYS, s_ref.shape[-1]
    NEG = -jnp.inf
    xs_ref[...] = s_ref[0]
    kiota = lax.broadcasted_iota(jnp.int32, (NK, TT), 0).astype(F32)

    def half_topk(it, carry):
        for c in range(2):
            x = xs_ref[c]
            m = jnp.max(x, axis=0, keepdims=True)
            idx = jnp.min(jnp.where(x == m, kiota, float(NK)), axis=0, keepdims=True)
            xs_ref[c] = jnp.where(kiota == idx, NEG, x)
            sv_ref[c, pl.ds(it, 1), :] = m
            si_ref[c, pl.ds(it, 1), :] = idx
        return carry

    lax.fori_loop(0, K, half_topk, 0)

    for i in range(K):
        cand_ref[pl.ds(i * K, K), :] = sv_ref[0, i:i + 1, :] + sv_ref[1]
        cidx_ref[pl.ds(i * K, K), :] = si_ref[0, i:i + 1, :] * float(NK) + si_ref[1]
    piota = lax.broadcasted_iota(jnp.int32, (K * K, TT), 0).astype(F32)

    def pair_topk(it, carry):
        x = cand_ref[...]
        m = jnp.max(x, axis=0, keepdims=True)
        pos = jnp.min(jnp.where(x == m, piota, float(K * K)), axis=0, keepdims=True)
        sel = piota == pos
        cand_ref[...] = jnp.where(sel, NEG, x)
        ts_ref[pl.ds(it, 1), :] = m
        e_ref[0, pl.ds(it, 1), :] = jnp.max(jnp.where(sel, cidx_ref[...], -1.0), axis=0,
                                            keepdims=True).astype(jnp.int32)
        return carry

    lax.fori_loop(0, K, pair_topk, 0)
    ts = ts_ref[...]
    ex = jnp.exp(ts - ts[0:1, :])
    g_ref[0] = ex / jnp.sum(ex, axis=0, keepdims=True)


def peer_route_topk(sT):
    H, _, NK, N = sT.shape
    TT, K = ROUTE_TT, PEER_TOPK
    out_spec = pl.BlockSpec((1, K, TT), lambda h, i: (h, 0, i))
    return pl.pallas_call(
        _peer_route_body,
        grid=(H, N // TT),
        in_specs=[pl.BlockSpec((1, 2, NK, TT), lambda h, i: (h, 0, 0, i))],
        out_specs=[out_spec, out_spec],
        out_shape=[jax.ShapeDtypeStruct((H, K, N), jnp.int32), jax.ShapeDtypeStruct((H, K, N), F32)],
        scratch_shapes=[pltpu.VMEM((2, NK, TT), F32), pltpu.VMEM((2, K, TT), F32), pltpu.VMEM((2, K, TT), F32),
                        pltpu.VMEM((K * K, TT), F32), pltpu.VMEM((K * K, TT), F32), pltpu.VMEM((K, TT), F32)],
        compiler_params=pltpu.CompilerParams(dimension_semantics=("parallel", "parallel")),
        name="peer_route_topk",
    )(sT)


def peer_route(xf, p):
    N = xf.shape[0]
    wq, keys = p['peer_wq'], p['peer_keys'].astype(F32)
    q = (xf @ wq).astype(F32).reshape(N, PEER_HEADS, 2, PEER_DQ // 2)
    sT = jnp.einsum('thcq,hckq->hckt', q, keys)
    eT, gT = peer_route_topk(sT)
    return (eT.transpose(2, 0, 1).reshape(N, PEER_SEL), gT.transpose(2, 0, 1).reshape(N, PEER_SEL))


def peer_ffn(h, p):
    Bn, L, D = h.shape
    N = Bn * L
    xf = h.reshape(N, D)
    eidx, gate = peer_route(xf, p)
    u_pk, v_pk = p['peer_u_pk'], p['peer_v_pk']
    outs = []
    for c0 in range(0, N, PEER_CHUNK):
        rows = slice(c0, c0 + PEER_CHUNK)
        ug, vg = sc_gather_pair(u_pk, v_pk, eidx[rows].reshape(-1))
        outs.append(peer_apply(xf[rows], gate[rows], ug, vg))
    return jnp.concatenate(outs, axis=0).reshape(Bn, L, D)


def merge_groups(ys, g, dtype):
    outs, off = [], 0
    for y, w in zip(ys, GROUP_WIDTHS):
        outs.append(rmsnorm(y, g[off:off + w]).astype(dtype))
        off += w
    return jnp.concatenate(outs, axis=-1)


def token_mixers(h_c, h_l, p, need_ctx):
    zc = split_cols(h_c @ p['w_in'])
    zl = split_cols(h_l @ p['w_in'])
    a_c, a_l = mixer_rglru(zc[0:2], zl[0:2], p, need_ctx)
    b_c, b_l = mixer_rwkv(zc[2:8], zl[2:8], p, need_ctx)
    m_c, m_l = mixer_mlstm(zc[8:13], zl[8:13], p, need_ctx)
    d_l = mixer_hyena(zl[13:16], p)
    o_l = merge_groups([a_l, b_l, m_l, d_l], p['grp_g'], h_l.dtype) @ p['w_out']
    if not need_ctx:
        return None, o_l
    d_c = mixer_hyena(zc[13:16], p)
    o_c = merge_groups([a_c, b_c, m_c, d_c], p['grp_g'], h_c.dtype) @ p['w_out']
    return o_c, o_l


def trunk_layer(x_l, x_c, c, c_ctx, p, need_ctx):
    Bn = c.shape[0]
    mod_l = (jax.nn.silu(c) @ p['ada_w'] + p['ada_b']).reshape(Bn, 6, 1, D_MODEL)
    mod_c = (jax.nn.silu(c_ctx) @ p['ada_w'] + p['ada_b']).reshape(6, 1, 1, D_MODEL)
    h_l = modulate(rmsnorm(x_l, p['norm1_g']), mod_l[:, 0], mod_l[:, 1])
    h_c = modulate(rmsnorm(x_c, p['norm1_g']), mod_c[0], mod_c[1])
    o_c, o_l = token_mixers(h_c, h_l, p, need_ctx)
    x_l = x_l + mod_l[:, 2] * o_l
    x_l = x_l + mod_l[:, 5] * peer_ffn(modulate(rmsnorm(x_l, p['norm2_g']), mod_l[:, 3], mod_l[:, 4]), p)
    if need_ctx:
        x_c = x_c + mod_c[2] * o_c
        x_c = x_c + mod_c[5] * peer_ffn(modulate(rmsnorm(x_c, p['norm2_g']), mod_c[3], mod_c[4]), p)
    return x_l, x_c


def _final_norm_body(x_ref, g_ref, o_ref):
    xf = x_ref[...]
    y = xf * lax.rsqrt(jnp.mean(xf * xf, axis=-1, keepdims=True) + EPS)
    o_ref[...] = y * g_ref[...]


def final_rmsnorm(x, g):
    Bn, L, D = x.shape
    rows = Bn * L
    tile = 1024
    out = pl.pallas_call(
        _final_norm_body,
        grid=(rows // tile,),
        in_specs=[pl.BlockSpec((tile, D), lambda i: (i, 0)), pl.BlockSpec((1, D), lambda i: (0, 0))],
        out_specs=pl.BlockSpec((tile, D), lambda i: (i, 0)),
        out_shape=jax.ShapeDtypeStruct((rows, D), x.dtype),
        name="final_rmsnorm",
    )(x.reshape(rows, D), g.reshape(1, D))
    return out.reshape(Bn, L, D)


def kernel(x, c, ctx, c_ctx, ada_w, ada_b, norm1_g, norm2_g, w_in, w_out, grp_g,
           lru_conv_w, lru_conv_b, lru_wr, lru_br, lru_wi, lru_bi, lru_lam,
           rwkv_mu, rwkv_w0, rwkv_w2, rwkv_a0, rwkv_a2, rwkv_g2, rwkv_kk, rwkv_ka, rwkv_rk, rwkv_ln_g, rwkv_ln_b,
           mlstm_conv_w, mlstm_conv_b, mlstm_gate_b,
           hy_conv_w, hy_conv_b, hy_w1, hy_b1, hy_w2, hy_b2, hy_w3, hy_freq, hy_bias,
           peer_wq, peer_keys, peer_u, peer_v, final_g):
    x_l, x_c = x, ctx
    for i in range(DEPTH):
        p = dict(ada_w=ada_w[i], ada_b=ada_b[i], norm1_g=norm1_g[i], norm2_g=norm2_g[i], w_in=w_in[i],
                 w_out=w_out[i], grp_g=grp_g[i],
                 lru_conv_w=lru_conv_w[i], lru_conv_b=lru_conv_b[i], lru_wr=lru_wr[i], lru_br=lru_br[i],
                 lru_wi=lru_wi[i], lru_bi=lru_bi[i], lru_lam=lru_lam[i],
                 rwkv_mu=rwkv_mu[i], rwkv_w0=rwkv_w0[i], rwkv_w2=rwkv_w2[i], rwkv_a0=rwkv_a0[i],
                 rwkv_a2=rwkv_a2[i], rwkv_g2=rwkv_g2[i], rwkv_kk=rwkv_kk[i], rwkv_ka=rwkv_ka[i],
                 rwkv_rk=rwkv_rk[i], rwkv_ln_g=rwkv_ln_g[i], rwkv_ln_b=rwkv_ln_b[i],
                 mlstm_conv_w=mlstm_conv_w[i], mlstm_conv_b=mlstm_conv_b[i], mlstm_gate_b=mlstm_gate_b[i],
                 hy_conv_w=hy_conv_w[i], hy_conv_b=hy_conv_b[i], hy_w1=hy_w1[i], hy_b1=hy_b1[i],
                 hy_w2=hy_w2[i], hy_b2=hy_b2[i], hy_w3=hy_w3[i], hy_freq=hy_freq[i], hy_bias=hy_bias[i],
                 peer_wq=peer_wq[i], peer_keys=peer_keys[i],
                 peer_u_pk=pack_bf16_halves(peer_u[i]), peer_v_pk=pack_bf16_halves(peer_v[i]))
        x_l, x_c = trunk_layer(x_l, x_c, c, c_ctx, p, i < DEPTH - 1)
    return final_rmsnorm(x_l, final_g)
```

```python
import functools
import math
import jax, jax.numpy as jnp
from jax import lax
import numpy as np
from jax.experimental import pallas as pl
from jax.experimental.pallas import tpu as pltpu
from jax.experimental.pallas import tpu_sc as plsc

D_MODEL = 1024
BATCH = 16
SEQ = 2048
DEPTH = 2

GRID_W = 64
CTX_LEN = 256
F32 = jnp.float32
EPS = 1e-6
HEAD_DIM = 64
D_MIX = D_MODEL
W_LRU = D_MIX // 4
W_RWKV = D_MIX // 4
W_MLSTM = D_MIX // 4
W_HYENA = D_MIX - W_LRU - W_RWKV - W_MLSTM
GROUP_WIDTHS = (W_LRU, W_RWKV, W_MLSTM, W_HYENA)
H_LRU = W_LRU // HEAD_DIM
H_RWKV = W_RWKV // HEAD_DIM
H_MLSTM = W_MLSTM // HEAD_DIM
CONV_W = 4
LRU_C = 8.0
RWKV_LORA_W = 32
RWKV_LORA_A = 32
RWKV_LORA_G = 64
RWKV_GN_EPS = 64e-5
MLSTM_CHUNK = 64
HYENA_ORDER = 2
HYENA_SHORT = 3
HYENA_EMB = 33
HYENA_HID = 64
HYENA_TARGET = 1e-2
HYENA_FAST = 0.3
HYENA_SLOW = 1.5
PEER_HEADS = 8
PEER_NKEYS = 128
PEER_EXPERTS = PEER_NKEYS * PEER_NKEYS
PEER_TOPK = 16
PEER_DQ = 256
PEER_BLOCK = 128
IN_SPLITS = (W_LRU, W_LRU, W_RWKV, W_RWKV, W_RWKV, RWKV_LORA_W, RWKV_LORA_A, RWKV_LORA_G, W_MLSTM, W_MLSTM, W_MLSTM, W_MLSTM, 4 * H_MLSTM, W_HYENA, W_HYENA, W_HYENA)
D_IN = sum(IN_SPLITS)


def rmsnorm(x, g):
    xf = x.astype(F32)
    y = xf * lax.rsqrt(jnp.mean(xf * xf, axis=-1, keepdims=True) + EPS)
    return (y * g.astype(F32)).astype(x.dtype)


def modulate(h, shift, scale):
    return h * (1 + scale) + shift


def flip_if(a, d, axis=1):
    return jnp.flip(a, axis) if d == 1 else a


def split_cols(z):
    offs = np.cumsum(IN_SPLITS)[:-1].tolist()
    return jnp.split(z, offs, axis=-1)


def dwconv(x, w, b, pad_left):
    K = w.shape[0]
    y = lax.conv_general_dilated(x, w[:, None, :].astype(x.dtype), (1,), [(pad_left, K - 1 - pad_left)],
                                 dimension_numbers=('NWC', 'WIO', 'NWC'), feature_group_count=x.shape[-1])
    return y + b.astype(x.dtype)


def token_shift(x, mu):
    prev = jnp.pad(x, ((0, 0), (1, 0), (0, 0)))[:, :-1]
    nxt = jnp.pad(x, ((0, 0), (0, 1), (0, 0)))[:, 1:]
    return x + mu[0] * (prev - x) + mu[1] * (nxt - x)


def split_heads(t, h):
    Bn, L, W = t.shape
    return t.reshape(Bn, L, h, W // h)


def to_colmajor(a):
    Bn, L, C = a.shape
    rows = L // GRID_W
    return a.reshape(Bn, rows, GRID_W, C).transpose(0, 2, 1, 3).reshape(Bn, L, C)


def from_colmajor(a):
    Bn, L, C = a.shape
    rows = L // GRID_W
    return a.reshape(Bn, GRID_W, rows, C).transpose(0, 2, 1, 3).reshape(Bn, L, C)


LRU_TC = 256


def _expm1(z):
    u = jnp.exp(z)
    return jnp.where(u == 1.0, z, (u - 1.0) * z / jnp.where(u == 1.0, 1.0, jnp.log(u)))


def _lru_body(x_ref, h0_ref, wr_ref, wi_ref, br_ref, bi_ref, sp_ref, h_ref, hT_ref, st_ref, *, reverse):
    TC, W = x_ref.shape[1], x_ref.shape[2]

    @pl.when(pl.program_id(1) == 0)
    def _():
        st_ref[...] = h0_ref[0]

    x = x_ref[0]
    r = jax.nn.sigmoid(jnp.dot(x, wr_ref[...], preferred_element_type=F32) + br_ref[...])
    i = jax.nn.sigmoid(jnp.dot(x, wi_ref[...], preferred_element_type=F32) + bi_ref[...])
    log_a = -LRU_C * r * sp_ref[...]
    a = jnp.exp(log_a)
    b = jnp.sqrt(-_expm1(2.0 * log_a)) * (i * x)
    row = lax.broadcasted_iota(jnp.int32, (TC, W), 0)
    s = 1
    while s < TC:
        if reverse:
            keep = row < TC - s
            a_sh = jnp.where(keep, pltpu.roll(a, TC - s, 0), 1.0)
            b_sh = jnp.where(keep, pltpu.roll(b, TC - s, 0), 0.0)
        else:
            keep = row >= s
            a_sh = jnp.where(keep, pltpu.roll(a, s, 0), 1.0)
            b_sh = jnp.where(keep, pltpu.roll(b, s, 0), 0.0)
        b = b + a * b_sh
        a = a * a_sh
        s *= 2
    h = b + a * st_ref[...]
    h_ref[0] = h
    last = h[0:1, :] if reverse else h[TC - 1:TC, :]
    st_ref[...] = last
    hT_ref[0] = last


def lru_scan_dir(xc, h0, wr_bd, wi_bd, br, bi, sp, reverse):
    Bn, L, W = xc.shape
    TC = min(LRU_TC, L)
    n = L // TC
    tmap = (lambda b, c: (b, n - 1 - c, 0)) if reverse else (lambda b, c: (b, c, 0))
    wspec = pl.BlockSpec((W, W), lambda b, c: (0, 0))
    vspec = pl.BlockSpec((1, W), lambda b, c: (0, 0))
    sspec = pl.BlockSpec((1, 1, W), lambda b, c: (b, 0, 0))
    return pl.pallas_call(
        functools.partial(_lru_body, reverse=reverse),
        grid=(Bn, n),
        in_specs=[pl.BlockSpec((1, TC, W), tmap), sspec, wspec, wspec, vspec, vspec, vspec],
        out_specs=[pl.BlockSpec((1, TC, W), tmap), sspec],
        out_shape=[jax.ShapeDtypeStruct((Bn, L, W), F32), jax.ShapeDtypeStruct((Bn, 1, W), F32)],
        scratch_shapes=[pltpu.VMEM((1, W), F32)],
        compiler_params=pltpu.CompilerParams(dimension_semantics=("parallel", "arbitrary")),
        name="rglru_scan",
    )(xc, h0, wr_bd, wi_bd, br, bi, sp)


def head_block_diag(w):
    D2, H, N, _ = w.shape
    return jnp.einsum('dhij,hg->dhigj', w, jnp.eye(H, dtype=w.dtype)).reshape(D2, H * N, H * N)


def mixer_rglru(f_c, f_l, p, need_ctx):
    (x_c, g_c), (x_l, g_l) = f_c, f_l
    xc_c = dwconv(x_c, p['lru_conv_w'], p['lru_conv_b'], CONV_W // 2).astype(F32)
    xc_l = dwconv(x_l, p['lru_conv_w'], p['lru_conv_b'], CONV_W // 2).astype(F32)
    h0 = jnp.zeros((xc_l.shape[0], 1, W_LRU), F32)
    wr, wi = head_block_diag(p['lru_wr']), head_block_diag(p['lru_wi'])
    sp = jax.nn.softplus(-p['lru_lam'])
    hs_c, hs_l = [], []
    for d in range(2):
        gates = (wr[d], wi[d], p['lru_br'][d][None], p['lru_bi'][d][None], sp[d][None])
        hc, st = lru_scan_dir(xc_c, h0, *gates, reverse=(d == 1))
        hl, _ = lru_scan_dir(xc_l, st, *gates, reverse=(d == 1))
        hs_c.append(hc)
        hs_l.append(hl)
    y_l = jax.nn.gelu(g_l.astype(F32)) * (hs_l[0] + hs_l[1])
    y_c = jax.nn.gelu(g_c.astype(F32)) * (hs_c[0] + hs_c[1]) if need_ctx else None
    return y_c, y_l


def rwkv_shift(f, p):
    r, k, v, zw, za, zg = [t.astype(F32) for t in f]
    mu = p['rwkv_mu']
    return (token_shift(r, mu[0]), token_shift(k, mu[1]), token_shift(v, mu[2]), zw, za, zg)


def rwkv_dir_inputs(f, p, d):
    r, k, v, zw, za, _ = f
    w_log = -jax.nn.softplus(-(p['rwkv_w0'][d] + jnp.tanh(zw) @ p['rwkv_w2'][d])) - 0.5
    decay = jnp.exp(-jnp.exp(w_log))
    a = jax.nn.sigmoid(p['rwkv_a0'][d] + za @ p['rwkv_a2'][d])
    kk = split_heads(k * p['rwkv_kk'], H_RWKV)
    kk = kk / jnp.maximum(jnp.linalg.norm(kk, axis=-1, keepdims=True), 1e-12)
    kd = k * (1 + (a - 1) * p['rwkv_ka'])
    return [split_heads(r, H_RWKV), split_heads(decay, H_RWKV), split_heads(kd, H_RWKV),
            split_heads(v, H_RWKV), kk, kk * split_heads(a, H_RWKV)]


RWKV_TC = HEAD_DIM
RWKV_PAIRS = 32


def _split_bf16(x):
    hi_f = lax.bitcast_convert_type(lax.bitcast_convert_type(x, jnp.uint32) & jnp.uint32(0xFFFF0000), F32)
    return hi_f.astype(jnp.bfloat16), (x - hi_f).astype(jnp.bfloat16)


def _segsum_bcast(x, ones2):
    hi, lo = _split_bf16(x)
    return jnp.dot(jnp.concatenate([hi, lo], axis=-1), ones2, preferred_element_type=F32)


def _rwkv_body(kk_ref, v_ref, r_ref, w_ref, kka_ref, k_ref, y_ref, s_ref):
    NP, N = RWKV_PAIRS, HEAD_DIM
    rev = pl.program_id(0)

    @pl.when(pl.program_id(1) == 0)
    def _():
        s_ref[...] = jnp.zeros_like(s_ref)

    y_ref[...] = jnp.zeros_like(y_ref)
    lane = lax.broadcasted_iota(jnp.int32, (N, 2 * N), 1)
    row = lax.broadcasted_iota(jnp.int32, (N, 2 * N), 0)
    diag = (lane % N) == row
    seg = (lax.broadcasted_iota(jnp.int32, (4 * N, 2 * N), 0) % (2 * N)) // N == \
        lax.broadcasted_iota(jnp.int32, (4 * N, 2 * N), 1) // N
    ones2 = jnp.where(seg, 1.0, 0.0).astype(jnp.bfloat16)

    seg4 = lax.broadcasted_iota(jnp.int32, (4 * N, 4 * N), 0) // N == \
        lax.broadcasted_iota(jnp.int32, (4 * N, 4 * N), 1) // N
    ones4 = jnp.where(seg4, 1.0, 0.0).astype(jnp.bfloat16)
    H = NP // 2

    def segsum_bcast_1pass(x):
        xb = x.astype(jnp.bfloat16)
        lhs = jnp.concatenate([xb[:H].reshape(H * N, 2 * N), xb[H:].reshape(H * N, 2 * N)], axis=-1)
        out = jnp.dot(lhs, ones4, preferred_element_type=F32)
        return jnp.concatenate([out[:, :2 * N].reshape(H, N, 2 * N), out[:, 2 * N:].reshape(H, N, 2 * N)], axis=0)

    def step(i, carry):
        t = i + rev * (RWKV_TC - 1 - 2 * i)

        def rowv(ref):
            x = ref[:, pl.ds(t, 1), :] if len(ref.shape) == 3 else ref[0, :, pl.ds(t, 1), :]
            return jnp.concatenate([x[:, :, :2 * N], x[:, :, 2 * N:]], axis=0)

        S = s_ref[...]
        SA = _segsum_bcast((S * rowv(kk_ref)).reshape(NP * N, 2 * N), ones2).reshape(NP, N, 2 * N)
        VB = segsum_bcast_1pass(jnp.where(diag, rowv(v_ref), 0.0))
        S = S * rowv(w_ref) - SA * rowv(kka_ref) + VB * rowv(k_ref)
        s_ref[...] = S
        Y = segsum_bcast_1pass(S * rowv(r_ref))
        y_ref[0, :, 0] = jnp.where((lane % N) == t, Y, y_ref[0, :, 0])
        return carry

    lax.fori_loop(0, RWKV_TC, step, 0)


def rwkv_scan_bidir(kk, v, r, w, kka, k, n_ctx_blocks):
    Bn, L, C = kk.shape
    TC, N = RWKV_TC, HEAD_DIM
    n = L // TC
    assert RWKV_PAIRS == Bn * C // (2 * N)

    def blk(d, c):
        back = jnp.where(c < n_ctx_blocks, n_ctx_blocks - 1 - c, n + n_ctx_blocks - 1 - c)
        return jnp.where(d == 0, c, back)

    shared = pl.BlockSpec((Bn, TC, C), lambda d, c: (0, blk(d, c), 0))
    per_dir = pl.BlockSpec((1, Bn, TC, C), lambda d, c: (d, 0, blk(d, c), 0))
    return pl.pallas_call(
        _rwkv_body,
        grid=(2, n),
        in_specs=[shared, shared, shared, per_dir, per_dir, per_dir],
        out_specs=pl.BlockSpec((1, RWKV_PAIRS, 1, N, 2 * N), lambda d, c: (d, 0, blk(d, c), 0, 0)),
        out_shape=jax.ShapeDtypeStruct((2, RWKV_PAIRS, n, N, 2 * N), F32),
        scratch_shapes=[pltpu.VMEM((RWKV_PAIRS, N, 2 * N), F32)],
        compiler_params=pltpu.CompilerParams(dimension_semantics=("parallel", "arbitrary"),
                                             vmem_limit_bytes=40 * 1024 * 1024),
        name="rwkv7_scan",
    )(kk, v, r, w, kka, k)


def rwkv7_scan_bidir(ins_c, ins_l):
    Bn, Lc = ins_c[0][0].shape[:2]
    Ll = ins_l[0][0].shape[1]
    L = Lc + Ll
    cat = lambda d, j: jnp.concatenate([ins_c[d][j], ins_l[d][j]], axis=1).reshape(Bn, L, W_RWKV)
    per_dir = lambda j: jnp.stack([cat(0, j), cat(1, j)])
    yT = rwkv_scan_bidir(cat(0, 4), cat(0, 3), cat(0, 0), per_dir(1), per_dir(5), per_dir(2), Lc // RWKV_TC)
    halves = W_RWKV // (2 * HEAD_DIM)
    y = yT.reshape(2, halves, Bn, L // HEAD_DIM, HEAD_DIM, 2, HEAD_DIM)
    y = y.transpose(0, 2, 3, 6, 1, 5, 4).reshape(2, Bn, L, H_RWKV, HEAD_DIM)
    return ([y[d, :, :Lc] for d in range(2)], [y[d, :, Lc:] for d in range(2)])


def rwkv_bonus(ins, p):
    r, _, kd, v = ins[:4]
    return jnp.sum(r * kd * p['rwkv_rk'], axis=-1, keepdims=True) * v


def rwkv_out(y, bonus, zg, p):
    Bn, L, H, N = y.shape
    mu = jnp.mean(y, axis=-1, keepdims=True)
    var = jnp.mean(jnp.square(y - mu), axis=-1, keepdims=True)
    yn = ((y - mu) * lax.rsqrt(var + RWKV_GN_EPS)).reshape(Bn, L, H * N) * p['rwkv_ln_g'] + p['rwkv_ln_b']
    g = jax.nn.sigmoid(zg) @ p['rwkv_g2']
    return (yn + bonus.reshape(Bn, L, H * N)) * g


def mixer_rwkv(f_c, f_l, p, need_ctx):
    fc, fl = rwkv_shift(f_c, p), rwkv_shift(f_l, p)
    ins_c = [rwkv_dir_inputs(fc, p, d) for d in range(2)]
    ins_l = [rwkv_dir_inputs(fl, p, d) for d in range(2)]
    o_c, o_l = rwkv7_scan_bidir(ins_c, ins_l)
    y_c, y_l = [], []
    for d in range(2):
        y_l.append((o_l[d], rwkv_bonus(ins_l[d], p)))
        if need_ctx:
            y_c.append((o_c[d], rwkv_bonus(ins_c[d], p)))
    out_l = rwkv_out(y_l[0][0] + y_l[1][0], y_l[0][1] + y_l[1][1], fl[5], p)
    out_c = rwkv_out(y_c[0][0] + y_c[1][0], y_c[0][1] + y_c[1][1], fc[5], p) if need_ctx else None
    return out_c, out_l


def mlstm_prep(f, p):
    q, k, v, o, gz = f
    Bn, L, _ = q.shape
    qk = jax.nn.silu(dwconv(jnp.concatenate([q, k], axis=-1), p['mlstm_conv_w'], p['mlstm_conv_b'], CONV_W // 2)).astype(F32)
    q, k = jnp.split(qk, 2, axis=-1)
    heads = lambda t: t.reshape(Bn, L, H_MLSTM, HEAD_DIM).transpose(0, 2, 1, 3)
    gates = (gz.astype(F32).reshape(Bn, L, 2, 2, H_MLSTM) + p['mlstm_gate_b']).transpose(2, 3, 0, 4, 1)
    return heads(q) * HEAD_DIM ** -0.5, heads(k), heads(v.astype(F32)), o.astype(F32), gates


def mlstm_chunkwise(q, k, v, ig, lf, state):
    Bn, H, L, N = q.shape
    T = MLSTM_CHUNK
    nc = L // T
    ch = lambda a: jnp.moveaxis(a.reshape((Bn, H, nc, T) + a.shape[3:]), 2, 0)
    mask = jnp.tril(jnp.ones((T, T), dtype=bool))

    def step(carry, inp):
        C, n, m = carry
        qc, kc, vc, ic, fc = inp
        b = jnp.cumsum(fc, axis=-1)
        logd = jnp.where(mask, b[..., :, None] - b[..., None, :] + ic[..., None, :], -jnp.inf)
        inter = b + m[..., None]
        mt = jnp.maximum(inter, jnp.max(logd, axis=-1))
        s = jnp.einsum('bhtn,bhsn->bhts', qc, kc) * jnp.exp(logd - mt[..., None])
        e_inter = jnp.exp(inter - mt)
        num = jnp.einsum('bhts,bhsn->bhtn', s, vc) + e_inter[..., None] * jnp.einsum('bhvk,bhtk->bhtv', C, qc)
        den = jnp.sum(s, axis=-1) + e_inter * jnp.einsum('bhk,bhtk->bht', n, qc)
        h = num / jnp.maximum(jnp.abs(den), jnp.exp(-mt))[..., None]
        bT = b[..., -1]
        logw = bT[..., None] - b + ic
        m_new = jnp.maximum(bT + m, jnp.max(logw, axis=-1))
        wgt = jnp.exp(logw - m_new[..., None])
        dec = jnp.exp(bT + m - m_new)
        C = dec[..., None, None] * C + jnp.einsum('bhs,bhsv,bhsk->bhvk', wgt, vc, kc)
        n = dec[..., None] * n + jnp.einsum('bhs,bhsk->bhk', wgt, kc)
        return (C, n, m_new), h

    state, hs = lax.scan(step, state, (ch(q), ch(k), ch(v), ch(ig), ch(lf)))
    return jnp.moveaxis(hs, 0, 2).reshape(Bn, H, L, N), state


def mixer_mlstm(f_c, f_l, p, need_ctx):
    f_l = [to_colmajor(t) for t in f_l]
    qc, kc, vc, oc, gc = mlstm_prep(f_c, p)
    ql, kl, vl, ol, gl = mlstm_prep(f_l, p)
    Bn = ql.shape[0]
    st0 = (jnp.zeros((Bn, H_MLSTM, HEAD_DIM, HEAD_DIM), F32), jnp.zeros((Bn, H_MLSTM, HEAD_DIM), F32),
           jnp.zeros((Bn, H_MLSTM), F32))
    h_c, h_l = [], []
    for d in range(2):
        hc, st = mlstm_chunkwise(flip_if(qc, d, 2), flip_if(kc, d, 2), flip_if(vc, d, 2), flip_if(gc[d, 0], d, 2),
                                 jax.nn.log_sigmoid(flip_if(gc[d, 1], d, 2)), st0)
        hl, _ = mlstm_chunkwise(flip_if(ql, d, 2), flip_if(kl, d, 2), flip_if(vl, d, 2), flip_if(gl[d, 0], d, 2),
                                jax.nn.log_sigmoid(flip_if(gl[d, 1], d, 2)), st)
        h_c.append(flip_if(hc, d, 2))
        h_l.append(flip_if(hl, d, 2))
    merge_heads = lambda h: h.transpose(0, 2, 1, 3).reshape(h.shape[0], h.shape[2], W_MLSTM)
    y_l = from_colmajor(jax.nn.sigmoid(ol) * merge_heads(h_l[0] + h_l[1]))
    y_c = jax.nn.sigmoid(oc) * merge_heads(h_c[0] + h_c[1]) if need_ctx else None
    return y_c, y_l


def hyena_spectrum(L, p):
    pos = jnp.arange(L, dtype=F32)
    t = pos / (L - 1)
    bands = (HYENA_EMB - 1) // 2
    freqs = jnp.linspace(1e-4, bands - 1, bands, dtype=F32)
    ang = (2 * math.pi / L) * pos[:, None] * freqs[None, :]
    z = jnp.concatenate([t[:, None], jnp.cos(ang), -jnp.sin(ang)], axis=-1)
    h = jnp.sin(p['hy_freq'][0] * (z @ p['hy_w1'] + p['hy_b1']))
    h = jnp.sin(p['hy_freq'][1] * (h @ p['hy_w2'] + p['hy_b2']))
    h = (h @ p['hy_w3']).astype(F32).reshape(L, HYENA_ORDER, 2, W_HYENA)
    deltas = jnp.abs(jnp.linspace(math.log(HYENA_TARGET) / HYENA_SLOW, math.log(HYENA_TARGET) / HYENA_FAST,
                                  W_HYENA, dtype=F32))
    h = h * jnp.exp(-t[:, None, None, None] * deltas)
    fwd, bwd = h[:, :, 0], h[:, :, 1]
    two = jnp.concatenate([fwd, jnp.zeros_like(fwd[:1]), jnp.flip(bwd[1:], axis=0)], axis=0)
    two = two / (jnp.sum(jnp.abs(two), axis=0, keepdims=True) + EPS)
    return jnp.fft.rfft(two, axis=0)


def long_conv(u, spec, bias):
    L = u.shape[1]
    y = jnp.fft.irfft(jnp.fft.rfft(u, n=2 * L, axis=1) * spec, n=2 * L, axis=1)[:, :L]
    return y + u * bias


def mixer_hyena(f, p):
    u = dwconv(jnp.concatenate(f, axis=-1), p['hy_conv_w'], p['hy_conv_b'], HYENA_SHORT // 2).astype(F32)
    v, x1, x2 = jnp.split(u, 3, axis=-1)
    spec = hyena_spectrum(u.shape[1], p)
    z = x1 * long_conv(v, spec[:, 0], p['hy_bias'][0])
    return x2 * long_conv(z, spec[:, 1], p['hy_bias'][1])


PEER_SEL = PEER_HEADS * PEER_TOPK
PEER_HALF = D_MODEL // 2
PEER_CHUNK = 4096
PEER_TT = 16
SC_GATHER_ROWS = 64


def pack_bf16_halves(tab):
    bits = lax.bitcast_convert_type(tab.astype(jnp.bfloat16), jnp.uint16).astype(jnp.uint32)
    half = tab.shape[1] // 2
    return bits[:, :half] | (bits[:, half:] << 16)


def sc_gather_pair(u_tab, v_tab, idx):
    n = idx.shape[0]
    W = u_tab.shape[1]
    info = plsc.get_sparse_core_info()
    nc, ns = info.num_cores, info.num_subcores
    per_w = n // (nc * ns)
    G = SC_GATHER_ROWS
    assert per_w * nc * ns == n and per_w % G == 0
    nsteps = per_w // G
    mesh = plsc.VectorSubcoreMesh(core_axis_name="c", subcore_axis_name="s")
    out = jax.ShapeDtypeStruct((n, W), u_tab.dtype)

    def body(u_hbm, v_hbm, idx_hbm, uo_hbm, vo_hbm, idx_v, urows, vrows, usem, vsem):
        wid = lax.axis_index("s") * nc + lax.axis_index("c")
        base = wid * per_w
        pltpu.sync_copy(idx_hbm.at[pl.ds(base, per_w)], idx_v)

        @pl.loop(0, nsteps)
        def _(i):
            off = pl.multiple_of(i * G, G)
            ids = idx_v.at[pl.ds(off, G)]
            cu = pltpu.async_copy(u_hbm.at[ids], urows, usem)
            cv = pltpu.async_copy(v_hbm.at[ids], vrows, vsem)
            cu.wait()
            pltpu.sync_copy(urows, uo_hbm.at[pl.ds(base + off, G)])
            cv.wait()
            pltpu.sync_copy(vrows, vo_hbm.at[pl.ds(base + off, G)])

    fn = pl.kernel(body, out_type=(out, out), mesh=mesh,
                   scratch_types=[pltpu.VMEM((per_w,), jnp.int32),
                                  pltpu.VMEM((G, W), u_tab.dtype), pltpu.VMEM((G, W), u_tab.dtype),
                                  pltpu.SemaphoreType.DMA, pltpu.SemaphoreType.DMA],
                   name="peer_sc_gather")
    return fn(u_tab, v_tab, idx)


def _unpack_halves(w):
    lo = lax.bitcast_convert_type(w << 16, F32)
    hi = lax.bitcast_convert_type(w & jnp.uint32(0xFFFF0000), F32)
    return lo, hi


def _peer_apply_body(x_ref, g_ref, ug_ref, vg_ref, o_ref):
    TT = x_ref.shape[0]
    gpad = jnp.concatenate([g_ref[...], jnp.zeros((PEER_SEL - TT, PEER_SEL), F32)], axis=0)
    gT = gpad.T
    for t in range(TT):
        rows = pl.ds(t * PEER_SEL, PEER_SEL)
        ulo, uhi = _unpack_halves(ug_ref[rows, :])
        xl = x_ref[pl.ds(t, 1), 0:PEER_HALF]
        xh = x_ref[pl.ds(t, 1), PEER_HALF:D_MODEL]
        dots = jnp.sum(ulo * xl + uhi * xh, axis=-1, keepdims=True)
        w = jax.nn.gelu(dots) * gT[:, t:t + 1]
        vlo, vhi = _unpack_halves(vg_ref[rows, :])
        o_ref[pl.ds(t, 1), 0:PEER_HALF] = jnp.sum(vlo * w, axis=0, keepdims=True)
        o_ref[pl.ds(t, 1), PEER_HALF:D_MODEL] = jnp.sum(vhi * w, axis=0, keepdims=True)


def peer_apply(x, gate, ug, vg):
    T = x.shape[0]
    TT = PEER_TT
    return pl.pallas_call(
        _peer_apply_body,
        grid=(T // TT,),
        in_specs=[pl.BlockSpec((TT, D_MODEL), lambda i: (i, 0)),
                  pl.BlockSpec((TT, PEER_SEL), lambda i: (i, 0)),
                  pl.BlockSpec((TT * PEER_SEL, PEER_HALF), lambda i: (i, 0)),
                  pl.BlockSpec((TT * PEER_SEL, PEER_HALF), lambda i: (i, 0))],
        out_specs=pl.BlockSpec((TT, D_MODEL), lambda i: (i, 0)),
        out_shape=jax.ShapeDtypeStruct((T, D_MODEL), F32),
        compiler_params=pltpu.CompilerParams(dimension_semantics=("parallel",),
                                             vmem_limit_bytes=40 * 1024 * 1024),
        name="peer_apply",
    )(x, gate, ug, vg)


ROUTE_TT = 256
ROUTE_PAIR_ROWS = sum(-(-(PEER_TOPK // (i + 1)) // 8) * 8 for i in range(PEER_TOPK // 2)) + PEER_TOPK // 2


def _peer_route_body(s_ref, e_ref, g_ref, xs_ref, sv_ref, si_ref, cand_ref, cidx_ref, pf_ref, ts_ref):
    K, NK, TT = PEER_TOPK, PEER_NKEYS, s_ref.shape[-1]
    NEG = -jnp.inf
    xs_ref[...] = s_ref[0]
    kiota = lax.broadcasted_iota(jnp.int32, (NK, TT), 0).astype(F32)

    def half_topk(it, carry):
        for c in range(2):
            x = xs_ref[c]
            m = jnp.max(x, axis=0, keepdims=True)
            idx = jnp.min(jnp.where(x == m, kiota, float(NK)), axis=0, keepdims=True)
            xs_ref[c] = jnp.where(kiota == idx, NEG, x)
            sv_ref[c, pl.ds(it, 1), :] = m
            si_ref[c, pl.ds(it, 1), :] = idx
        return carry

    lax.fori_loop(0, K, half_topk, 0)

    jiota = lambda rows: lax.broadcasted_iota(jnp.int32, (rows, TT), 0).astype(F32)
    off = 0
    for i in range(K // 2):
        n = K // (i + 1)
        rows = -(-n // 8) * 8
        ok = jiota(rows) < float(n)
        cand_ref[pl.ds(off, rows), :] = jnp.where(ok, sv_ref[0, i:i + 1, :] + sv_ref[1, 0:rows, :], NEG)
        cidx_ref[pl.ds(off, rows), :] = si_ref[0, i:i + 1, :] * float(NK) + si_ref[1, 0:rows, :]
        pf_ref[pl.ds(off, rows), :] = jnp.where(ok, float(i * K) + jiota(rows), float(K * K))
        off += rows
    cand_ref[pl.ds(off, K // 2), :] = sv_ref[0, K // 2:K, :] + sv_ref[1, 0:1, :]
    cidx_ref[pl.ds(off, K // 2), :] = si_ref[0, K // 2:K, :] * float(NK) + si_ref[1, 0:1, :]
    pf_ref[pl.ds(off, K // 2), :] = (float(K // 2) + jiota(K // 2)) * float(K)
    piota = pf_ref[...]

    def pair_topk(it, carry):
        x = cand_ref[...]
        m = jnp.max(x, axis=0, keepdims=True)
        pos = jnp.min(jnp.where(x == m, piota, float(K * K)), axis=0, keepdims=True)
        sel = piota == pos
        cand_ref[...] = jnp.where(sel, NEG, x)
        ts_ref[pl.ds(it, 1), :] = m
        e_ref[0, pl.ds(it, 1), :] = jnp.max(jnp.where(sel, cidx_ref[...], -1.0), axis=0,
                                            keepdims=True).astype(jnp.int32)
        return carry

    lax.fori_loop(0, K, pair_topk, 0)
    ts = ts_ref[...]
    ex = jnp.exp(ts - ts[0:1, :])
    g_ref[0] = ex / jnp.sum(ex, axis=0, keepdims=True)


def peer_route_topk(sT):
    H, _, NK, N = sT.shape
    TT, K = ROUTE_TT, PEER_TOPK
    out_spec = pl.BlockSpec((1, K, TT), lambda h, i: (h, 0, i))
    return pl.pallas_call(
        _peer_route_body,
        grid=(H, N // TT),
        in_specs=[pl.BlockSpec((1, 2, NK, TT), lambda h, i: (h, 0, 0, i))],
        out_specs=[out_spec, out_spec],
        out_shape=[jax.ShapeDtypeStruct((H, K, N), jnp.int32), jax.ShapeDtypeStruct((H, K, N), F32)],
        scratch_shapes=[pltpu.VMEM((2, NK, TT), F32), pltpu.VMEM((2, K, TT), F32), pltpu.VMEM((2, K, TT), F32),
                        pltpu.VMEM((ROUTE_PAIR_ROWS, TT), F32), pltpu.VMEM((ROUTE_PAIR_ROWS, TT), F32),
                        pltpu.VMEM((ROUTE_PAIR_ROWS, TT), F32), pltpu.VMEM((K, TT), F32)],
        compiler_params=pltpu.CompilerParams(dimension_semantics=("parallel", "parallel")),
        name="peer_route_topk",
    )(sT)


def peer_route(xf, p):
    N = xf.shape[0]
    wq, keys = p['peer_wq'], p['peer_keys'].astype(F32)
    q = (xf @ wq).astype(F32).reshape(N, PEER_HEADS, 2, PEER_DQ // 2)
    sT = jnp.einsum('thcq,hckq->hckt', q, keys)
    eT, gT = peer_route_topk(sT)
    return (eT.transpose(2, 0, 1).reshape(N, PEER_SEL), gT.transpose(2, 0, 1).reshape(N, PEER_SEL))


def peer_ffn(h, p):
    Bn, L, D = h.shape
    N = Bn * L
    xf = h.reshape(N, D)
    eidx, gate = peer_route(xf, p)
    u_pk, v_pk = p['peer_u_pk'], p['peer_v_pk']
    outs = []
    for c0 in range(0, N, PEER_CHUNK):
        rows = slice(c0, c0 + PEER_CHUNK)
        ug, vg = sc_gather_pair(u_pk, v_pk, eidx[rows].reshape(-1))
        outs.append(peer_apply(xf[rows], gate[rows], ug, vg))
    return jnp.concatenate(outs, axis=0).reshape(Bn, L, D)


def merge_groups(ys, g, dtype):
    outs, off = [], 0
    for y, w in zip(ys, GROUP_WIDTHS):
        outs.append(rmsnorm(y, g[off:off + w]).astype(dtype))
        off += w
    return jnp.concatenate(outs, axis=-1)


def token_mixers(h_c, h_l, p, need_ctx):
    zc = split_cols(h_c @ p['w_in'])
    zl = split_cols(h_l @ p['w_in'])
    a_c, a_l = mixer_rglru(zc[0:2], zl[0:2], p, need_ctx)
    b_c, b_l = mixer_rwkv(zc[2:8], zl[2:8], p, need_ctx)
    m_c, m_l = mixer_mlstm(zc[8:13], zl[8:13], p, need_ctx)
    d_l = mixer_hyena(zl[13:16], p)
    o_l = merge_groups([a_l, b_l, m_l, d_l], p['grp_g'], h_l.dtype) @ p['w_out']
    if not need_ctx:
        return None, o_l
    d_c = mixer_hyena(zc[13:16], p)
    o_c = merge_groups([a_c, b_c, m_c, d_c], p['grp_g'], h_c.dtype) @ p['w_out']
    return o_c, o_l


def trunk_layer(x_l, x_c, c, c_ctx, p, need_ctx):
    Bn = c.shape[0]
    mod_l = (jax.nn.silu(c) @ p['ada_w'] + p['ada_b']).reshape(Bn, 6, 1, D_MODEL)
    mod_c = (jax.nn.silu(c_ctx) @ p['ada_w'] + p['ada_b']).reshape(6, 1, 1, D_MODEL)
    h_l = modulate(rmsnorm(x_l, p['norm1_g']), mod_l[:, 0], mod_l[:, 1])
    h_c = modulate(rmsnorm(x_c, p['norm1_g']), mod_c[0], mod_c[1])
    o_c, o_l = token_mixers(h_c, h_l, p, need_ctx)
    x_l = x_l + mod_l[:, 2] * o_l
    x_l = x_l + mod_l[:, 5] * peer_ffn(modulate(rmsnorm(x_l, p['norm2_g']), mod_l[:, 3], mod_l[:, 4]), p)
    if need_ctx:
        x_c = x_c + mod_c[2] * o_c
        x_c = x_c + mod_c[5] * peer_ffn(modulate(rmsnorm(x_c, p['norm2_g']), mod_c[3], mod_c[4]), p)
    return x_l, x_c


def _final_norm_body(x_ref, g_ref, o_ref):
    xf = x_ref[...]
    y = xf * lax.rsqrt(jnp.mean(xf * xf, axis=-1, keepdims=True) + EPS)
    o_ref[...] = y * g_ref[...]


def final_rmsnorm(x, g):
    Bn, L, D = x.shape
    rows = Bn * L
    tile = 1024
    out = pl.pallas_call(
        _final_norm_body,
        grid=(rows // tile,),
        in_specs=[pl.BlockSpec((tile, D), lambda i: (i, 0)), pl.BlockSpec((1, D), lambda i: (0, 0))],
        out_specs=pl.BlockSpec((tile, D), lambda i: (i, 0)),
        out_shape=jax.ShapeDtypeStruct((rows, D), x.dtype),
        name="final_rmsnorm",
    )(x.reshape(rows, D), g.reshape(1, D))
    return out.reshape(Bn, L, D)


def kernel(x, c, ctx, c_ctx, ada_w, ada_b, norm1_g, norm2_g, w_in, w_out, grp_g,
           lru_conv_w, lru_conv_b, lru_wr, lru_br, lru_wi, lru_bi, lru_lam,
           rwkv_mu, rwkv_w0, rwkv_w2, rwkv_a0, rwkv_a2, rwkv_g2, rwkv_kk, rwkv_ka, rwkv_rk, rwkv_ln_g, rwkv_ln_b,
           mlstm_conv_w, mlstm_conv_b, mlstm_gate_b,
           hy_conv_w, hy_conv_b, hy_w1, hy_b1, hy_w2, hy_b2, hy_w3, hy_freq, hy_bias,
           peer_wq, peer_keys, peer_u, peer_v, final_g):
    x_l, x_c = x, ctx
    for i in range(DEPTH):
        p = dict(ada_w=ada_w[i], ada_b=ada_b[i], norm1_g=norm1_g[i], norm2_g=norm2_g[i], w_in=w_in[i],
                 w_out=w_out[i], grp_g=grp_g[i],
                 lru_conv_w=lru_conv_w[i], lru_conv_b=lru_conv_b[i], lru_wr=lru_wr[i], lru_br=lru_br[i],
                 lru_wi=lru_wi[i], lru_bi=lru_bi[i], lru_lam=lru_lam[i],
                 rwkv_mu=rwkv_mu[i], rwkv_w0=rwkv_w0[i], rwkv_w2=rwkv_w2[i], rwkv_a0=rwkv_a0[i],
                 rwkv_a2=rwkv_a2[i], rwkv_g2=rwkv_g2[i], rwkv_kk=rwkv_kk[i], rwkv_ka=rwkv_ka[i],
                 rwkv_rk=rwkv_rk[i], rwkv_ln_g=rwkv_ln_g[i], rwkv_ln_b=rwkv_ln_b[i],
                 mlstm_conv_w=mlstm_conv_w[i], mlstm_conv_b=mlstm_conv_b[i], mlstm_gate_b=mlstm_gate_b[i],
                 hy_conv_w=hy_conv_w[i], hy_conv_b=hy_conv_b[i], hy_w1=hy_w1[i], hy_b1=hy_b1[i],
                 hy_w2=hy_w2[i], hy_b2=hy_b2[i], hy_w3=hy_w3[i], hy_freq=hy_freq[i], hy_bias=hy_bias[i],
                 peer_wq=peer_wq[i], peer_keys=peer_keys[i],
                 peer_u_pk=pack_bf16_halves(peer_u[i]), peer_v_pk=pack_bf16_halves(peer_v[i]))
        x_l, x_c = trunk_layer(x_l, x_c, c, c_ctx, p, i < DEPTH - 1)
    return final_rmsnorm(x_l, final_g)
```

```python
import functools
import math
import jax, jax.numpy as jnp
from jax import lax
import numpy as np
from jax.experimental import pallas as pl
from jax.experimental.pallas import tpu as pltpu
from jax.experimental.pallas import tpu_sc as plsc

D_MODEL = 1024
BATCH = 16
SEQ = 2048
DEPTH = 2

GRID_W = 64
CTX_LEN = 256
F32 = jnp.float32
EPS = 1e-6
HEAD_DIM = 64
D_MIX = D_MODEL
W_LRU = D_MIX // 4
W_RWKV = D_MIX // 4
W_MLSTM = D_MIX // 4
W_HYENA = D_MIX - W_LRU - W_RWKV - W_MLSTM
GROUP_WIDTHS = (W_LRU, W_RWKV, W_MLSTM, W_HYENA)
H_LRU = W_LRU // HEAD_DIM
H_RWKV = W_RWKV // HEAD_DIM
H_MLSTM = W_MLSTM // HEAD_DIM
CONV_W = 4
LRU_C = 8.0
RWKV_LORA_W = 32
RWKV_LORA_A = 32
RWKV_LORA_G = 64
RWKV_GN_EPS = 64e-5
MLSTM_CHUNK = 64
HYENA_ORDER = 2
HYENA_SHORT = 3
HYENA_EMB = 33
HYENA_HID = 64
HYENA_TARGET = 1e-2
HYENA_FAST = 0.3
HYENA_SLOW = 1.5
PEER_HEADS = 8
PEER_NKEYS = 128
PEER_EXPERTS = PEER_NKEYS * PEER_NKEYS
PEER_TOPK = 16
PEER_DQ = 256
PEER_BLOCK = 128
IN_SPLITS = (W_LRU, W_LRU, W_RWKV, W_RWKV, W_RWKV, RWKV_LORA_W, RWKV_LORA_A, RWKV_LORA_G, W_MLSTM, W_MLSTM, W_MLSTM, W_MLSTM, 4 * H_MLSTM, W_HYENA, W_HYENA, W_HYENA)
D_IN = sum(IN_SPLITS)


def rmsnorm(x, g):
    xf = x.astype(F32)
    y = xf * lax.rsqrt(jnp.mean(xf * xf, axis=-1, keepdims=True) + EPS)
    return (y * g.astype(F32)).astype(x.dtype)


def modulate(h, shift, scale):
    return h * (1 + scale) + shift


def flip_if(a, d, axis=1):
    return jnp.flip(a, axis) if d == 1 else a


def split_cols(z):
    offs = np.cumsum(IN_SPLITS)[:-1].tolist()
    return jnp.split(z, offs, axis=-1)


def dwconv(x, w, b, pad_left):
    K = w.shape[0]
    y = lax.conv_general_dilated(x, w[:, None, :].astype(x.dtype), (1,), [(pad_left, K - 1 - pad_left)],
                                 dimension_numbers=('NWC', 'WIO', 'NWC'), feature_group_count=x.shape[-1])
    return y + b.astype(x.dtype)


def token_shift(x, mu):
    prev = jnp.pad(x, ((0, 0), (1, 0), (0, 0)))[:, :-1]
    nxt = jnp.pad(x, ((0, 0), (0, 1), (0, 0)))[:, 1:]
    return x + mu[0] * (prev - x) + mu[1] * (nxt - x)


def split_heads(t, h):
    Bn, L, W = t.shape
    return t.reshape(Bn, L, h, W // h)


def to_colmajor(a):
    Bn, L, C = a.shape
    rows = L // GRID_W
    return a.reshape(Bn, rows, GRID_W, C).transpose(0, 2, 1, 3).reshape(Bn, L, C)


def from_colmajor(a):
    Bn, L, C = a.shape
    rows = L // GRID_W
    return a.reshape(Bn, GRID_W, rows, C).transpose(0, 2, 1, 3).reshape(Bn, L, C)


LRU_TC = 256


def _expm1(z):
    u = jnp.exp(z)
    return jnp.where(u == 1.0, z, (u - 1.0) * z / jnp.where(u == 1.0, 1.0, jnp.log(u)))


def _lru_body(x_ref, h0_ref, wr_ref, wi_ref, br_ref, bi_ref, sp_ref, h_ref, hT_ref, st_ref, *, reverse):
    TC, W = x_ref.shape[1], x_ref.shape[2]

    @pl.when(pl.program_id(1) == 0)
    def _():
        st_ref[...] = h0_ref[0]

    x = x_ref[0]
    r = jax.nn.sigmoid(jnp.dot(x, wr_ref[...], preferred_element_type=F32) + br_ref[...])
    i = jax.nn.sigmoid(jnp.dot(x, wi_ref[...], preferred_element_type=F32) + bi_ref[...])
    log_a = -LRU_C * r * sp_ref[...]
    a = jnp.exp(log_a)
    b = jnp.sqrt(-_expm1(2.0 * log_a)) * (i * x)
    row = lax.broadcasted_iota(jnp.int32, (TC, W), 0)
    s = 1
    while s < TC:
        if reverse:
            keep = row < TC - s
            a_sh = jnp.where(keep, pltpu.roll(a, TC - s, 0), 1.0)
            b_sh = jnp.where(keep, pltpu.roll(b, TC - s, 0), 0.0)
        else:
            keep = row >= s
            a_sh = jnp.where(keep, pltpu.roll(a, s, 0), 1.0)
            b_sh = jnp.where(keep, pltpu.roll(b, s, 0), 0.0)
        b = b + a * b_sh
        a = a * a_sh
        s *= 2
    h = b + a * st_ref[...]
    h_ref[0] = h
    last = h[0:1, :] if reverse else h[TC - 1:TC, :]
    st_ref[...] = last
    hT_ref[0] = last


def lru_scan_dir(xc, h0, wr_bd, wi_bd, br, bi, sp, reverse):
    Bn, L, W = xc.shape
    TC = min(LRU_TC, L)
    n = L // TC
    tmap = (lambda b, c: (b, n - 1 - c, 0)) if reverse else (lambda b, c: (b, c, 0))
    wspec = pl.BlockSpec((W, W), lambda b, c: (0, 0))
    vspec = pl.BlockSpec((1, W), lambda b, c: (0, 0))
    sspec = pl.BlockSpec((1, 1, W), lambda b, c: (b, 0, 0))
    return pl.pallas_call(
        functools.partial(_lru_body, reverse=reverse),
        grid=(Bn, n),
        in_specs=[pl.BlockSpec((1, TC, W), tmap), sspec, wspec, wspec, vspec, vspec, vspec],
        out_specs=[pl.BlockSpec((1, TC, W), tmap), sspec],
        out_shape=[jax.ShapeDtypeStruct((Bn, L, W), F32), jax.ShapeDtypeStruct((Bn, 1, W), F32)],
        scratch_shapes=[pltpu.VMEM((1, W), F32)],
        compiler_params=pltpu.CompilerParams(dimension_semantics=("parallel", "arbitrary")),
        name="rglru_scan",
    )(xc, h0, wr_bd, wi_bd, br, bi, sp)


def head_block_diag(w):
    D2, H, N, _ = w.shape
    return jnp.einsum('dhij,hg->dhigj', w, jnp.eye(H, dtype=w.dtype)).reshape(D2, H * N, H * N)


def mixer_rglru(f_c, f_l, p, need_ctx):
    (x_c, g_c), (x_l, g_l) = f_c, f_l
    xc_c = dwconv(x_c, p['lru_conv_w'], p['lru_conv_b'], CONV_W // 2).astype(F32)
    xc_l = dwconv(x_l, p['lru_conv_w'], p['lru_conv_b'], CONV_W // 2).astype(F32)
    h0 = jnp.zeros((xc_l.shape[0], 1, W_LRU), F32)
    wr, wi = head_block_diag(p['lru_wr']), head_block_diag(p['lru_wi'])
    sp = jax.nn.softplus(-p['lru_lam'])
    hs_c, hs_l = [], []
    for d in range(2):
        gates = (wr[d], wi[d], p['lru_br'][d][None], p['lru_bi'][d][None], sp[d][None])
        hc, st = lru_scan_dir(xc_c, h0, *gates, reverse=(d == 1))
        hl, _ = lru_scan_dir(xc_l, st, *gates, reverse=(d == 1))
        hs_c.append(hc)
        hs_l.append(hl)
    y_l = jax.nn.gelu(g_l.astype(F32)) * (hs_l[0] + hs_l[1])
    y_c = jax.nn.gelu(g_c.astype(F32)) * (hs_c[0] + hs_c[1]) if need_ctx else None
    return y_c, y_l


def rwkv_shift(f, p):
    r, k, v, zw, za, zg = [t.astype(F32) for t in f]
    mu = p['rwkv_mu']
    return (token_shift(r, mu[0]), token_shift(k, mu[1]), token_shift(v, mu[2]), zw, za, zg)


def rwkv_dir_inputs(f, p, d):
    r, k, v, zw, za, _ = f
    w_log = -jax.nn.softplus(-(p['rwkv_w0'][d] + jnp.tanh(zw) @ p['rwkv_w2'][d])) - 0.5
    decay = jnp.exp(-jnp.exp(w_log))
    a = jax.nn.sigmoid(p['rwkv_a0'][d] + za @ p['rwkv_a2'][d])
    kk = split_heads(k * p['rwkv_kk'], H_RWKV)
    kk = kk / jnp.maximum(jnp.linalg.norm(kk, axis=-1, keepdims=True), 1e-12)
    kd = k * (1 + (a - 1) * p['rwkv_ka'])
    return [split_heads(r, H_RWKV), split_heads(decay, H_RWKV), split_heads(kd, H_RWKV),
            split_heads(v, H_RWKV), kk, kk * split_heads(a, H_RWKV)]


RWKV_TC = HEAD_DIM
RWKV_PAIRS = 32


def _rwkv_body(kk_ref, v_ref, r_ref, w_ref, kka_ref, k_ref, y_ref, s_ref):
    NP, N = RWKV_PAIRS, HEAD_DIM
    rev = pl.program_id(0)

    @pl.when(pl.program_id(1) == 0)
    def _():
        s_ref[...] = jnp.zeros_like(s_ref)

    y_ref[...] = jnp.zeros_like(y_ref)
    lane = lax.broadcasted_iota(jnp.int32, (N, 2 * N), 1)
    row = lax.broadcasted_iota(jnp.int32, (N, 2 * N), 0)
    diag = (lane % N) == row
    seg4 = lax.broadcasted_iota(jnp.int32, (4 * N, 4 * N), 0) // N == \
        lax.broadcasted_iota(jnp.int32, (4 * N, 4 * N), 1) // N
    ones4 = jnp.where(seg4, 1.0, 0.0).astype(jnp.bfloat16)

    def segsum_bcast(x):
        H = x.shape[0] // 2
        xb = x.astype(jnp.bfloat16)
        lhs = jnp.concatenate([xb[:H].reshape(H * N, 2 * N), xb[H:].reshape(H * N, 2 * N)], axis=-1)
        out = jnp.dot(lhs, ones4, preferred_element_type=F32)
        return jnp.concatenate([out[:, :2 * N].reshape(H, N, 2 * N), out[:, 2 * N:].reshape(H, N, 2 * N)], axis=0)

    def rowv(ref, t):
        x = ref[:, pl.ds(t, 1), :] if len(ref.shape) == 3 else ref[0, :, pl.ds(t, 1), :]
        return jnp.concatenate([x[:, :, :2 * N], x[:, :, 2 * N:]], axis=0)

    def step(i, carry):
        t = i + rev * (RWKV_TC - 1 - 2 * i)
        tp = jnp.clip(t - 1 + 2 * rev, 0, RWKV_TC - 1)
        S = s_ref[...]
        R = segsum_bcast(jnp.concatenate([S * rowv(kk_ref, t), jnp.where(diag, rowv(v_ref, t), 0.0),
                                          S * rowv(r_ref, tp)], axis=0))
        y_ref[0, :, 0] = jnp.where(((lane % N) == tp) & (i > 0), R[2 * NP:], y_ref[0, :, 0])
        s_ref[...] = S * rowv(w_ref, t) - R[:NP] * rowv(kka_ref, t) + R[NP:2 * NP] * rowv(k_ref, t)
        return carry

    lax.fori_loop(0, RWKV_TC, step, 0)
    t_last = (RWKV_TC - 1) * (1 - rev)
    Y = segsum_bcast(s_ref[...] * rowv(r_ref, t_last))
    y_ref[0, :, 0] = jnp.where((lane % N) == t_last, Y, y_ref[0, :, 0])


def rwkv_scan_bidir(kk, v, r, w, kka, k, n_ctx_blocks):
    Bn, L, C = kk.shape
    TC, N = RWKV_TC, HEAD_DIM
    n = L // TC
    assert RWKV_PAIRS == Bn * C // (2 * N)

    def blk(d, c):
        back = jnp.where(c < n_ctx_blocks, n_ctx_blocks - 1 - c, n + n_ctx_blocks - 1 - c)
        return jnp.where(d == 0, c, back)

    shared = pl.BlockSpec((Bn, TC, C), lambda d, c: (0, blk(d, c), 0))
    per_dir = pl.BlockSpec((1, Bn, TC, C), lambda d, c: (d, 0, blk(d, c), 0))
    return pl.pallas_call(
        _rwkv_body,
        grid=(2, n),
        in_specs=[shared, shared, shared, per_dir, per_dir, per_dir],
        out_specs=pl.BlockSpec((1, RWKV_PAIRS, 1, N, 2 * N), lambda d, c: (d, 0, blk(d, c), 0, 0)),
        out_shape=jax.ShapeDtypeStruct((2, RWKV_PAIRS, n, N, 2 * N), F32),
        scratch_shapes=[pltpu.VMEM((RWKV_PAIRS, N, 2 * N), F32)],
        compiler_params=pltpu.CompilerParams(dimension_semantics=("parallel", "arbitrary"),
                                             vmem_limit_bytes=40 * 1024 * 1024),
        name="rwkv7_scan",
    )(kk, v, r, w, kka, k)


def rwkv7_scan_bidir(ins_c, ins_l):
    Bn, Lc = ins_c[0][0].shape[:2]
    Ll = ins_l[0][0].shape[1]
    L = Lc + Ll
    cat = lambda d, j: jnp.concatenate([ins_c[d][j], ins_l[d][j]], axis=1).reshape(Bn, L, W_RWKV)
    per_dir = lambda j: jnp.stack([cat(0, j), cat(1, j)])
    yT = rwkv_scan_bidir(cat(0, 4), cat(0, 3), cat(0, 0), per_dir(1), per_dir(5), per_dir(2), Lc // RWKV_TC)
    halves = W_RWKV // (2 * HEAD_DIM)
    y = yT.reshape(2, halves, Bn, L // HEAD_DIM, HEAD_DIM, 2, HEAD_DIM)
    y = y.transpose(0, 2, 3, 6, 1, 5, 4).reshape(2, Bn, L, H_RWKV, HEAD_DIM)
    return ([y[d, :, :Lc] for d in range(2)], [y[d, :, Lc:] for d in range(2)])


def rwkv_bonus(ins, p):
    r, _, kd, v = ins[:4]
    return jnp.sum(r * kd * p['rwkv_rk'], axis=-1, keepdims=True) * v


def rwkv_out(y, bonus, zg, p):
    Bn, L, H, N = y.shape
    mu = jnp.mean(y, axis=-1, keepdims=True)
    var = jnp.mean(jnp.square(y - mu), axis=-1, keepdims=True)
    yn = ((y - mu) * lax.rsqrt(var + RWKV_GN_EPS)).reshape(Bn, L, H * N) * p['rwkv_ln_g'] + p['rwkv_ln_b']
    g = jax.nn.sigmoid(zg) @ p['rwkv_g2']
    return (yn + bonus.reshape(Bn, L, H * N)) * g


def mixer_rwkv(f_c, f_l, p, need_ctx):
    fc, fl = rwkv_shift(f_c, p), rwkv_shift(f_l, p)
    ins_c = [rwkv_dir_inputs(fc, p, d) for d in range(2)]
    ins_l = [rwkv_dir_inputs(fl, p, d) for d in range(2)]
    o_c, o_l = rwkv7_scan_bidir(ins_c, ins_l)
    y_c, y_l = [], []
    for d in range(2):
        y_l.append((o_l[d], rwkv_bonus(ins_l[d], p)))
        if need_ctx:
            y_c.append((o_c[d], rwkv_bonus(ins_c[d], p)))
    out_l = rwkv_out(y_l[0][0] + y_l[1][0], y_l[0][1] + y_l[1][1], fl[5], p)
    out_c = rwkv_out(y_c[0][0] + y_c[1][0], y_c[0][1] + y_c[1][1], fc[5], p) if need_ctx else None
    return out_c, out_l


def mlstm_prep(f, p):
    q, k, v, o, gz = f
    Bn, L, _ = q.shape
    qk = jax.nn.silu(dwconv(jnp.concatenate([q, k], axis=-1), p['mlstm_conv_w'], p['mlstm_conv_b'], CONV_W // 2)).astype(F32)
    q, k = jnp.split(qk, 2, axis=-1)
    heads = lambda t: t.reshape(Bn, L, H_MLSTM, HEAD_DIM).transpose(0, 2, 1, 3)
    gates = (gz.astype(F32).reshape(Bn, L, 2, 2, H_MLSTM) + p['mlstm_gate_b']).transpose(2, 3, 0, 4, 1)
    return heads(q) * HEAD_DIM ** -0.5, heads(k), heads(v.astype(F32)), o.astype(F32), gates


def mlstm_chunkwise(q, k, v, ig, lf, state):
    Bn, H, L, N = q.shape
    T = MLSTM_CHUNK
    nc = L // T
    ch = lambda a: jnp.moveaxis(a.reshape((Bn, H, nc, T) + a.shape[3:]), 2, 0)
    mask = jnp.tril(jnp.ones((T, T), dtype=bool))

    def step(carry, inp):
        C, n, m = carry
        qc, kc, vc, ic, fc = inp
        b = jnp.cumsum(fc, axis=-1)
        logd = jnp.where(mask, b[..., :, None] - b[..., None, :] + ic[..., None, :], -jnp.inf)
        inter = b + m[..., None]
        mt = jnp.maximum(inter, jnp.max(logd, axis=-1))
        s = jnp.einsum('bhtn,bhsn->bhts', qc, kc) * jnp.exp(logd - mt[..., None])
        e_inter = jnp.exp(inter - mt)
        num = jnp.einsum('bhts,bhsn->bhtn', s, vc) + e_inter[..., None] * jnp.einsum('bhvk,bhtk->bhtv', C, qc)
        den = jnp.sum(s, axis=-1) + e_inter * jnp.einsum('bhk,bhtk->bht', n, qc)
        h = num / jnp.maximum(jnp.abs(den), jnp.exp(-mt))[..., None]
        bT = b[..., -1]
        logw = bT[..., None] - b + ic
        m_new = jnp.maximum(bT + m, jnp.max(logw, axis=-1))
        wgt = jnp.exp(logw - m_new[..., None])
        dec = jnp.exp(bT + m - m_new)
        C = dec[..., None, None] * C + jnp.einsum('bhs,bhsv,bhsk->bhvk', wgt, vc, kc)
        n = dec[..., None] * n + jnp.einsum('bhs,bhsk->bhk', wgt, kc)
        return (C, n, m_new), h

    state, hs = lax.scan(step, state, (ch(q), ch(k), ch(v), ch(ig), ch(lf)))
    return jnp.moveaxis(hs, 0, 2).reshape(Bn, H, L, N), state


def mixer_mlstm(f_c, f_l, p, need_ctx):
    f_l = [to_colmajor(t) for t in f_l]
    qc, kc, vc, oc, gc = mlstm_prep(f_c, p)
    ql, kl, vl, ol, gl = mlstm_prep(f_l, p)
    Bn = ql.shape[0]
    st0 = (jnp.zeros((Bn, H_MLSTM, HEAD_DIM, HEAD_DIM), F32), jnp.zeros((Bn, H_MLSTM, HEAD_DIM), F32),
           jnp.zeros((Bn, H_MLSTM), F32))
    h_c, h_l = [], []
    for d in range(2):
        hc, st = mlstm_chunkwise(flip_if(qc, d, 2), flip_if(kc, d, 2), flip_if(vc, d, 2), flip_if(gc[d, 0], d, 2),
                                 jax.nn.log_sigmoid(flip_if(gc[d, 1], d, 2)), st0)
        hl, _ = mlstm_chunkwise(flip_if(ql, d, 2), flip_if(kl, d, 2), flip_if(vl, d, 2), flip_if(gl[d, 0], d, 2),
                                jax.nn.log_sigmoid(flip_if(gl[d, 1], d, 2)), st)
        h_c.append(flip_if(hc, d, 2))
        h_l.append(flip_if(hl, d, 2))
    merge_heads = lambda h: h.transpose(0, 2, 1, 3).reshape(h.shape[0], h.shape[2], W_MLSTM)
    y_l = from_colmajor(jax.nn.sigmoid(ol) * merge_heads(h_l[0] + h_l[1]))
    y_c = jax.nn.sigmoid(oc) * merge_heads(h_c[0] + h_c[1]) if need_ctx else None
    return y_c, y_l


def hyena_spectrum(L, p):
    pos = jnp.arange(L, dtype=F32)
    t = pos / (L - 1)
    bands = (HYENA_EMB - 1) // 2
    freqs = jnp.linspace(1e-4, bands - 1, bands, dtype=F32)
    ang = (2 * math.pi / L) * pos[:, None] * freqs[None, :]
    z = jnp.concatenate([t[:, None], jnp.cos(ang), -jnp.sin(ang)], axis=-1)
    h = jnp.sin(p['hy_freq'][0] * (z @ p['hy_w1'] + p['hy_b1']))
    h = jnp.sin(p['hy_freq'][1] * (h @ p['hy_w2'] + p['hy_b2']))
    h = (h @ p['hy_w3']).astype(F32).reshape(L, HYENA_ORDER, 2, W_HYENA)
    deltas = jnp.abs(jnp.linspace(math.log(HYENA_TARGET) / HYENA_SLOW, math.log(HYENA_TARGET) / HYENA_FAST,
                                  W_HYENA, dtype=F32))
    h = h * jnp.exp(-t[:, None, None, None] * deltas)
    fwd, bwd = h[:, :, 0], h[:, :, 1]
    two = jnp.concatenate([fwd, jnp.zeros_like(fwd[:1]), jnp.flip(bwd[1:], axis=0)], axis=0)
    two = two / (jnp.sum(jnp.abs(two), axis=0, keepdims=True) + EPS)
    return jnp.fft.rfft(two, axis=0)


def long_conv(u, spec, bias):
    L = u.shape[1]
    y = jnp.fft.irfft(jnp.fft.rfft(u, n=2 * L, axis=1) * spec, n=2 * L, axis=1)[:, :L]
    return y + u * bias


def mixer_hyena(f, p):
    u = dwconv(jnp.concatenate(f, axis=-1), p['hy_conv_w'], p['hy_conv_b'], HYENA_SHORT // 2).astype(F32)
    v, x1, x2 = jnp.split(u, 3, axis=-1)
    spec = hyena_spectrum(u.shape[1], p)
    z = x1 * long_conv(v, spec[:, 0], p['hy_bias'][0])
    return x2 * long_conv(z, spec[:, 1], p['hy_bias'][1])


PEER_SEL = PEER_HEADS * PEER_TOPK
PEER_HALF = D_MODEL // 2
PEER_CHUNK = 4096
PEER_TT = 16
SC_GATHER_ROWS = 64


def pack_bf16_halves(tab):
    bits = lax.bitcast_convert_type(tab.astype(jnp.bfloat16), jnp.uint16).astype(jnp.uint32)
    half = tab.shape[1] // 2
    return bits[:, :half] | (bits[:, half:] << 16)


def sc_gather_pair(u_tab, v_tab, idx):
    n = idx.shape[0]
    W = u_tab.shape[1]
    info = plsc.get_sparse_core_info()
    nc, ns = info.num_cores, info.num_subcores
    per_w = n // (nc * ns)
    G = SC_GATHER_ROWS
    assert per_w * nc * ns == n and per_w % G == 0
    nsteps = per_w // G
    mesh = plsc.VectorSubcoreMesh(core_axis_name="c", subcore_axis_name="s")
    out = jax.ShapeDtypeStruct((n, W), u_tab.dtype)

    def body(u_hbm, v_hbm, idx_hbm, uo_hbm, vo_hbm, idx_v, urows, vrows, usem, vsem):
        wid = lax.axis_index("s") * nc + lax.axis_index("c")
        base = wid * per_w
        pltpu.sync_copy(idx_hbm.at[pl.ds(base, per_w)], idx_v)

        @pl.loop(0, nsteps)
        def _(i):
            off = pl.multiple_of(i * G, G)
            ids = idx_v.at[pl.ds(off, G)]
            cu = pltpu.async_copy(u_hbm.at[ids], urows, usem)
            cv = pltpu.async_copy(v_hbm.at[ids], vrows, vsem)
            cu.wait()
            pltpu.sync_copy(urows, uo_hbm.at[pl.ds(base + off, G)])
            cv.wait()
            pltpu.sync_copy(vrows, vo_hbm.at[pl.ds(base + off, G)])

    fn = pl.kernel(body, out_type=(out, out), mesh=mesh,
                   scratch_types=[pltpu.VMEM((per_w,), jnp.int32),
                                  pltpu.VMEM((G, W), u_tab.dtype), pltpu.VMEM((G, W), u_tab.dtype),
                                  pltpu.SemaphoreType.DMA, pltpu.SemaphoreType.DMA],
                   name="peer_sc_gather")
    return fn(u_tab, v_tab, idx)


def _unpack_halves(w):
    lo = lax.bitcast_convert_type(w << 16, F32)
    hi = lax.bitcast_convert_type(w & jnp.uint32(0xFFFF0000), F32)
    return lo, hi


def _peer_apply_body(x_ref, g_ref, ug_ref, vg_ref, o_ref):
    TT = x_ref.shape[0]
    gpad = jnp.concatenate([g_ref[...], jnp.zeros((PEER_SEL - TT, PEER_SEL), F32)], axis=0)
    gT = gpad.T
    for t in range(TT):
        rows = pl.ds(t * PEER_SEL, PEER_SEL)
        ulo, uhi = _unpack_halves(ug_ref[rows, :])
        xl = x_ref[pl.ds(t, 1), 0:PEER_HALF]
        xh = x_ref[pl.ds(t, 1), PEER_HALF:D_MODEL]
        dots = jnp.sum(ulo * xl + uhi * xh, axis=-1, keepdims=True)
        w = jax.nn.gelu(dots) * gT[:, t:t + 1]
        vlo, vhi = _unpack_halves(vg_ref[rows, :])
        o_ref[pl.ds(t, 1), 0:PEER_HALF] = jnp.sum(vlo * w, axis=0, keepdims=True)
        o_ref[pl.ds(t, 1), PEER_HALF:D_MODEL] = jnp.sum(vhi * w, axis=0, keepdims=True)


def peer_apply(x, gate, ug, vg):
    T = x.shape[0]
    TT = PEER_TT
    return pl.pallas_call(
        _peer_apply_body,
        grid=(T // TT,),
        in_specs=[pl.BlockSpec((TT, D_MODEL), lambda i: (i, 0)),
                  pl.BlockSpec((TT, PEER_SEL), lambda i: (i, 0)),
                  pl.BlockSpec((TT * PEER_SEL, PEER_HALF), lambda i: (i, 0)),
                  pl.BlockSpec((TT * PEER_SEL, PEER_HALF), lambda i: (i, 0))],
        out_specs=pl.BlockSpec((TT, D_MODEL), lambda i: (i, 0)),
        out_shape=jax.ShapeDtypeStruct((T, D_MODEL), F32),
        compiler_params=pltpu.CompilerParams(dimension_semantics=("parallel",),
                                             vmem_limit_bytes=40 * 1024 * 1024),
        name="peer_apply",
    )(x, gate, ug, vg)


ROUTE_TT = 1024
ROUTE_PAIR_ROWS = sum(-(-(PEER_TOPK // (i + 1)) // 8) * 8 for i in range(PEER_TOPK // 2)) + PEER_TOPK // 2


def _peer_route_body(s_ref, e_ref, g_ref, xs_ref, sv_ref, si_ref, cand_ref, cidx_ref, pf_ref, ts_ref):
    K, NK, TT = PEER_TOPK, PEER_NKEYS, s_ref.shape[-1]
    NEG = -jnp.inf
    xs_ref[...] = s_ref[0]
    kiota = lax.broadcasted_iota(jnp.int32, (NK, TT), 0).astype(F32)

    def half_topk(it, carry):
        for c in range(2):
            x = xs_ref[c]
            m = jnp.max(x, axis=0, keepdims=True)
            idx = jnp.min(jnp.where(x == m, kiota, float(NK)), axis=0, keepdims=True)
            xs_ref[c] = jnp.where(kiota == idx, NEG, x)
            sv_ref[c, pl.ds(it, 1), :] = m
            si_ref[c, pl.ds(it, 1), :] = idx
        return carry

    lax.fori_loop(0, K, half_topk, 0)

    jiota = lambda rows: lax.broadcasted_iota(jnp.int32, (rows, TT), 0).astype(F32)
    off = 0
    for i in range(K // 2):
        n = K // (i + 1)
        rows = -(-n // 8) * 8
        ok = jiota(rows) < float(n)
        cand_ref[pl.ds(off, rows), :] = jnp.where(ok, sv_ref[0, i:i + 1, :] + sv_ref[1, 0:rows, :], NEG)
        cidx_ref[pl.ds(off, rows), :] = si_ref[0, i:i + 1, :] * float(NK) + si_ref[1, 0:rows, :]
        pf_ref[pl.ds(off, rows), :] = jnp.where(ok, float(i * K) + jiota(rows), float(K * K))
        off += rows
    cand_ref[pl.ds(off, K // 2), :] = sv_ref[0, K // 2:K, :] + sv_ref[1, 0:1, :]
    cidx_ref[pl.ds(off, K // 2), :] = si_ref[0, K // 2:K, :] * float(NK) + si_ref[1, 0:1, :]
    pf_ref[pl.ds(off, K // 2), :] = (float(K // 2) + jiota(K // 2)) * float(K)
    piota = pf_ref[...]

    def pair_topk(it, carry):
        x = cand_ref[...]
        m = jnp.max(x, axis=0, keepdims=True)
        pos = jnp.min(jnp.where(x == m, piota, float(K * K)), axis=0, keepdims=True)
        sel = piota == pos
        cand_ref[...] = jnp.where(sel, NEG, x)
        ts_ref[pl.ds(it, 1), :] = m
        e_ref[0, pl.ds(it, 1), :] = jnp.max(jnp.where(sel, cidx_ref[...], -1.0), axis=0,
                                            keepdims=True).astype(jnp.int32)
        return carry

    lax.fori_loop(0, K, pair_topk, 0)
    ts = ts_ref[...]
    ex = jnp.exp(ts - ts[0:1, :])
    g_ref[0] = ex / jnp.sum(ex, axis=0, keepdims=True)


def peer_route_topk(sT):
    H, _, NK, N = sT.shape
    TT, K = ROUTE_TT, PEER_TOPK
    out_spec = pl.BlockSpec((1, K, TT), lambda h, i: (h, 0, i))
    return pl.pallas_call(
        _peer_route_body,
        grid=(H, N // TT),
        in_specs=[pl.BlockSpec((1, 2, NK, TT), lambda h, i: (h, 0, 0, i))],
        out_specs=[out_spec, out_spec],
        out_shape=[jax.ShapeDtypeStruct((H, K, N), jnp.int32), jax.ShapeDtypeStruct((H, K, N), F32)],
        scratch_shapes=[pltpu.VMEM((2, NK, TT), F32), pltpu.VMEM((2, K, TT), F32), pltpu.VMEM((2, K, TT), F32),
                        pltpu.VMEM((ROUTE_PAIR_ROWS, TT), F32), pltpu.VMEM((ROUTE_PAIR_ROWS, TT), F32),
                        pltpu.VMEM((ROUTE_PAIR_ROWS, TT), F32), pltpu.VMEM((K, TT), F32)],
        compiler_params=pltpu.CompilerParams(dimension_semantics=("parallel", "parallel")),
        name="peer_route_topk",
    )(sT)


def peer_route(xf, p):
    N = xf.shape[0]
    wq, keys = p['peer_wq'], p['peer_keys'].astype(F32)
    q = (xf @ wq).astype(F32).reshape(N, PEER_HEADS, 2, PEER_DQ // 2)
    sT = jnp.einsum('thcq,hckq->hckt', q, keys)
    eT, gT = peer_route_topk(sT)
    return (eT.transpose(2, 0, 1).reshape(N, PEER_SEL), gT.transpose(2, 0, 1).reshape(N, PEER_SEL))


def peer_ffn(h, p):
    Bn, L, D = h.shape
    N = Bn * L
    xf = h.reshape(N, D)
    eidx, gate = peer_route(xf, p)
    u_pk, v_pk = p['peer_u_pk'], p['peer_v_pk']
    outs = []
    for c0 in range(0, N, PEER_CHUNK):
        rows = slice(c0, c0 + PEER_CHUNK)
        ug, vg = sc_gather_pair(u_pk, v_pk, eidx[rows].reshape(-1))
        outs.append(peer_apply(xf[rows], gate[rows], ug, vg))
    return jnp.concatenate(outs, axis=0).reshape(Bn, L, D)


def merge_groups(ys, g, dtype):
    outs, off = [], 0
    for y, w in zip(ys, GROUP_WIDTHS):
        outs.append(rmsnorm(y, g[off:off + w]).astype(dtype))
        off += w
    return jnp.concatenate(outs, axis=-1)


def token_mixers(h_c, h_l, p, need_ctx):
    zc = split_cols(h_c @ p['w_in'])
    zl = split_cols(h_l @ p['w_in'])
    a_c, a_l = mixer_rglru(zc[0:2], zl[0:2], p, need_ctx)
    b_c, b_l = mixer_rwkv(zc[2:8], zl[2:8], p, need_ctx)
    m_c, m_l = mixer_mlstm(zc[8:13], zl[8:13], p, need_ctx)
    d_l = mixer_hyena(zl[13:16], p)
    o_l = merge_groups([a_l, b_l, m_l, d_l], p['grp_g'], h_l.dtype) @ p['w_out']
    if not need_ctx:
        return None, o_l
    d_c = mixer_hyena(zc[13:16], p)
    o_c = merge_groups([a_c, b_c, m_c, d_c], p['grp_g'], h_c.dtype) @ p['w_out']
    return o_c, o_l


def trunk_layer(x_l, x_c, c, c_ctx, p, need_ctx):
    Bn = c.shape[0]
    mod_l = (jax.nn.silu(c) @ p['ada_w'] + p['ada_b']).reshape(Bn, 6, 1, D_MODEL)
    mod_c = (jax.nn.silu(c_ctx) @ p['ada_w'] + p['ada_b']).reshape(6, 1, 1, D_MODEL)
    h_l = modulate(rmsnorm(x_l, p['norm1_g']), mod_l[:, 0], mod_l[:, 1])
    h_c = modulate(rmsnorm(x_c, p['norm1_g']), mod_c[0], mod_c[1])
    o_c, o_l = token_mixers(h_c, h_l, p, need_ctx)
    x_l = x_l + mod_l[:, 2] * o_l
    x_l = x_l + mod_l[:, 5] * peer_ffn(modulate(rmsnorm(x_l, p['norm2_g']), mod_l[:, 3], mod_l[:, 4]), p)
    if need_ctx:
        x_c = x_c + mod_c[2] * o_c
        x_c = x_c + mod_c[5] * peer_ffn(modulate(rmsnorm(x_c, p['norm2_g']), mod_c[3], mod_c[4]), p)
    return x_l, x_c


def _final_norm_body(x_ref, g_ref, o_ref):
    xf = x_ref[...]
    y = xf * lax.rsqrt(jnp.mean(xf * xf, axis=-1, keepdims=True) + EPS)
    o_ref[...] = y * g_ref[...]


def final_rmsnorm(x, g):
    Bn, L, D = x.shape
    rows = Bn * L
    tile = 1024
    out = pl.pallas_call(
        _final_norm_body,
        grid=(rows // tile,),
        in_specs=[pl.BlockSpec((tile, D), lambda i: (i, 0)), pl.BlockSpec((1, D), lambda i: (0, 0))],
        out_specs=pl.BlockSpec((tile, D), lambda i: (i, 0)),
        out_shape=jax.ShapeDtypeStruct((rows, D), x.dtype),
        name="final_rmsnorm",
    )(x.reshape(rows, D), g.reshape(1, D))
    return out.reshape(Bn, L, D)


def kernel(x, c, ctx, c_ctx, ada_w, ada_b, norm1_g, norm2_g, w_in, w_out, grp_g,
           lru_conv_w, lru_conv_b, lru_wr, lru_br, lru_wi, lru_bi, lru_lam,
           rwkv_mu, rwkv_w0, rwkv_w2, rwkv_a0, rwkv_a2, rwkv_g2, rwkv_kk, rwkv_ka, rwkv_rk, rwkv_ln_g, rwkv_ln_b,
           mlstm_conv_w, mlstm_conv_b, mlstm_gate_b,
           hy_conv_w, hy_conv_b, hy_w1, hy_b1, hy_w2, hy_b2, hy_w3, hy_freq, hy_bias,
           peer_wq, peer_keys, peer_u, peer_v, final_g):
    x_l, x_c = x, ctx
    for i in range(DEPTH):
        p = dict(ada_w=ada_w[i], ada_b=ada_b[i], norm1_g=norm1_g[i], norm2_g=norm2_g[i], w_in=w_in[i],
                 w_out=w_out[i], grp_g=grp_g[i],
                 lru_conv_w=lru_conv_w[i], lru_conv_b=lru_conv_b[i], lru_wr=lru_wr[i], lru_br=lru_br[i],
                 lru_wi=lru_wi[i], lru_bi=lru_bi[i], lru_lam=lru_lam[i],
                 rwkv_mu=rwkv_mu[i], rwkv_w0=rwkv_w0[i], rwkv_w2=rwkv_w2[i], rwkv_a0=rwkv_a0[i],
                 rwkv_a2=rwkv_a2[i], rwkv_g2=rwkv_g2[i], rwkv_kk=rwkv_kk[i], rwkv_ka=rwkv_ka[i],
                 rwkv_rk=rwkv_rk[i], rwkv_ln_g=rwkv_ln_g[i], rwkv_ln_b=rwkv_ln_b[i],
                 mlstm_conv_w=mlstm_conv_w[i], mlstm_conv_b=mlstm_conv_b[i], mlstm_gate_b=mlstm_gate_b[i],
                 hy_conv_w=hy_conv_w[i], hy_conv_b=hy_conv_b[i], hy_w1=hy_w1[i], hy_b1=hy_b1[i],
                 hy_w2=hy_w2[i], hy_b2=hy_b2[i], hy_w3=hy_w3[i], hy_freq=hy_freq[i], hy_bias=hy_bias[i],
                 peer_wq=peer_wq[i], peer_keys=peer_keys[i],
                 peer_u_pk=pack_bf16_halves(peer_u[i]), peer_v_pk=pack_bf16_halves(peer_v[i]))
        x_l, x_c = trunk_layer(x_l, x_c, c, c_ctx, p, i < DEPTH - 1)
    return final_rmsnorm(x_l, final_g)
```

```python
import functools
import math
import jax, jax.numpy as jnp
from jax import lax
import numpy as np
from jax.experimental import pallas as pl
from jax.experimental.pallas import tpu as pltpu
from jax.experimental.pallas import tpu_sc as plsc

D_MODEL = 1024
BATCH = 16
SEQ = 2048
DEPTH = 2

GRID_W = 64
CTX_LEN = 256
F32 = jnp.float32
EPS = 1e-6
HEAD_DIM = 64
D_MIX = D_MODEL
W_LRU = D_MIX // 4
W_RWKV = D_MIX // 4
W_MLSTM = D_MIX // 4
W_HYENA = D_MIX - W_LRU - W_RWKV - W_MLSTM
GROUP_WIDTHS = (W_LRU, W_RWKV, W_MLSTM, W_HYENA)
H_LRU = W_LRU // HEAD_DIM
H_RWKV = W_RWKV // HEAD_DIM
H_MLSTM = W_MLSTM // HEAD_DIM
CONV_W = 4
LRU_C = 8.0
RWKV_LORA_W = 32
RWKV_LORA_A = 32
RWKV_LORA_G = 64
RWKV_GN_EPS = 64e-5
MLSTM_CHUNK = 64
HYENA_ORDER = 2
HYENA_SHORT = 3
HYENA_EMB = 33
HYENA_HID = 64
HYENA_TARGET = 1e-2
HYENA_FAST = 0.3
HYENA_SLOW = 1.5
PEER_HEADS = 8
PEER_NKEYS = 128
PEER_EXPERTS = PEER_NKEYS * PEER_NKEYS
PEER_TOPK = 16
PEER_DQ = 256
PEER_BLOCK = 128
IN_SPLITS = (W_LRU, W_LRU, W_RWKV, W_RWKV, W_RWKV, RWKV_LORA_W, RWKV_LORA_A, RWKV_LORA_G, W_MLSTM, W_MLSTM, W_MLSTM, W_MLSTM, 4 * H_MLSTM, W_HYENA, W_HYENA, W_HYENA)
D_IN = sum(IN_SPLITS)


def rmsnorm(x, g):
    xf = x.astype(F32)
    y = xf * lax.rsqrt(jnp.mean(xf * xf, axis=-1, keepdims=True) + EPS)
    return (y * g.astype(F32)).astype(x.dtype)


def modulate(h, shift, scale):
    return h * (1 + scale) + shift


def flip_if(a, d, axis=1):
    return jnp.flip(a, axis) if d == 1 else a


def split_cols(z):
    offs = np.cumsum(IN_SPLITS)[:-1].tolist()
    return jnp.split(z, offs, axis=-1)


def dwconv(x, w, b, pad_left):
    K = w.shape[0]
    y = lax.conv_general_dilated(x, w[:, None, :].astype(x.dtype), (1,), [(pad_left, K - 1 - pad_left)],
                                 dimension_numbers=('NWC', 'WIO', 'NWC'), feature_group_count=x.shape[-1])
    return y + b.astype(x.dtype)


def token_shift(x, mu):
    prev = jnp.pad(x, ((0, 0), (1, 0), (0, 0)))[:, :-1]
    nxt = jnp.pad(x, ((0, 0), (0, 1), (0, 0)))[:, 1:]
    return x + mu[0] * (prev - x) + mu[1] * (nxt - x)


def split_heads(t, h):
    Bn, L, W = t.shape
    return t.reshape(Bn, L, h, W // h)


def to_colmajor(a):
    Bn, L, C = a.shape
    rows = L // GRID_W
    return a.reshape(Bn, rows, GRID_W, C).transpose(0, 2, 1, 3).reshape(Bn, L, C)


def from_colmajor(a):
    Bn, L, C = a.shape
    rows = L // GRID_W
    return a.reshape(Bn, GRID_W, rows, C).transpose(0, 2, 1, 3).reshape(Bn, L, C)


LRU_TC = 256


def _expm1(z):
    u = jnp.exp(z)
    return jnp.where(u == 1.0, z, (u - 1.0) * z / jnp.where(u == 1.0, 1.0, jnp.log(u)))


def _lru_body(x_ref, h0_ref, wr_ref, wi_ref, br_ref, bi_ref, sp_ref, h_ref, hT_ref, st_ref, *, reverse):
    TC, W = x_ref.shape[1], x_ref.shape[2]

    @pl.when(pl.program_id(1) == 0)
    def _():
        st_ref[...] = h0_ref[0]

    x = x_ref[0]
    r = jax.nn.sigmoid(jnp.dot(x, wr_ref[...], preferred_element_type=F32) + br_ref[...])
    i = jax.nn.sigmoid(jnp.dot(x, wi_ref[...], preferred_element_type=F32) + bi_ref[...])
    log_a = -LRU_C * r * sp_ref[...]
    a = jnp.exp(log_a)
    b = jnp.sqrt(-_expm1(2.0 * log_a)) * (i * x)
    row = lax.broadcasted_iota(jnp.int32, (TC, W), 0)
    s = 1
    while s < TC:
        if reverse:
            keep = row < TC - s
            a_sh = jnp.where(keep, pltpu.roll(a, TC - s, 0), 1.0)
            b_sh = jnp.where(keep, pltpu.roll(b, TC - s, 0), 0.0)
        else:
            keep = row >= s
            a_sh = jnp.where(keep, pltpu.roll(a, s, 0), 1.0)
            b_sh = jnp.where(keep, pltpu.roll(b, s, 0), 0.0)
        b = b + a * b_sh
        a = a * a_sh
        s *= 2
    h = b + a * st_ref[...]
    h_ref[0] = h
    last = h[0:1, :] if reverse else h[TC - 1:TC, :]
    st_ref[...] = last
    hT_ref[0] = last


def lru_scan_dir(xc, h0, wr_bd, wi_bd, br, bi, sp, reverse):
    Bn, L, W = xc.shape
    TC = min(LRU_TC, L)
    n = L // TC
    tmap = (lambda b, c: (b, n - 1 - c, 0)) if reverse else (lambda b, c: (b, c, 0))
    wspec = pl.BlockSpec((W, W), lambda b, c: (0, 0))
    vspec = pl.BlockSpec((1, W), lambda b, c: (0, 0))
    sspec = pl.BlockSpec((1, 1, W), lambda b, c: (b, 0, 0))
    return pl.pallas_call(
        functools.partial(_lru_body, reverse=reverse),
        grid=(Bn, n),
        in_specs=[pl.BlockSpec((1, TC, W), tmap), sspec, wspec, wspec, vspec, vspec, vspec],
        out_specs=[pl.BlockSpec((1, TC, W), tmap), sspec],
        out_shape=[jax.ShapeDtypeStruct((Bn, L, W), F32), jax.ShapeDtypeStruct((Bn, 1, W), F32)],
        scratch_shapes=[pltpu.VMEM((1, W), F32)],
        compiler_params=pltpu.CompilerParams(dimension_semantics=("parallel", "arbitrary")),
        name="rglru_scan",
    )(xc, h0, wr_bd, wi_bd, br, bi, sp)


def head_block_diag(w):
    D2, H, N, _ = w.shape
    return jnp.einsum('dhij,hg->dhigj', w, jnp.eye(H, dtype=w.dtype)).reshape(D2, H * N, H * N)


def mixer_rglru(f_c, f_l, p, need_ctx):
    (x_c, g_c), (x_l, g_l) = f_c, f_l
    xc_c = dwconv(x_c, p['lru_conv_w'], p['lru_conv_b'], CONV_W // 2).astype(F32)
    xc_l = dwconv(x_l, p['lru_conv_w'], p['lru_conv_b'], CONV_W // 2).astype(F32)
    h0 = jnp.zeros((xc_l.shape[0], 1, W_LRU), F32)
    wr, wi = head_block_diag(p['lru_wr']), head_block_diag(p['lru_wi'])
    sp = jax.nn.softplus(-p['lru_lam'])
    hs_c, hs_l = [], []
    for d in range(2):
        gates = (wr[d], wi[d], p['lru_br'][d][None], p['lru_bi'][d][None], sp[d][None])
        hc, st = lru_scan_dir(xc_c, h0, *gates, reverse=(d == 1))
        hl, _ = lru_scan_dir(xc_l, st, *gates, reverse=(d == 1))
        hs_c.append(hc)
        hs_l.append(hl)
    y_l = jax.nn.gelu(g_l.astype(F32)) * (hs_l[0] + hs_l[1])
    y_c = jax.nn.gelu(g_c.astype(F32)) * (hs_c[0] + hs_c[1]) if need_ctx else None
    return y_c, y_l


def rwkv_shift(f, p):
    r, k, v, zw, za, zg = [t.astype(F32) for t in f]
    mu = p['rwkv_mu']
    return (token_shift(r, mu[0]), token_shift(k, mu[1]), token_shift(v, mu[2]), zw, za, zg)


def rwkv_dir_inputs(f, p, d):
    r, k, v, zw, za, _ = f
    w_log = -jax.nn.softplus(-(p['rwkv_w0'][d] + jnp.tanh(zw) @ p['rwkv_w2'][d])) - 0.5
    decay = jnp.exp(-jnp.exp(w_log))
    a = jax.nn.sigmoid(p['rwkv_a0'][d] + za @ p['rwkv_a2'][d])
    kk = split_heads(k * p['rwkv_kk'], H_RWKV)
    kk = kk / jnp.maximum(jnp.linalg.norm(kk, axis=-1, keepdims=True), 1e-12)
    kd = k * (1 + (a - 1) * p['rwkv_ka'])
    return [split_heads(r, H_RWKV), split_heads(decay, H_RWKV), split_heads(kd, H_RWKV),
            split_heads(v, H_RWKV), kk, kk * split_heads(a, H_RWKV)]


RWKV_TC = HEAD_DIM
RWKV_PAIRS = 32


def _rwkv_body(kk_ref, v_ref, r_ref, w_ref, kka_ref, k_ref, y_ref, s_ref):
    NP, N = RWKV_PAIRS, HEAD_DIM
    rev = pl.program_id(0)

    @pl.when(pl.program_id(1) == 0)
    def _():
        s_ref[...] = jnp.zeros_like(s_ref)

    y_ref[...] = jnp.zeros_like(y_ref)
    lane = lax.broadcasted_iota(jnp.int32, (N, 2 * N), 1)
    row = lax.broadcasted_iota(jnp.int32, (N, 2 * N), 0)
    diag = (lane % N) == row
    seg4 = lax.broadcasted_iota(jnp.int32, (4 * N, 4 * N), 0) // N == \
        lax.broadcasted_iota(jnp.int32, (4 * N, 4 * N), 1) // N
    ones4 = jnp.where(seg4, 1.0, 0.0).astype(jnp.bfloat16)

    def segsum_bcast(x):
        H = x.shape[0] // 2
        xb = x.astype(jnp.bfloat16)
        lhs = jnp.concatenate([xb[:H].reshape(H * N, 2 * N), xb[H:].reshape(H * N, 2 * N)], axis=-1)
        out = jnp.dot(lhs, ones4, preferred_element_type=F32)
        return jnp.concatenate([out[:, :2 * N].reshape(H, N, 2 * N), out[:, 2 * N:].reshape(H, N, 2 * N)], axis=0)

    def rowv(ref, t):
        x = ref[:, pl.ds(t, 1), :] if len(ref.shape) == 3 else ref[0, :, pl.ds(t, 1), :]
        return jnp.concatenate([x[:, :, :2 * N], x[:, :, 2 * N:]], axis=0)

    def step(i, carry):
        t = i + rev * (RWKV_TC - 1 - 2 * i)
        tp = jnp.clip(t - 1 + 2 * rev, 0, RWKV_TC - 1)
        S = s_ref[...]
        R = segsum_bcast(jnp.concatenate([S * rowv(kk_ref, t), jnp.where(diag, rowv(v_ref, t), 0.0),
                                          S * rowv(r_ref, tp)], axis=0))
        y_ref[0, :, 0] = jnp.where(((lane % N) == tp) & (i > 0), R[2 * NP:], y_ref[0, :, 0])
        s_ref[...] = S * rowv(w_ref, t) - R[:NP] * rowv(kka_ref, t) + R[NP:2 * NP] * rowv(k_ref, t)
        return carry

    lax.fori_loop(0, RWKV_TC, step, 0)
    t_last = (RWKV_TC - 1) * (1 - rev)
    Y = segsum_bcast(s_ref[...] * rowv(r_ref, t_last))
    y_ref[0, :, 0] = jnp.where((lane % N) == t_last, Y, y_ref[0, :, 0])


def rwkv_scan_bidir(kk, v, r, w, kka, k, n_ctx_blocks):
    Bn, L, C = kk.shape
    TC, N = RWKV_TC, HEAD_DIM
    n = L // TC
    assert RWKV_PAIRS == Bn * C // (2 * N)

    def blk(d, c):
        back = jnp.where(c < n_ctx_blocks, n_ctx_blocks - 1 - c, n + n_ctx_blocks - 1 - c)
        return jnp.where(d == 0, c, back)

    shared = pl.BlockSpec((Bn, TC, C), lambda d, c: (0, blk(d, c), 0))
    per_dir = pl.BlockSpec((1, Bn, TC, C), lambda d, c: (d, 0, blk(d, c), 0))
    return pl.pallas_call(
        _rwkv_body,
        grid=(2, n),
        in_specs=[shared, shared, shared, per_dir, per_dir, per_dir],
        out_specs=pl.BlockSpec((1, RWKV_PAIRS, 1, N, 2 * N), lambda d, c: (d, 0, blk(d, c), 0, 0)),
        out_shape=jax.ShapeDtypeStruct((2, RWKV_PAIRS, n, N, 2 * N), F32),
        scratch_shapes=[pltpu.VMEM((RWKV_PAIRS, N, 2 * N), F32)],
        compiler_params=pltpu.CompilerParams(dimension_semantics=("parallel", "arbitrary"),
                                             vmem_limit_bytes=40 * 1024 * 1024),
        name="rwkv7_scan",
    )(kk, v, r, w, kka, k)


def rwkv7_scan_bidir(ins_c, ins_l):
    Bn, Lc = ins_c[0][0].shape[:2]
    Ll = ins_l[0][0].shape[1]
    L = Lc + Ll
    cat = lambda d, j: jnp.concatenate([ins_c[d][j], ins_l[d][j]], axis=1).reshape(Bn, L, W_RWKV)
    per_dir = lambda j: jnp.stack([cat(0, j), cat(1, j)])
    yT = rwkv_scan_bidir(cat(0, 4), cat(0, 3), cat(0, 0), per_dir(1), per_dir(5), per_dir(2), Lc // RWKV_TC)
    halves = W_RWKV // (2 * HEAD_DIM)
    y = yT.reshape(2, halves, Bn, L // HEAD_DIM, HEAD_DIM, 2, HEAD_DIM)
    y = y.transpose(0, 2, 3, 6, 1, 5, 4).reshape(2, Bn, L, H_RWKV, HEAD_DIM)
    return ([y[d, :, :Lc] for d in range(2)], [y[d, :, Lc:] for d in range(2)])


def rwkv_bonus(ins, p):
    r, _, kd, v = ins[:4]
    return jnp.sum(r * kd * p['rwkv_rk'], axis=-1, keepdims=True) * v


def rwkv_out(y, bonus, zg, p):
    Bn, L, H, N = y.shape
    mu = jnp.mean(y, axis=-1, keepdims=True)
    var = jnp.mean(jnp.square(y - mu), axis=-1, keepdims=True)
    yn = ((y - mu) * lax.rsqrt(var + RWKV_GN_EPS)).reshape(Bn, L, H * N) * p['rwkv_ln_g'] + p['rwkv_ln_b']
    g = jax.nn.sigmoid(zg) @ p['rwkv_g2']
    return (yn + bonus.reshape(Bn, L, H * N)) * g


def mixer_rwkv(f_c, f_l, p, need_ctx):
    fc, fl = rwkv_shift(f_c, p), rwkv_shift(f_l, p)
    ins_c = [rwkv_dir_inputs(fc, p, d) for d in range(2)]
    ins_l = [rwkv_dir_inputs(fl, p, d) for d in range(2)]
    o_c, o_l = rwkv7_scan_bidir(ins_c, ins_l)
    y_c, y_l = [], []
    for d in range(2):
        y_l.append((o_l[d], rwkv_bonus(ins_l[d], p)))
        if need_ctx:
            y_c.append((o_c[d], rwkv_bonus(ins_c[d], p)))
    out_l = rwkv_out(y_l[0][0] + y_l[1][0], y_l[0][1] + y_l[1][1], fl[5], p)
    out_c = rwkv_out(y_c[0][0] + y_c[1][0], y_c[0][1] + y_c[1][1], fc[5], p) if need_ctx else None
    return out_c, out_l


def mlstm_prep(f, p):
    q, k, v, o, gz = f
    Bn, L, _ = q.shape
    qk = jax.nn.silu(dwconv(jnp.concatenate([q, k], axis=-1), p['mlstm_conv_w'], p['mlstm_conv_b'], CONV_W // 2)).astype(F32)
    q, k = jnp.split(qk, 2, axis=-1)
    gates = gz.astype(F32).reshape(Bn, L, 2, 2, H_MLSTM) + p['mlstm_gate_b']
    return q * HEAD_DIM ** -0.5, k, v.astype(F32), o.astype(F32), gates


def _mlstm_body(q_ref, k_ref, v_ref, gc_ref, gr_ref, h_ref, c_ref, n_ref, m_ref):
    T, N, H = MLSTM_CHUNK, HEAD_DIM, H_MLSTM
    rev = pl.program_id(0)

    @pl.when(pl.program_id(2) == 0)
    def _():
        c_ref[...] = jnp.zeros_like(c_ref)
        n_ref[...] = jnp.zeros_like(n_ref)
        m_ref[...] = jnp.zeros_like(m_ref)

    ti = lax.broadcasted_iota(jnp.int32, (T, T), 0)
    si = lax.broadcasted_iota(jnp.int32, (T, T), 1)
    mask = (si - ti) * (1 - 2 * rev) <= 0
    tri = jnp.where(mask, 1.0, 0.0)
    gc = gc_ref[0, 0, 0]
    gr = gr_ref[0, 0, 0]
    hp = lax.Precision.HIGHEST
    b_col = jnp.dot(tri, gc[:, H:], precision=hp, preferred_element_type=F32)
    b_row = lax.dot_general(gr[H:], tri, (((1,), (1,)), ((), ())), precision=hp, preferred_element_type=F32)
    b_tot = jnp.sum(gc[:, H:], axis=0, keepdims=True)
    outs = []
    for h in range(H):
        sl = slice(h * N, (h + 1) * N)
        qh, kh, vh = q_ref[0, :, sl], k_ref[0, :, sl], v_ref[0, :, sl]
        C, n, m = c_ref[h], n_ref[h], m_ref[h][:, 0:1]
        bc, br = b_col[:, h:h + 1], b_row[h:h + 1, :]
        ic, ir = gc[:, h:h + 1], gr[h:h + 1, :]
        bT = b_tot[:, h:h + 1]
        logd = jnp.where(mask, bc - br + ir, -jnp.inf)
        inter = bc + m
        mt = jnp.maximum(inter, jnp.max(logd, axis=-1, keepdims=True))
        s = lax.dot_general(qh, kh, (((1,), (1,)), ((), ())), preferred_element_type=F32) * jnp.exp(logd - mt)
        e_inter = jnp.exp(inter - mt)
        qc = lax.dot_general(qh, C, (((1,), (1,)), ((), ())), preferred_element_type=F32)
        num = jnp.dot(s, vh, preferred_element_type=F32) + e_inter * qc
        den = jnp.sum(s, axis=-1, keepdims=True) + e_inter * jnp.sum(qh * n, axis=-1, keepdims=True)
        outs.append(num / jnp.maximum(jnp.abs(den), jnp.exp(-mt)))
        m_new = jnp.maximum(bT + m, jnp.max(bT - br + ir, axis=-1, keepdims=True))
        w_col = jnp.exp(bT - bc + ic - m_new)
        dec = jnp.exp(bT + m - m_new)
        c_ref[h] = dec * C + lax.dot_general(vh * w_col, kh, (((0,), (0,)), ((), ())), preferred_element_type=F32)
        n_ref[h] = dec * n + jnp.sum(kh * w_col, axis=0, keepdims=True)
        m_ref[h] = jnp.broadcast_to(m_new, (1, 128))
    h_ref[0, 0] = jnp.concatenate(outs, axis=-1)


def mlstm_scan_bidir(q, k, v, gcol, grow, n_ctx_blocks):
    Bn, L, W = q.shape
    T = MLSTM_CHUNK
    n = L // T

    def blk(d, c):
        back = jnp.where(c < n_ctx_blocks, n_ctx_blocks - 1 - c, n + n_ctx_blocks - 1 - c)
        return jnp.where(d == 0, c, back)

    xspec = pl.BlockSpec((1, T, W), lambda d, b, c: (b, blk(d, c), 0))
    gcs = pl.BlockSpec((1, 1, 1, T, 2 * H_MLSTM), lambda d, b, c: (d, b, blk(d, c), 0, 0))
    grs = pl.BlockSpec((1, 1, 1, 2 * H_MLSTM, T), lambda d, b, c: (d, b, blk(d, c), 0, 0))
    return pl.pallas_call(
        _mlstm_body,
        grid=(2, Bn, n),
        in_specs=[xspec, xspec, xspec, gcs, grs],
        out_specs=pl.BlockSpec((1, 1, T, W), lambda d, b, c: (d, b, blk(d, c), 0)),
        out_shape=jax.ShapeDtypeStruct((2, Bn, L, W), F32),
        scratch_shapes=[pltpu.VMEM((H_MLSTM, HEAD_DIM, HEAD_DIM), F32), pltpu.VMEM((H_MLSTM, 1, HEAD_DIM), F32),
                        pltpu.VMEM((H_MLSTM, 1, 128), F32)],
        compiler_params=pltpu.CompilerParams(dimension_semantics=("parallel", "parallel", "arbitrary")),
        name="mlstm_chunkwise",
    )(q, k, v, gcol, grow)


def mixer_mlstm(f_c, f_l, p, need_ctx):
    f_l = [to_colmajor(t) for t in f_l]
    qc, kc, vc, oc, gc = mlstm_prep(f_c, p)
    ql, kl, vl, ol, gl = mlstm_prep(f_l, p)
    Bn, Lc = qc.shape[:2]
    cat = lambda a, b: jnp.concatenate([a, b], axis=1)
    g = cat(gc, gl)
    L = g.shape[1]
    g = jnp.concatenate([g[:, :, :, 0], jax.nn.log_sigmoid(g[:, :, :, 1])], axis=-1)
    gcol = g.transpose(2, 0, 1, 3).reshape(2, Bn, L // MLSTM_CHUNK, MLSTM_CHUNK, 2 * H_MLSTM)
    h = mlstm_scan_bidir(cat(qc, ql), cat(kc, kl), cat(vc, vl), gcol, gcol.transpose(0, 1, 2, 4, 3),
                         Lc // MLSTM_CHUNK)
    hs = h[0] + h[1]
    y_l = from_colmajor(jax.nn.sigmoid(ol) * hs[:, Lc:])
    y_c = jax.nn.sigmoid(oc) * hs[:, :Lc] if need_ctx else None
    return y_c, y_l


def hyena_spectrum(L, p):
    pos = jnp.arange(L, dtype=F32)
    t = pos / (L - 1)
    bands = (HYENA_EMB - 1) // 2
    freqs = jnp.linspace(1e-4, bands - 1, bands, dtype=F32)
    ang = (2 * math.pi / L) * pos[:, None] * freqs[None, :]
    z = jnp.concatenate([t[:, None], jnp.cos(ang), -jnp.sin(ang)], axis=-1)
    h = jnp.sin(p['hy_freq'][0] * (z @ p['hy_w1'] + p['hy_b1']))
    h = jnp.sin(p['hy_freq'][1] * (h @ p['hy_w2'] + p['hy_b2']))
    h = (h @ p['hy_w3']).astype(F32).reshape(L, HYENA_ORDER, 2, W_HYENA)
    deltas = jnp.abs(jnp.linspace(math.log(HYENA_TARGET) / HYENA_SLOW, math.log(HYENA_TARGET) / HYENA_FAST,
                                  W_HYENA, dtype=F32))
    h = h * jnp.exp(-t[:, None, None, None] * deltas)
    fwd, bwd = h[:, :, 0], h[:, :, 1]
    two = jnp.concatenate([fwd, jnp.zeros_like(fwd[:1]), jnp.flip(bwd[1:], axis=0)], axis=0)
    two = two / (jnp.sum(jnp.abs(two), axis=0, keepdims=True) + EPS)
    return jnp.fft.rfft(two, axis=0)


def long_conv(u, spec, bias):
    L = u.shape[1]
    y = jnp.fft.irfft(jnp.fft.rfft(u, n=2 * L, axis=1) * spec, n=2 * L, axis=1)[:, :L]
    return y + u * bias


def mixer_hyena(f, p):
    u = dwconv(jnp.concatenate(f, axis=-1), p['hy_conv_w'], p['hy_conv_b'], HYENA_SHORT // 2).astype(F32)
    v, x1, x2 = jnp.split(u, 3, axis=-1)
    spec = hyena_spectrum(u.shape[1], p)
    z = x1 * long_conv(v, spec[:, 0], p['hy_bias'][0])
    return x2 * long_conv(z, spec[:, 1], p['hy_bias'][1])


PEER_SEL = PEER_HEADS * PEER_TOPK
PEER_HALF = D_MODEL // 2
PEER_CHUNK = 4096
PEER_TT = 16
SC_GATHER_ROWS = 64


def pack_bf16_halves(tab):
    bits = lax.bitcast_convert_type(tab.astype(jnp.bfloat16), jnp.uint16).astype(jnp.uint32)
    half = tab.shape[1] // 2
    return bits[:, :half] | (bits[:, half:] << 16)


def sc_gather_pair(u_tab, v_tab, idx):
    n = idx.shape[0]
    W = u_tab.shape[1]
    info = plsc.get_sparse_core_info()
    nc, ns = info.num_cores, info.num_subcores
    per_w = n // (nc * ns)
    G = SC_GATHER_ROWS
    assert per_w * nc * ns == n and per_w % G == 0
    nsteps = per_w // G
    mesh = plsc.VectorSubcoreMesh(core_axis_name="c", subcore_axis_name="s")
    out = jax.ShapeDtypeStruct((n, W), u_tab.dtype)

    def body(u_hbm, v_hbm, idx_hbm, uo_hbm, vo_hbm, idx_v, urows, vrows, usem, vsem):
        wid = lax.axis_index("s") * nc + lax.axis_index("c")
        base = wid * per_w
        pltpu.sync_copy(idx_hbm.at[pl.ds(base, per_w)], idx_v)

        @pl.loop(0, nsteps)
        def _(i):
            off = pl.multiple_of(i * G, G)
            ids = idx_v.at[pl.ds(off, G)]
            cu = pltpu.async_copy(u_hbm.at[ids], urows, usem)
            cv = pltpu.async_copy(v_hbm.at[ids], vrows, vsem)
            cu.wait()
            pltpu.sync_copy(urows, uo_hbm.at[pl.ds(base + off, G)])
            cv.wait()
            pltpu.sync_copy(vrows, vo_hbm.at[pl.ds(base + off, G)])

    fn = pl.kernel(body, out_type=(out, out), mesh=mesh,
                   scratch_types=[pltpu.VMEM((per_w,), jnp.int32),
                                  pltpu.VMEM((G, W), u_tab.dtype), pltpu.VMEM((G, W), u_tab.dtype),
                                  pltpu.SemaphoreType.DMA, pltpu.SemaphoreType.DMA],
                   name="peer_sc_gather")
    return fn(u_tab, v_tab, idx)


def _unpack_halves(w):
    lo = lax.bitcast_convert_type(w << 16, F32)
    hi = lax.bitcast_convert_type(w & jnp.uint32(0xFFFF0000), F32)
    return lo, hi


def _peer_apply_body(x_ref, g_ref, ug_ref, vg_ref, o_ref):
    TT = x_ref.shape[0]
    gpad = jnp.concatenate([g_ref[...], jnp.zeros((PEER_SEL - TT, PEER_SEL), F32)], axis=0)
    gT = gpad.T
    for t in range(TT):
        rows = pl.ds(t * PEER_SEL, PEER_SEL)
        ulo, uhi = _unpack_halves(ug_ref[rows, :])
        xl = x_ref[pl.ds(t, 1), 0:PEER_HALF]
        xh = x_ref[pl.ds(t, 1), PEER_HALF:D_MODEL]
        dots = jnp.sum(ulo * xl + uhi * xh, axis=-1, keepdims=True)
        w = jax.nn.gelu(dots) * gT[:, t:t + 1]
        vlo, vhi = _unpack_halves(vg_ref[rows, :])
        o_ref[pl.ds(t, 1), 0:PEER_HALF] = jnp.sum(vlo * w, axis=0, keepdims=True)
        o_ref[pl.ds(t, 1), PEER_HALF:D_MODEL] = jnp.sum(vhi * w, axis=0, keepdims=True)


def peer_apply(x, gate, ug, vg):
    T = x.shape[0]
    TT = PEER_TT
    return pl.pallas_call(
        _peer_apply_body,
        grid=(T // TT,),
        in_specs=[pl.BlockSpec((TT, D_MODEL), lambda i: (i, 0)),
                  pl.BlockSpec((TT, PEER_SEL), lambda i: (i, 0)),
                  pl.BlockSpec((TT * PEER_SEL, PEER_HALF), lambda i: (i, 0)),
                  pl.BlockSpec((TT * PEER_SEL, PEER_HALF), lambda i: (i, 0))],
        out_specs=pl.BlockSpec((TT, D_MODEL), lambda i: (i, 0)),
        out_shape=jax.ShapeDtypeStruct((T, D_MODEL), F32),
        compiler_params=pltpu.CompilerParams(dimension_semantics=("parallel",),
                                             vmem_limit_bytes=40 * 1024 * 1024),
        name="peer_apply",
    )(x, gate, ug, vg)


ROUTE_TT = 1024
ROUTE_PAIR_ROWS = sum(-(-(PEER_TOPK // (i + 1)) // 8) * 8 for i in range(PEER_TOPK // 2)) + PEER_TOPK // 2


def _peer_route_body(s_ref, e_ref, g_ref, xs_ref, sv_ref, si_ref, cand_ref, cidx_ref, pf_ref, ts_ref):
    K, NK, TT = PEER_TOPK, PEER_NKEYS, s_ref.shape[-1]
    NEG = -jnp.inf
    xs_ref[...] = s_ref[0]
    kiota = lax.broadcasted_iota(jnp.int32, (NK, TT), 0).astype(F32)

    def half_topk(it, carry):
        for c in range(2):
            x = xs_ref[c]
            m = jnp.max(x, axis=0, keepdims=True)
            idx = jnp.min(jnp.where(x == m, kiota, float(NK)), axis=0, keepdims=True)
            xs_ref[c] = jnp.where(kiota == idx, NEG, x)
            sv_ref[c, pl.ds(it, 1), :] = m
            si_ref[c, pl.ds(it, 1), :] = idx
        return carry

    lax.fori_loop(0, K, half_topk, 0)

    jiota = lambda rows: lax.broadcasted_iota(jnp.int32, (rows, TT), 0).astype(F32)
    off = 0
    for i in range(K // 2):
        n = K // (i + 1)
        rows = -(-n // 8) * 8
        ok = jiota(rows) < float(n)
        cand_ref[pl.ds(off, rows), :] = jnp.where(ok, sv_ref[0, i:i + 1, :] + sv_ref[1, 0:rows, :], NEG)
        cidx_ref[pl.ds(off, rows), :] = si_ref[0, i:i + 1, :] * float(NK) + si_ref[1, 0:rows, :]
        pf_ref[pl.ds(off, rows), :] = jnp.where(ok, float(i * K) + jiota(rows), float(K * K))
        off += rows
    cand_ref[pl.ds(off, K // 2), :] = sv_ref[0, K // 2:K, :] + sv_ref[1, 0:1, :]
    cidx_ref[pl.ds(off, K // 2), :] = si_ref[0, K // 2:K, :] * float(NK) + si_ref[1, 0:1, :]
    pf_ref[pl.ds(off, K // 2), :] = (float(K // 2) + jiota(K // 2)) * float(K)
    piota = pf_ref[...]

    def pair_topk(it, carry):
        x = cand_ref[...]
        m = jnp.max(x, axis=0, keepdims=True)
        pos = jnp.min(jnp.where(x == m, piota, float(K * K)), axis=0, keepdims=True)
        sel = piota == pos
        cand_ref[...] = jnp.where(sel, NEG, x)
        ts_ref[pl.ds(it, 1), :] = m
        e_ref[0, pl.ds(it, 1), :] = jnp.max(jnp.where(sel, cidx_ref[...], -1.0), axis=0,
                                            keepdims=True).astype(jnp.int32)
        return carry

    lax.fori_loop(0, K, pair_topk, 0)
    ts = ts_ref[...]
    ex = jnp.exp(ts - ts[0:1, :])
    g_ref[0] = ex / jnp.sum(ex, axis=0, keepdims=True)


def peer_route_topk(sT):
    H, _, NK, N = sT.shape
    TT, K = ROUTE_TT, PEER_TOPK
    out_spec = pl.BlockSpec((1, K, TT), lambda h, i: (h, 0, i))
    return pl.pallas_call(
        _peer_route_body,
        grid=(H, N // TT),
        in_specs=[pl.BlockSpec((1, 2, NK, TT), lambda h, i: (h, 0, 0, i))],
        out_specs=[out_spec, out_spec],
        out_shape=[jax.ShapeDtypeStruct((H, K, N), jnp.int32), jax.ShapeDtypeStruct((H, K, N), F32)],
        scratch_shapes=[pltpu.VMEM((2, NK, TT), F32), pltpu.VMEM((2, K, TT), F32), pltpu.VMEM((2, K, TT), F32),
                        pltpu.VMEM((ROUTE_PAIR_ROWS, TT), F32), pltpu.VMEM((ROUTE_PAIR_ROWS, TT), F32),
                        pltpu.VMEM((ROUTE_PAIR_ROWS, TT), F32), pltpu.VMEM((K, TT), F32)],
        compiler_params=pltpu.CompilerParams(dimension_semantics=("parallel", "parallel")),
        name="peer_route_topk",
    )(sT)


def peer_route(xf, p):
    N = xf.shape[0]
    wq, keys = p['peer_wq'], p['peer_keys'].astype(F32)
    q = (xf @ wq).astype(F32).reshape(N, PEER_HEADS, 2, PEER_DQ // 2)
    sT = jnp.einsum('thcq,hckq->hckt', q, keys)
    eT, gT = peer_route_topk(sT)
    return (eT.transpose(2, 0, 1).reshape(N, PEER_SEL), gT.transpose(2, 0, 1).reshape(N, PEER_SEL))


def peer_ffn(h, p):
    Bn, L, D = h.shape
    N = Bn * L
    xf = h.reshape(N, D)
    eidx, gate = peer_route(xf, p)
    u_pk, v_pk = p['peer_u_pk'], p['peer_v_pk']
    outs = []
    for c0 in range(0, N, PEER_CHUNK):
        rows = slice(c0, c0 + PEER_CHUNK)
        ug, vg = sc_gather_pair(u_pk, v_pk, eidx[rows].reshape(-1))
        outs.append(peer_apply(xf[rows], gate[rows], ug, vg))
    return jnp.concatenate(outs, axis=0).reshape(Bn, L, D)


def merge_groups(ys, g, dtype):
    outs, off = [], 0
    for y, w in zip(ys, GROUP_WIDTHS):
        outs.append(rmsnorm(y, g[off:off + w]).astype(dtype))
        off += w
    return jnp.concatenate(outs, axis=-1)


def token_mixers(h_c, h_l, p, need_ctx):
    zc = split_cols(h_c @ p['w_in'])
    zl = split_cols(h_l @ p['w_in'])
    a_c, a_l = mixer_rglru(zc[0:2], zl[0:2], p, need_ctx)
    b_c, b_l = mixer_rwkv(zc[2:8], zl[2:8], p, need_ctx)
    m_c, m_l = mixer_mlstm(zc[8:13], zl[8:13], p, need_ctx)
    d_l = mixer_hyena(zl[13:16], p)
    o_l = merge_groups([a_l, b_l, m_l, d_l], p['grp_g'], h_l.dtype) @ p['w_out']
    if not need_ctx:
        return None, o_l
    d_c = mixer_hyena(zc[13:16], p)
    o_c = merge_groups([a_c, b_c, m_c, d_c], p['grp_g'], h_c.dtype) @ p['w_out']
    return o_c, o_l


def trunk_layer(x_l, x_c, c, c_ctx, p, need_ctx):
    Bn = c.shape[0]
    mod_l = (jax.nn.silu(c) @ p['ada_w'] + p['ada_b']).reshape(Bn, 6, 1, D_MODEL)
    mod_c = (jax.nn.silu(c_ctx) @ p['ada_w'] + p['ada_b']).reshape(6, 1, 1, D_MODEL)
    h_l = modulate(rmsnorm(x_l, p['norm1_g']), mod_l[:, 0], mod_l[:, 1])
    h_c = modulate(rmsnorm(x_c, p['norm1_g']), mod_c[0], mod_c[1])
    o_c, o_l = token_mixers(h_c, h_l, p, need_ctx)
    x_l = x_l + mod_l[:, 2] * o_l
    x_l = x_l + mod_l[:, 5] * peer_ffn(modulate(rmsnorm(x_l, p['norm2_g']), mod_l[:, 3], mod_l[:, 4]), p)
    if need_ctx:
        x_c = x_c + mod_c[2] * o_c
        x_c = x_c + mod_c[5] * peer_ffn(modulate(rmsnorm(x_c, p['norm2_g']), mod_c[3], mod_c[4]), p)
    return x_l, x_c


def _final_norm_body(x_ref, g_ref, o_ref):
    xf = x_ref[...]
    y = xf * lax.rsqrt(jnp.mean(xf * xf, axis=-1, keepdims=True) + EPS)
    o_ref[...] = y * g_ref[...]


def final_rmsnorm(x, g):
    Bn, L, D = x.shape
    rows = Bn * L
    tile = 1024
    out = pl.pallas_call(
        _final_norm_body,
        grid=(rows // tile,),
        in_specs=[pl.BlockSpec((tile, D), lambda i: (i, 0)), pl.BlockSpec((1, D), lambda i: (0, 0))],
        out_specs=pl.BlockSpec((tile, D), lambda i: (i, 0)),
        out_shape=jax.ShapeDtypeStruct((rows, D), x.dtype),
        name="final_rmsnorm",
    )(x.reshape(rows, D), g.reshape(1, D))
    return out.reshape(Bn, L, D)


def kernel(x, c, ctx, c_ctx, ada_w, ada_b, norm1_g, norm2_g, w_in, w_out, grp_g,
           lru_conv_w, lru_conv_b, lru_wr, lru_br, lru_wi, lru_bi, lru_lam,
           rwkv_mu, rwkv_w0, rwkv_w2, rwkv_a0, rwkv_a2, rwkv_g2, rwkv_kk, rwkv_ka, rwkv_rk, rwkv_ln_g, rwkv_ln_b,
           mlstm_conv_w, mlstm_conv_b, mlstm_gate_b,
           hy_conv_w, hy_conv_b, hy_w1, hy_b1, hy_w2, hy_b2, hy_w3, hy_freq, hy_bias,
           peer_wq, peer_keys, peer_u, peer_v, final_g):
    x_l, x_c = x, ctx
    for i in range(DEPTH):
        p = dict(ada_w=ada_w[i], ada_b=ada_b[i], norm1_g=norm1_g[i], norm2_g=norm2_g[i], w_in=w_in[i],
                 w_out=w_out[i], grp_g=grp_g[i],
                 lru_conv_w=lru_conv_w[i], lru_conv_b=lru_conv_b[i], lru_wr=lru_wr[i], lru_br=lru_br[i],
                 lru_wi=lru_wi[i], lru_bi=lru_bi[i], lru_lam=lru_lam[i],
                 rwkv_mu=rwkv_mu[i], rwkv_w0=rwkv_w0[i], rwkv_w2=rwkv_w2[i], rwkv_a0=rwkv_a0[i],
                 rwkv_a2=rwkv_a2[i], rwkv_g2=rwkv_g2[i], rwkv_kk=rwkv_kk[i], rwkv_ka=rwkv_ka[i],
                 rwkv_rk=rwkv_rk[i], rwkv_ln_g=rwkv_ln_g[i], rwkv_ln_b=rwkv_ln_b[i],
                 mlstm_conv_w=mlstm_conv_w[i], mlstm_conv_b=mlstm_conv_b[i], mlstm_gate_b=mlstm_gate_b[i],
                 hy_conv_w=hy_conv_w[i], hy_conv_b=hy_conv_b[i], hy_w1=hy_w1[i], hy_b1=hy_b1[i],
                 hy_w2=hy_w2[i], hy_b2=hy_b2[i], hy_w3=hy_w3[i], hy_freq=hy_freq[i], hy_bias=hy_bias[i],
                 peer_wq=peer_wq[i], peer_keys=peer_keys[i],
                 peer_u_pk=pack_bf16_halves(peer_u[i]), peer_v_pk=pack_bf16_halves(peer_v[i]))
        x_l, x_c = trunk_layer(x_l, x_c, c, c_ctx, p, i < DEPTH - 1)
    return final_rmsnorm(x_l, final_g)
```

```python
import functools
import math
import jax, jax.numpy as jnp
from jax import lax
import numpy as np
from jax.experimental import pallas as pl
from jax.experimental.pallas import tpu as pltpu
from jax.experimental.pallas import tpu_sc as plsc

D_MODEL = 1024
BATCH = 16
SEQ = 2048
DEPTH = 2

GRID_W = 64
CTX_LEN = 256
F32 = jnp.float32
EPS = 1e-6
HEAD_DIM = 64
D_MIX = D_MODEL
W_LRU = D_MIX // 4
W_RWKV = D_MIX // 4
W_MLSTM = D_MIX // 4
W_HYENA = D_MIX - W_LRU - W_RWKV - W_MLSTM
GROUP_WIDTHS = (W_LRU, W_RWKV, W_MLSTM, W_HYENA)
H_LRU = W_LRU // HEAD_DIM
H_RWKV = W_RWKV // HEAD_DIM
H_MLSTM = W_MLSTM // HEAD_DIM
CONV_W = 4
LRU_C = 8.0
RWKV_LORA_W = 32
RWKV_LORA_A = 32
RWKV_LORA_G = 64
RWKV_GN_EPS = 64e-5
MLSTM_CHUNK = 64
HYENA_ORDER = 2
HYENA_SHORT = 3
HYENA_EMB = 33
HYENA_HID = 64
HYENA_TARGET = 1e-2
HYENA_FAST = 0.3
HYENA_SLOW = 1.5
PEER_HEADS = 8
PEER_NKEYS = 128
PEER_EXPERTS = PEER_NKEYS * PEER_NKEYS
PEER_TOPK = 16
PEER_DQ = 256
PEER_BLOCK = 128
IN_SPLITS = (W_LRU, W_LRU, W_RWKV, W_RWKV, W_RWKV, RWKV_LORA_W, RWKV_LORA_A, RWKV_LORA_G, W_MLSTM, W_MLSTM, W_MLSTM, W_MLSTM, 4 * H_MLSTM, W_HYENA, W_HYENA, W_HYENA)
D_IN = sum(IN_SPLITS)


def rmsnorm(x, g):
    xf = x.astype(F32)
    y = xf * lax.rsqrt(jnp.mean(xf * xf, axis=-1, keepdims=True) + EPS)
    return (y * g.astype(F32)).astype(x.dtype)


def modulate(h, shift, scale):
    return h * (1 + scale) + shift


def flip_if(a, d, axis=1):
    return jnp.flip(a, axis) if d == 1 else a


def split_cols(z):
    offs = np.cumsum(IN_SPLITS)[:-1].tolist()
    return jnp.split(z, offs, axis=-1)


def dwconv(x, w, b, pad_left):
    K = w.shape[0]
    y = lax.conv_general_dilated(x, w[:, None, :].astype(x.dtype), (1,), [(pad_left, K - 1 - pad_left)],
                                 dimension_numbers=('NWC', 'WIO', 'NWC'), feature_group_count=x.shape[-1])
    return y + b.astype(x.dtype)


def token_shift(x, mu):
    prev = jnp.pad(x, ((0, 0), (1, 0), (0, 0)))[:, :-1]
    nxt = jnp.pad(x, ((0, 0), (0, 1), (0, 0)))[:, 1:]
    return x + mu[0] * (prev - x) + mu[1] * (nxt - x)


def split_heads(t, h):
    Bn, L, W = t.shape
    return t.reshape(Bn, L, h, W // h)


def to_colmajor(a):
    Bn, L, C = a.shape
    rows = L // GRID_W
    return a.reshape(Bn, rows, GRID_W, C).transpose(0, 2, 1, 3).reshape(Bn, L, C)


def from_colmajor(a):
    Bn, L, C = a.shape
    rows = L // GRID_W
    return a.reshape(Bn, GRID_W, rows, C).transpose(0, 2, 1, 3).reshape(Bn, L, C)


LRU_TC = 256


def _expm1(z):
    u = jnp.exp(z)
    return jnp.where(u == 1.0, z, (u - 1.0) * z / jnp.where(u == 1.0, 1.0, jnp.log(u)))


def _lru_body(x_ref, h0_ref, wr_ref, wi_ref, br_ref, bi_ref, sp_ref, h_ref, hT_ref, st_ref, *, reverse):
    TC, W = x_ref.shape[1], x_ref.shape[2]

    @pl.when(pl.program_id(1) == 0)
    def _():
        st_ref[...] = h0_ref[0]

    x = x_ref[0]
    r = jax.nn.sigmoid(jnp.dot(x, wr_ref[...], preferred_element_type=F32) + br_ref[...])
    i = jax.nn.sigmoid(jnp.dot(x, wi_ref[...], preferred_element_type=F32) + bi_ref[...])
    log_a = -LRU_C * r * sp_ref[...]
    a = jnp.exp(log_a)
    b = jnp.sqrt(-_expm1(2.0 * log_a)) * (i * x)
    row = lax.broadcasted_iota(jnp.int32, (TC, W), 0)
    s = 1
    while s < TC:
        if reverse:
            keep = row < TC - s
            a_sh = jnp.where(keep, pltpu.roll(a, TC - s, 0), 1.0)
            b_sh = jnp.where(keep, pltpu.roll(b, TC - s, 0), 0.0)
        else:
            keep = row >= s
            a_sh = jnp.where(keep, pltpu.roll(a, s, 0), 1.0)
            b_sh = jnp.where(keep, pltpu.roll(b, s, 0), 0.0)
        b = b + a * b_sh
        a = a * a_sh
        s *= 2
    h = b + a * st_ref[...]
    h_ref[0] = h
    last = h[0:1, :] if reverse else h[TC - 1:TC, :]
    st_ref[...] = last
    hT_ref[0] = last


def lru_scan_dir(xc, h0, wr_bd, wi_bd, br, bi, sp, reverse):
    Bn, L, W = xc.shape
    TC = min(LRU_TC, L)
    n = L // TC
    tmap = (lambda b, c: (b, n - 1 - c, 0)) if reverse else (lambda b, c: (b, c, 0))
    wspec = pl.BlockSpec((W, W), lambda b, c: (0, 0))
    vspec = pl.BlockSpec((1, W), lambda b, c: (0, 0))
    sspec = pl.BlockSpec((1, 1, W), lambda b, c: (b, 0, 0))
    return pl.pallas_call(
        functools.partial(_lru_body, reverse=reverse),
        grid=(Bn, n),
        in_specs=[pl.BlockSpec((1, TC, W), tmap), sspec, wspec, wspec, vspec, vspec, vspec],
        out_specs=[pl.BlockSpec((1, TC, W), tmap), sspec],
        out_shape=[jax.ShapeDtypeStruct((Bn, L, W), F32), jax.ShapeDtypeStruct((Bn, 1, W), F32)],
        scratch_shapes=[pltpu.VMEM((1, W), F32)],
        compiler_params=pltpu.CompilerParams(dimension_semantics=("parallel", "arbitrary")),
        name="rglru_scan",
    )(xc, h0, wr_bd, wi_bd, br, bi, sp)


def head_block_diag(w):
    D2, H, N, _ = w.shape
    return jnp.einsum('dhij,hg->dhigj', w, jnp.eye(H, dtype=w.dtype)).reshape(D2, H * N, H * N)


def mixer_rglru(f_c, f_l, p, need_ctx):
    (x_c, g_c), (x_l, g_l) = f_c, f_l
    xc_c = dwconv(x_c, p['lru_conv_w'], p['lru_conv_b'], CONV_W // 2).astype(F32)
    xc_l = dwconv(x_l, p['lru_conv_w'], p['lru_conv_b'], CONV_W // 2).astype(F32)
    h0 = jnp.zeros((xc_l.shape[0], 1, W_LRU), F32)
    wr, wi = head_block_diag(p['lru_wr']), head_block_diag(p['lru_wi'])
    sp = jax.nn.softplus(-p['lru_lam'])
    hs_c, hs_l = [], []
    for d in range(2):
        gates = (wr[d], wi[d], p['lru_br'][d][None], p['lru_bi'][d][None], sp[d][None])
        hc, st = lru_scan_dir(xc_c, h0, *gates, reverse=(d == 1))
        hl, _ = lru_scan_dir(xc_l, st, *gates, reverse=(d == 1))
        hs_c.append(hc)
        hs_l.append(hl)
    y_l = jax.nn.gelu(g_l.astype(F32)) * (hs_l[0] + hs_l[1])
    y_c = jax.nn.gelu(g_c.astype(F32)) * (hs_c[0] + hs_c[1]) if need_ctx else None
    return y_c, y_l


def rwkv_shift(f, p):
    r, k, v, zw, za, zg = [t.astype(F32) for t in f]
    mu = p['rwkv_mu']
    return (token_shift(r, mu[0]), token_shift(k, mu[1]), token_shift(v, mu[2]), zw, za, zg)


def rwkv_dir_inputs(f, p, d):
    r, k, v, zw, za, _ = f
    w_log = -jax.nn.softplus(-(p['rwkv_w0'][d] + jnp.tanh(zw) @ p['rwkv_w2'][d])) - 0.5
    decay = jnp.exp(-jnp.exp(w_log))
    a = jax.nn.sigmoid(p['rwkv_a0'][d] + za @ p['rwkv_a2'][d])
    kk = split_heads(k * p['rwkv_kk'], H_RWKV)
    kk = kk / jnp.maximum(jnp.linalg.norm(kk, axis=-1, keepdims=True), 1e-12)
    kd = k * (1 + (a - 1) * p['rwkv_ka'])
    return [split_heads(r, H_RWKV), split_heads(decay, H_RWKV), split_heads(kd, H_RWKV),
            split_heads(v, H_RWKV), kk, kk * split_heads(a, H_RWKV)]


RWKV_TC = HEAD_DIM
RWKV_PAIRS = 32


def _rwkv_body(kk_ref, v_ref, r_ref, w_ref, kka_ref, k_ref, y_ref, s_ref):
    NP, N = RWKV_PAIRS, HEAD_DIM
    rev = pl.program_id(0)

    @pl.when(pl.program_id(1) == 0)
    def _():
        s_ref[...] = jnp.zeros_like(s_ref)

    y_ref[...] = jnp.zeros_like(y_ref)
    lane = lax.broadcasted_iota(jnp.int32, (N, 2 * N), 1)
    row = lax.broadcasted_iota(jnp.int32, (N, 2 * N), 0)
    diag = (lane % N) == row
    seg4 = lax.broadcasted_iota(jnp.int32, (4 * N, 4 * N), 0) // N == \
        lax.broadcasted_iota(jnp.int32, (4 * N, 4 * N), 1) // N
    ones4 = jnp.where(seg4, 1.0, 0.0).astype(jnp.bfloat16)

    def segsum_bcast(x):
        H = x.shape[0] // 2
        xb = x.astype(jnp.bfloat16)
        lhs = jnp.concatenate([xb[:H].reshape(H * N, 2 * N), xb[H:].reshape(H * N, 2 * N)], axis=-1)
        out = jnp.dot(lhs, ones4, preferred_element_type=F32)
        return jnp.concatenate([out[:, :2 * N].reshape(H, N, 2 * N), out[:, 2 * N:].reshape(H, N, 2 * N)], axis=0)

    def rowv(ref, t):
        x = ref[:, pl.ds(t, 1), :] if len(ref.shape) == 3 else ref[0, :, pl.ds(t, 1), :]
        return jnp.concatenate([x[:, :, :2 * N], x[:, :, 2 * N:]], axis=0)

    def step(i, carry):
        t = i + rev * (RWKV_TC - 1 - 2 * i)
        tp = jnp.clip(t - 1 + 2 * rev, 0, RWKV_TC - 1)
        S = s_ref[...]
        R = segsum_bcast(jnp.concatenate([S * rowv(kk_ref, t), jnp.where(diag, rowv(v_ref, t), 0.0),
                                          S * rowv(r_ref, tp)], axis=0))
        y_ref[0, :, 0] = jnp.where(((lane % N) == tp) & (i > 0), R[2 * NP:], y_ref[0, :, 0])
        s_ref[...] = S * rowv(w_ref, t) - R[:NP] * rowv(kka_ref, t) + R[NP:2 * NP] * rowv(k_ref, t)
        return carry

    lax.fori_loop(0, RWKV_TC, step, 0)
    t_last = (RWKV_TC - 1) * (1 - rev)
    Y = segsum_bcast(s_ref[...] * rowv(r_ref, t_last))
    y_ref[0, :, 0] = jnp.where((lane % N) == t_last, Y, y_ref[0, :, 0])


def rwkv_scan_bidir(kk, v, r, w, kka, k, n_ctx_blocks):
    Bn, L, C = kk.shape
    TC, N = RWKV_TC, HEAD_DIM
    n = L // TC
    assert RWKV_PAIRS == Bn * C // (2 * N)

    def blk(d, c):
        back = jnp.where(c < n_ctx_blocks, n_ctx_blocks - 1 - c, n + n_ctx_blocks - 1 - c)
        return jnp.where(d == 0, c, back)

    shared = pl.BlockSpec((Bn, TC, C), lambda d, c: (0, blk(d, c), 0))
    per_dir = pl.BlockSpec((1, Bn, TC, C), lambda d, c: (d, 0, blk(d, c), 0))
    return pl.pallas_call(
        _rwkv_body,
        grid=(2, n),
        in_specs=[shared, shared, shared, per_dir, per_dir, per_dir],
        out_specs=pl.BlockSpec((1, RWKV_PAIRS, 1, N, 2 * N), lambda d, c: (d, 0, blk(d, c), 0, 0)),
        out_shape=jax.ShapeDtypeStruct((2, RWKV_PAIRS, n, N, 2 * N), F32),
        scratch_shapes=[pltpu.VMEM((RWKV_PAIRS, N, 2 * N), F32)],
        compiler_params=pltpu.CompilerParams(dimension_semantics=("parallel", "arbitrary"),
                                             vmem_limit_bytes=40 * 1024 * 1024),
        name="rwkv7_scan",
    )(kk, v, r, w, kka, k)


def rwkv7_scan_bidir(ins_c, ins_l):
    Bn, Lc = ins_c[0][0].shape[:2]
    Ll = ins_l[0][0].shape[1]
    L = Lc + Ll
    cat = lambda d, j: jnp.concatenate([ins_c[d][j], ins_l[d][j]], axis=1).reshape(Bn, L, W_RWKV)
    per_dir = lambda j: jnp.stack([cat(0, j), cat(1, j)])
    yT = rwkv_scan_bidir(cat(0, 4), cat(0, 3), cat(0, 0), per_dir(1), per_dir(5), per_dir(2), Lc // RWKV_TC)
    halves = W_RWKV // (2 * HEAD_DIM)
    y = yT.reshape(2, halves, Bn, L // HEAD_DIM, HEAD_DIM, 2, HEAD_DIM)
    y = y.transpose(0, 2, 3, 6, 1, 5, 4).reshape(2, Bn, L, H_RWKV, HEAD_DIM)
    return ([y[d, :, :Lc] for d in range(2)], [y[d, :, Lc:] for d in range(2)])


def rwkv_bonus(ins, p):
    r, _, kd, v = ins[:4]
    return jnp.sum(r * kd * p['rwkv_rk'], axis=-1, keepdims=True) * v


def rwkv_out(y, bonus, zg, p):
    Bn, L, H, N = y.shape
    mu = jnp.mean(y, axis=-1, keepdims=True)
    var = jnp.mean(jnp.square(y - mu), axis=-1, keepdims=True)
    yn = ((y - mu) * lax.rsqrt(var + RWKV_GN_EPS)).reshape(Bn, L, H * N) * p['rwkv_ln_g'] + p['rwkv_ln_b']
    g = jax.nn.sigmoid(zg) @ p['rwkv_g2']
    return (yn + bonus.reshape(Bn, L, H * N)) * g


def mixer_rwkv(f_c, f_l, p, need_ctx):
    fc, fl = rwkv_shift(f_c, p), rwkv_shift(f_l, p)
    ins_c = [rwkv_dir_inputs(fc, p, d) for d in range(2)]
    ins_l = [rwkv_dir_inputs(fl, p, d) for d in range(2)]
    o_c, o_l = rwkv7_scan_bidir(ins_c, ins_l)
    y_c, y_l = [], []
    for d in range(2):
        y_l.append((o_l[d], rwkv_bonus(ins_l[d], p)))
        if need_ctx:
            y_c.append((o_c[d], rwkv_bonus(ins_c[d], p)))
    out_l = rwkv_out(y_l[0][0] + y_l[1][0], y_l[0][1] + y_l[1][1], fl[5], p)
    out_c = rwkv_out(y_c[0][0] + y_c[1][0], y_c[0][1] + y_c[1][1], fc[5], p) if need_ctx else None
    return out_c, out_l


def mlstm_prep(f, p):
    q, k, v, o, gz = f
    Bn, L, _ = q.shape
    qk = jax.nn.silu(dwconv(jnp.concatenate([q, k], axis=-1), p['mlstm_conv_w'], p['mlstm_conv_b'], CONV_W // 2)).astype(F32)
    q, k = jnp.split(qk, 2, axis=-1)
    gates = gz.astype(F32).reshape(Bn, L, 2, 2, H_MLSTM) + p['mlstm_gate_b']
    return q * HEAD_DIM ** -0.5, k, v.astype(F32), o.astype(F32), gates


MLSTM_ROWS = 2


def _mlstm_body(q_ref, k_ref, v_ref, gc_ref, gr_ref, h_ref, c_ref, n_ref, m_ref):
    T, N, H = MLSTM_CHUNK, HEAD_DIM, H_MLSTM
    rev = pl.program_id(0)

    @pl.when(pl.program_id(2) == 0)
    def _():
        c_ref[...] = jnp.zeros_like(c_ref)
        n_ref[...] = jnp.zeros_like(n_ref)
        m_ref[...] = jnp.zeros_like(m_ref)

    ti = lax.broadcasted_iota(jnp.int32, (T, T), 0)
    si = lax.broadcasted_iota(jnp.int32, (T, T), 1)
    mask = (si - ti) * (1 - 2 * rev) <= 0
    tri = jnp.where(mask, 1.0, 0.0)
    hp = lax.Precision.HIGHEST
    gcs = [gc_ref[0, r, 0] for r in range(MLSTM_ROWS)]
    grs = [gr_ref[0, r, 0] for r in range(MLSTM_ROWS)]
    b_cols = [jnp.dot(tri, g[:, H:], precision=hp, preferred_element_type=F32) for g in gcs]
    b_rows = [lax.dot_general(g[H:], tri, (((1,), (1,)), ((), ())), precision=hp, preferred_element_type=F32)
              for g in grs]
    b_tots = [jnp.sum(g[:, H:], axis=0, keepdims=True) for g in gcs]
    units = [(r, h) for r in range(MLSTM_ROWS) for h in range(H)]
    hs = range(len(units))
    sl = [slice(h * N, (h + 1) * N) for _, h in units]
    qh = [q_ref[units[h][0], :, sl[h]] for h in hs]
    kh = [k_ref[units[h][0], :, sl[h]] for h in hs]
    vh = [v_ref[units[h][0], :, sl[h]] for h in hs]
    C = [c_ref[h] for h in hs]
    n = [n_ref[h] for h in hs]
    m = [m_ref[h][:, 0:1] for h in hs]
    bc = [b_cols[r][:, h:h + 1] for r, h in units]
    br = [b_rows[r][h:h + 1, :] for r, h in units]
    ic = [gcs[r][:, h:h + 1] for r, h in units]
    ir = [grs[r][h:h + 1, :] for r, h in units]
    bT = [b_tots[r][:, h:h + 1] for r, h in units]
    nt = (((1,), (1,)), ((), ()))
    qk = [lax.dot_general(qh[h], kh[h], nt, preferred_element_type=F32) for h in hs]
    qc = [lax.dot_general(qh[h], C[h], nt, preferred_element_type=F32) for h in hs]
    logd = [jnp.where(mask, bc[h] - br[h] + ir[h], -jnp.inf) for h in hs]
    inter = [bc[h] + m[h] for h in hs]
    mt = [jnp.maximum(inter[h], jnp.max(logd[h], axis=-1, keepdims=True)) for h in hs]
    s = [qk[h] * jnp.exp(logd[h] - mt[h]) for h in hs]
    e_inter = [jnp.exp(inter[h] - mt[h]) for h in hs]
    num = [jnp.dot(s[h], vh[h], preferred_element_type=F32) + e_inter[h] * qc[h] for h in hs]
    den = [jnp.sum(s[h], axis=-1, keepdims=True) + e_inter[h] * jnp.sum(qh[h] * n[h], axis=-1, keepdims=True)
           for h in hs]
    out = [num[h] / jnp.maximum(jnp.abs(den[h]), jnp.exp(-mt[h])) for h in hs]
    for r in range(MLSTM_ROWS):
        h_ref[0, r] = jnp.concatenate(out[r * H:(r + 1) * H], axis=-1)
    m_new = [jnp.maximum(bT[h] + m[h], jnp.max(bT[h] - br[h] + ir[h], axis=-1, keepdims=True)) for h in hs]
    w_col = [jnp.exp(bT[h] - bc[h] + ic[h] - m_new[h]) for h in hs]
    dec = [jnp.exp(bT[h] + m[h] - m_new[h]) for h in hs]
    vk = [lax.dot_general(vh[h] * w_col[h], kh[h], (((0,), (0,)), ((), ())), preferred_element_type=F32) for h in hs]
    for h in hs:
        c_ref[h] = dec[h] * C[h] + vk[h]
        n_ref[h] = dec[h] * n[h] + jnp.sum(kh[h] * w_col[h], axis=0, keepdims=True)
        m_ref[h] = jnp.broadcast_to(m_new[h], (1, 128))


def mlstm_scan_bidir(q, k, v, gcol, grow, n_ctx_blocks):
    Bn, L, W = q.shape
    T = MLSTM_CHUNK
    n = L // T

    def blk(d, c):
        back = jnp.where(c < n_ctx_blocks, n_ctx_blocks - 1 - c, n + n_ctx_blocks - 1 - c)
        return jnp.where(d == 0, c, back)

    R = MLSTM_ROWS
    xspec = pl.BlockSpec((R, T, W), lambda d, b, c: (b, blk(d, c), 0))
    gcs = pl.BlockSpec((1, R, 1, T, 2 * H_MLSTM), lambda d, b, c: (d, b, blk(d, c), 0, 0))
    grs = pl.BlockSpec((1, R, 1, 2 * H_MLSTM, T), lambda d, b, c: (d, b, blk(d, c), 0, 0))
    return pl.pallas_call(
        _mlstm_body,
        grid=(2, Bn // R, n),
        in_specs=[xspec, xspec, xspec, gcs, grs],
        out_specs=pl.BlockSpec((1, R, T, W), lambda d, b, c: (d, b, blk(d, c), 0)),
        out_shape=jax.ShapeDtypeStruct((2, Bn, L, W), F32),
        scratch_shapes=[pltpu.VMEM((R * H_MLSTM, HEAD_DIM, HEAD_DIM), F32), pltpu.VMEM((R * H_MLSTM, 1, HEAD_DIM), F32),
                        pltpu.VMEM((R * H_MLSTM, 1, 128), F32)],
        compiler_params=pltpu.CompilerParams(dimension_semantics=("parallel", "parallel", "arbitrary")),
        name="mlstm_chunkwise",
    )(q, k, v, gcol, grow)


def mixer_mlstm(f_c, f_l, p, need_ctx):
    f_l = [to_colmajor(t) for t in f_l]
    qc, kc, vc, oc, gc = mlstm_prep(f_c, p)
    ql, kl, vl, ol, gl = mlstm_prep(f_l, p)
    Bn, Lc = qc.shape[:2]
    cat = lambda a, b: jnp.concatenate([a, b], axis=1)
    g = cat(gc, gl)
    L = g.shape[1]
    g = jnp.concatenate([g[:, :, :, 0], jax.nn.log_sigmoid(g[:, :, :, 1])], axis=-1)
    gcol = g.transpose(2, 0, 1, 3).reshape(2, Bn, L // MLSTM_CHUNK, MLSTM_CHUNK, 2 * H_MLSTM)
    h = mlstm_scan_bidir(cat(qc, ql), cat(kc, kl), cat(vc, vl), gcol, gcol.transpose(0, 1, 2, 4, 3),
                         Lc // MLSTM_CHUNK)
    hs = h[0] + h[1]
    y_l = from_colmajor(jax.nn.sigmoid(ol) * hs[:, Lc:])
    y_c = jax.nn.sigmoid(oc) * hs[:, :Lc] if need_ctx else None
    return y_c, y_l


def hyena_spectrum(L, p):
    pos = jnp.arange(L, dtype=F32)
    t = pos / (L - 1)
    bands = (HYENA_EMB - 1) // 2
    freqs = jnp.linspace(1e-4, bands - 1, bands, dtype=F32)
    ang = (2 * math.pi / L) * pos[:, None] * freqs[None, :]
    z = jnp.concatenate([t[:, None], jnp.cos(ang), -jnp.sin(ang)], axis=-1)
    h = jnp.sin(p['hy_freq'][0] * (z @ p['hy_w1'] + p['hy_b1']))
    h = jnp.sin(p['hy_freq'][1] * (h @ p['hy_w2'] + p['hy_b2']))
    h = (h @ p['hy_w3']).astype(F32).reshape(L, HYENA_ORDER, 2, W_HYENA)
    deltas = jnp.abs(jnp.linspace(math.log(HYENA_TARGET) / HYENA_SLOW, math.log(HYENA_TARGET) / HYENA_FAST,
                                  W_HYENA, dtype=F32))
    h = h * jnp.exp(-t[:, None, None, None] * deltas)
    fwd, bwd = h[:, :, 0], h[:, :, 1]
    two = jnp.concatenate([fwd, jnp.zeros_like(fwd[:1]), jnp.flip(bwd[1:], axis=0)], axis=0)
    two = two / (jnp.sum(jnp.abs(two), axis=0, keepdims=True) + EPS)
    return jnp.fft.rfft(two, axis=0)


def long_conv(u, spec, bias):
    L = u.shape[1]
    y = jnp.fft.irfft(jnp.fft.rfft(u, n=2 * L, axis=1) * spec, n=2 * L, axis=1)[:, :L]
    return y + u * bias


def mixer_hyena(f, p):
    u = dwconv(jnp.concatenate(f, axis=-1), p['hy_conv_w'], p['hy_conv_b'], HYENA_SHORT // 2).astype(F32)
    v, x1, x2 = jnp.split(u, 3, axis=-1)
    spec = hyena_spectrum(u.shape[1], p)
    z = x1 * long_conv(v, spec[:, 0], p['hy_bias'][0])
    return x2 * long_conv(z, spec[:, 1], p['hy_bias'][1])


PEER_SEL = PEER_HEADS * PEER_TOPK
PEER_HALF = D_MODEL // 2
PEER_CHUNK = 4096
PEER_FIRST_CHUNK = 1024
PEER_TT = 16
SC_GATHER_ROWS = 64


def pack_bf16_halves(tab):
    bits = lax.bitcast_convert_type(tab.astype(jnp.bfloat16), jnp.uint16).astype(jnp.uint32)
    half = tab.shape[1] // 2
    return bits[:, :half] | (bits[:, half:] << 16)


def sc_gather_pair(u_tab, v_tab, idx):
    n = idx.shape[0]
    W = u_tab.shape[1]
    info = plsc.get_sparse_core_info()
    nc, ns = info.num_cores, info.num_subcores
    per_w = n // (nc * ns)
    G = SC_GATHER_ROWS
    assert per_w * nc * ns == n and per_w % G == 0
    nsteps = per_w // G
    mesh = plsc.VectorSubcoreMesh(core_axis_name="c", subcore_axis_name="s")
    out = jax.ShapeDtypeStruct((n, W), u_tab.dtype)

    def body(u_hbm, v_hbm, idx_hbm, uo_hbm, vo_hbm, idx_v, urows, vrows, usem, vsem):
        wid = lax.axis_index("s") * nc + lax.axis_index("c")
        base = wid * per_w
        pltpu.sync_copy(idx_hbm.at[pl.ds(base, per_w)], idx_v)

        @pl.loop(0, nsteps)
        def _(i):
            off = pl.multiple_of(i * G, G)
            ids = idx_v.at[pl.ds(off, G)]
            cu = pltpu.async_copy(u_hbm.at[ids], urows, usem)
            cv = pltpu.async_copy(v_hbm.at[ids], vrows, vsem)
            cu.wait()
            pltpu.sync_copy(urows, uo_hbm.at[pl.ds(base + off, G)])
            cv.wait()
            pltpu.sync_copy(vrows, vo_hbm.at[pl.ds(base + off, G)])

    fn = pl.kernel(body, out_type=(out, out), mesh=mesh,
                   scratch_types=[pltpu.VMEM((per_w,), jnp.int32),
                                  pltpu.VMEM((G, W), u_tab.dtype), pltpu.VMEM((G, W), u_tab.dtype),
                                  pltpu.SemaphoreType.DMA, pltpu.SemaphoreType.DMA],
                   name="peer_sc_gather")
    return fn(u_tab, v_tab, idx)


def _unpack_halves(w):
    lo = lax.bitcast_convert_type(w << 16, F32)
    hi = lax.bitcast_convert_type(w & jnp.uint32(0xFFFF0000), F32)
    return lo, hi


def _peer_apply_body(x_ref, g_ref, ug_ref, vg_ref, o_ref):
    TT = x_ref.shape[0]
    gpad = jnp.concatenate([g_ref[...], jnp.zeros((PEER_SEL - TT, PEER_SEL), F32)], axis=0)
    gT = gpad.T
    for t in range(TT):
        rows = pl.ds(t * PEER_SEL, PEER_SEL)
        ulo, uhi = _unpack_halves(ug_ref[rows, :])
        xl = x_ref[pl.ds(t, 1), 0:PEER_HALF]
        xh = x_ref[pl.ds(t, 1), PEER_HALF:D_MODEL]
        dots = jnp.sum(ulo * xl + uhi * xh, axis=-1, keepdims=True)
        w = jax.nn.gelu(dots) * gT[:, t:t + 1]
        vlo, vhi = _unpack_halves(vg_ref[rows, :])
        o_ref[pl.ds(t, 1), 0:PEER_HALF] = jnp.sum(vlo * w, axis=0, keepdims=True)
        o_ref[pl.ds(t, 1), PEER_HALF:D_MODEL] = jnp.sum(vhi * w, axis=0, keepdims=True)


def peer_apply(x, gate, ug, vg):
    T = x.shape[0]
    TT = PEER_TT
    return pl.pallas_call(
        _peer_apply_body,
        grid=(T // TT,),
        in_specs=[pl.BlockSpec((TT, D_MODEL), lambda i: (i, 0)),
                  pl.BlockSpec((TT, PEER_SEL), lambda i: (i, 0)),
                  pl.BlockSpec((TT * PEER_SEL, PEER_HALF), lambda i: (i, 0)),
                  pl.BlockSpec((TT * PEER_SEL, PEER_HALF), lambda i: (i, 0))],
        out_specs=pl.BlockSpec((TT, D_MODEL), lambda i: (i, 0)),
        out_shape=jax.ShapeDtypeStruct((T, D_MODEL), F32),
        compiler_params=pltpu.CompilerParams(dimension_semantics=("parallel",),
                                             vmem_limit_bytes=40 * 1024 * 1024),
        name="peer_apply",
    )(x, gate, ug, vg)


ROUTE_TT = 1024
ROUTE_PAIR_ROWS = sum(-(-(PEER_TOPK // (i + 1)) // 8) * 8 for i in range(PEER_TOPK // 2)) + PEER_TOPK // 2


def _peer_route_body(s_ref, e_ref, g_ref, xs_ref, sv_ref, si_ref, cand_ref, cidx_ref, pf_ref, ts_ref):
    K, NK, TT = PEER_TOPK, PEER_NKEYS, s_ref.shape[-1]
    NEG = -jnp.inf
    xs_ref[...] = s_ref[0]
    kiota = lax.broadcasted_iota(jnp.int32, (NK, TT), 0).astype(F32)

    def half_topk(it, carry):
        for c in range(2):
            x = xs_ref[c]
            m = jnp.max(x, axis=0, keepdims=True)
            idx = jnp.min(jnp.where(x == m, kiota, float(NK)), axis=0, keepdims=True)
            xs_ref[c] = jnp.where(kiota == idx, NEG, x)
            sv_ref[c, pl.ds(it, 1), :] = m
            si_ref[c, pl.ds(it, 1), :] = idx
        return carry

    lax.fori_loop(0, K, half_topk, 0)

    jiota = lambda rows: lax.broadcasted_iota(jnp.int32, (rows, TT), 0).astype(F32)
    off = 0
    for i in range(K // 2):
        n = K // (i + 1)
        rows = -(-n // 8) * 8
        ok = jiota(rows) < float(n)
        cand_ref[pl.ds(off, rows), :] = jnp.where(ok, sv_ref[0, i:i + 1, :] + sv_ref[1, 0:rows, :], NEG)
        cidx_ref[pl.ds(off, rows), :] = si_ref[0, i:i + 1, :] * float(NK) + si_ref[1, 0:rows, :]
        pf_ref[pl.ds(off, rows), :] = jnp.where(ok, float(i * K) + jiota(rows), float(K * K))
        off += rows
    cand_ref[pl.ds(off, K // 2), :] = sv_ref[0, K // 2:K, :] + sv_ref[1, 0:1, :]
    cidx_ref[pl.ds(off, K // 2), :] = si_ref[0, K // 2:K, :] * float(NK) + si_ref[1, 0:1, :]
    pf_ref[pl.ds(off, K // 2), :] = (float(K // 2) + jiota(K // 2)) * float(K)
    piota = pf_ref[...]

    def pair_topk(it, carry):
        x = cand_ref[...]
        m = jnp.max(x, axis=0, keepdims=True)
        pos = jnp.min(jnp.where(x == m, piota, float(K * K)), axis=0, keepdims=True)
        sel = piota == pos
        cand_ref[...] = jnp.where(sel, NEG, x)
        ts_ref[pl.ds(it, 1), :] = m
        e_ref[0, pl.ds(it, 1), :] = jnp.max(jnp.where(sel, cidx_ref[...], -1.0), axis=0,
                                            keepdims=True).astype(jnp.int32)
        return carry

    lax.fori_loop(0, K, pair_topk, 0)
    ts = ts_ref[...]
    ex = jnp.exp(ts - ts[0:1, :])
    g_ref[0] = ex / jnp.sum(ex, axis=0, keepdims=True)


def peer_route_topk(sT):
    H, _, NK, N = sT.shape
    TT, K = ROUTE_TT, PEER_TOPK
    out_spec = pl.BlockSpec((1, K, TT), lambda h, i: (h, 0, i))
    return pl.pallas_call(
        _peer_route_body,
        grid=(H, N // TT),
        in_specs=[pl.BlockSpec((1, 2, NK, TT), lambda h, i: (h, 0, 0, i))],
        out_specs=[out_spec, out_spec],
        out_shape=[jax.ShapeDtypeStruct((H, K, N), jnp.int32), jax.ShapeDtypeStruct((H, K, N), F32)],
        scratch_shapes=[pltpu.VMEM((2, NK, TT), F32), pltpu.VMEM((2, K, TT), F32), pltpu.VMEM((2, K, TT), F32),
                        pltpu.VMEM((ROUTE_PAIR_ROWS, TT), F32), pltpu.VMEM((ROUTE_PAIR_ROWS, TT), F32),
                        pltpu.VMEM((ROUTE_PAIR_ROWS, TT), F32), pltpu.VMEM((K, TT), F32)],
        compiler_params=pltpu.CompilerParams(dimension_semantics=("parallel", "parallel")),
        name="peer_route_topk",
    )(sT)


def peer_route(xf, p):
    N = xf.shape[0]
    wq, keys = p['peer_wq'], p['peer_keys'].astype(F32)
    q = (xf @ wq).astype(F32).reshape(N, PEER_HEADS, 2, PEER_DQ // 2)
    sT = jnp.einsum('thcq,hckq->hckt', q, keys)
    eT, gT = peer_route_topk(sT)
    return (eT.transpose(2, 0, 1).reshape(N, PEER_SEL), gT.transpose(2, 0, 1).reshape(N, PEER_SEL))


def peer_ffn(h, p):
    Bn, L, D = h.shape
    N = Bn * L
    xf = h.reshape(N, D)
    eidx, gate = peer_route(xf, p)
    u_pk, v_pk = p['peer_u_pk'], p['peer_v_pk']
    outs, c0 = [], 0
    while c0 < N:
        rows = slice(c0, min(N, c0 + (PEER_FIRST_CHUNK if c0 == 0 else PEER_CHUNK)))
        ug, vg = sc_gather_pair(u_pk, v_pk, eidx[rows].reshape(-1))
        outs.append(peer_apply(xf[rows], gate[rows], ug, vg))
        c0 = rows.stop
    return jnp.concatenate(outs, axis=0).reshape(Bn, L, D)


def merge_groups(ys, g, dtype):
    outs, off = [], 0
    for y, w in zip(ys, GROUP_WIDTHS):
        outs.append(rmsnorm(y, g[off:off + w]).astype(dtype))
        off += w
    return jnp.concatenate(outs, axis=-1)


def token_mixers(h_c, h_l, p, need_ctx):
    zc = split_cols(h_c @ p['w_in'])
    zl = split_cols(h_l @ p['w_in'])
    a_c, a_l = mixer_rglru(zc[0:2], zl[0:2], p, need_ctx)
    b_c, b_l = mixer_rwkv(zc[2:8], zl[2:8], p, need_ctx)
    m_c, m_l = mixer_mlstm(zc[8:13], zl[8:13], p, need_ctx)
    d_l = mixer_hyena(zl[13:16], p)
    o_l = merge_groups([a_l, b_l, m_l, d_l], p['grp_g'], h_l.dtype) @ p['w_out']
    if not need_ctx:
        return None, o_l
    d_c = mixer_hyena(zc[13:16], p)
    o_c = merge_groups([a_c, b_c, m_c, d_c], p['grp_g'], h_c.dtype) @ p['w_out']
    return o_c, o_l


def trunk_layer(x_l, x_c, c, c_ctx, p, need_ctx):
    Bn = c.shape[0]
    mod_l = (jax.nn.silu(c) @ p['ada_w'] + p['ada_b']).reshape(Bn, 6, 1, D_MODEL)
    mod_c = (jax.nn.silu(c_ctx) @ p['ada_w'] + p['ada_b']).reshape(6, 1, 1, D_MODEL)
    h_l = modulate(rmsnorm(x_l, p['norm1_g']), mod_l[:, 0], mod_l[:, 1])
    h_c = modulate(rmsnorm(x_c, p['norm1_g']), mod_c[0], mod_c[1])
    o_c, o_l = token_mixers(h_c, h_l, p, need_ctx)
    x_l = x_l + mod_l[:, 2] * o_l
    x_l = x_l + mod_l[:, 5] * peer_ffn(modulate(rmsnorm(x_l, p['norm2_g']), mod_l[:, 3], mod_l[:, 4]), p)
    if need_ctx:
        x_c = x_c + mod_c[2] * o_c
        x_c = x_c + mod_c[5] * peer_ffn(modulate(rmsnorm(x_c, p['norm2_g']), mod_c[3], mod_c[4]), p)
    return x_l, x_c


def _final_norm_body(x_ref, g_ref, o_ref):
    xf = x_ref[...]
    y = xf * lax.rsqrt(jnp.mean(xf * xf, axis=-1, keepdims=True) + EPS)
    o_ref[...] = y * g_ref[...]


def final_rmsnorm(x, g):
    Bn, L, D = x.shape
    rows = Bn * L
    tile = 1024
    out = pl.pallas_call(
        _final_norm_body,
        grid=(rows // tile,),
        in_specs=[pl.BlockSpec((tile, D), lambda i: (i, 0)), pl.BlockSpec((1, D), lambda i: (0, 0))],
        out_specs=pl.BlockSpec((tile, D), lambda i: (i, 0)),
        out_shape=jax.ShapeDtypeStruct((rows, D), x.dtype),
        name="final_rmsnorm",
    )(x.reshape(rows, D), g.reshape(1, D))
    return out.reshape(Bn, L, D)


def kernel(x, c, ctx, c_ctx, ada_w, ada_b, norm1_g, norm2_g, w_in, w_out, grp_g,
           lru_conv_w, lru_conv_b, lru_wr, lru_br, lru_wi, lru_bi, lru_lam,
           rwkv_mu, rwkv_w0, rwkv_w2, rwkv_a0, rwkv_a2, rwkv_g2, rwkv_kk, rwkv_ka, rwkv_rk, rwkv_ln_g, rwkv_ln_b,
           mlstm_conv_w, mlstm_conv_b, mlstm_gate_b,
           hy_conv_w, hy_conv_b, hy_w1, hy_b1, hy_w2, hy_b2, hy_w3, hy_freq, hy_bias,
           peer_wq, peer_keys, peer_u, peer_v, final_g):
    x_l, x_c = x, ctx
    for i in range(DEPTH):
        p = dict(ada_w=ada_w[i], ada_b=ada_b[i], norm1_g=norm1_g[i], norm2_g=norm2_g[i], w_in=w_in[i],
                 w_out=w_out[i], grp_g=grp_g[i],
                 lru_conv_w=lru_conv_w[i], lru_conv_b=lru_conv_b[i], lru_wr=lru_wr[i], lru_br=lru_br[i],
                 lru_wi=lru_wi[i], lru_bi=lru_bi[i], lru_lam=lru_lam[i],
                 rwkv_mu=rwkv_mu[i], rwkv_w0=rwkv_w0[i], rwkv_w2=rwkv_w2[i], rwkv_a0=rwkv_a0[i],
                 rwkv_a2=rwkv_a2[i], rwkv_g2=rwkv_g2[i], rwkv_kk=rwkv_kk[i], rwkv_ka=rwkv_ka[i],
                 rwkv_rk=rwkv_rk[i], rwkv_ln_g=rwkv_ln_g[i], rwkv_ln_b=rwkv_ln_b[i],
                 mlstm_conv_w=mlstm_conv_w[i], mlstm_conv_b=mlstm_conv_b[i], mlstm_gate_b=mlstm_gate_b[i],
                 hy_conv_w=hy_conv_w[i], hy_conv_b=hy_conv_b[i], hy_w1=hy_w1[i], hy_b1=hy_b1[i],
                 hy_w2=hy_w2[i], hy_b2=hy_b2[i], hy_w3=hy_w3[i], hy_freq=hy_freq[i], hy_bias=hy_bias[i],
                 peer_wq=peer_wq[i], peer_keys=peer_keys[i],
                 peer_u_pk=pack_bf16_halves(peer_u[i]), peer_v_pk=pack_bf16_halves(peer_v[i]))
        x_l, x_c = trunk_layer(x_l, x_c, c, c_ctx, p, i < DEPTH - 1)
    return final_rmsnorm(x_l, final_g)
```

```python
import functools
import math
import jax, jax.numpy as jnp
from jax import lax
import numpy as np
from jax.experimental import pallas as pl
from jax.experimental.pallas import tpu as pltpu
from jax.experimental.pallas import tpu_sc as plsc

D_MODEL = 1024
BATCH = 16
SEQ = 2048
DEPTH = 2

GRID_W = 64
CTX_LEN = 256
F32 = jnp.float32
EPS = 1e-6
HEAD_DIM = 64
D_MIX = D_MODEL
W_LRU = D_MIX // 4
W_RWKV = D_MIX // 4
W_MLSTM = D_MIX // 4
W_HYENA = D_MIX - W_LRU - W_RWKV - W_MLSTM
GROUP_WIDTHS = (W_LRU, W_RWKV, W_MLSTM, W_HYENA)
H_LRU = W_LRU // HEAD_DIM
H_RWKV = W_RWKV // HEAD_DIM
H_MLSTM = W_MLSTM // HEAD_DIM
CONV_W = 4
LRU_C = 8.0
RWKV_LORA_W = 32
RWKV_LORA_A = 32
RWKV_LORA_G = 64
RWKV_GN_EPS = 64e-5
MLSTM_CHUNK = 64
HYENA_ORDER = 2
HYENA_SHORT = 3
HYENA_EMB = 33
HYENA_HID = 64
HYENA_TARGET = 1e-2
HYENA_FAST = 0.3
HYENA_SLOW = 1.5
PEER_HEADS = 8
PEER_NKEYS = 128
PEER_EXPERTS = PEER_NKEYS * PEER_NKEYS
PEER_TOPK = 16
PEER_DQ = 256
PEER_BLOCK = 128
IN_SPLITS = (W_LRU, W_LRU, W_RWKV, W_RWKV, W_RWKV, RWKV_LORA_W, RWKV_LORA_A, RWKV_LORA_G, W_MLSTM, W_MLSTM, W_MLSTM, W_MLSTM, 4 * H_MLSTM, W_HYENA, W_HYENA, W_HYENA)
D_IN = sum(IN_SPLITS)


def rmsnorm(x, g):
    xf = x.astype(F32)
    y = xf * lax.rsqrt(jnp.mean(xf * xf, axis=-1, keepdims=True) + EPS)
    return (y * g.astype(F32)).astype(x.dtype)


def modulate(h, shift, scale):
    return h * (1 + scale) + shift


def flip_if(a, d, axis=1):
    return jnp.flip(a, axis) if d == 1 else a


def split_cols(z):
    offs = np.cumsum(IN_SPLITS)[:-1].tolist()
    return jnp.split(z, offs, axis=-1)


def dwconv(x, w, b, pad_left):
    K = w.shape[0]
    y = lax.conv_general_dilated(x, w[:, None, :].astype(x.dtype), (1,), [(pad_left, K - 1 - pad_left)],
                                 dimension_numbers=('NWC', 'WIO', 'NWC'), feature_group_count=x.shape[-1])
    return y + b.astype(x.dtype)


def token_shift(x, mu):
    prev = jnp.pad(x, ((0, 0), (1, 0), (0, 0)))[:, :-1]
    nxt = jnp.pad(x, ((0, 0), (0, 1), (0, 0)))[:, 1:]
    return x + mu[0] * (prev - x) + mu[1] * (nxt - x)


def split_heads(t, h):
    Bn, L, W = t.shape
    return t.reshape(Bn, L, h, W // h)


def to_colmajor(a):
    Bn, L, C = a.shape
    rows = L // GRID_W
    return a.reshape(Bn, rows, GRID_W, C).transpose(0, 2, 1, 3).reshape(Bn, L, C)


def from_colmajor(a):
    Bn, L, C = a.shape
    rows = L // GRID_W
    return a.reshape(Bn, GRID_W, rows, C).transpose(0, 2, 1, 3).reshape(Bn, L, C)


LRU_TC = 256


def _expm1(z):
    u = jnp.exp(z)
    return jnp.where(u == 1.0, z, (u - 1.0) * z / jnp.where(u == 1.0, 1.0, jnp.log(u)))


def _lru_body(x_ref, h0_ref, wr_ref, wi_ref, br_ref, bi_ref, sp_ref, h_ref, hT_ref, st_ref, *, reverse):
    TC, W = x_ref.shape[1], x_ref.shape[2]

    @pl.when(pl.program_id(1) == 0)
    def _():
        st_ref[...] = h0_ref[0]

    x = x_ref[0]
    r = jax.nn.sigmoid(jnp.dot(x, wr_ref[...], preferred_element_type=F32) + br_ref[...])
    i = jax.nn.sigmoid(jnp.dot(x, wi_ref[...], preferred_element_type=F32) + bi_ref[...])
    log_a = -LRU_C * r * sp_ref[...]
    a = jnp.exp(log_a)
    b = jnp.sqrt(-_expm1(2.0 * log_a)) * (i * x)
    row = lax.broadcasted_iota(jnp.int32, (TC, W), 0)
    s = 1
    while s < TC:
        if reverse:
            keep = row < TC - s
            a_sh = jnp.where(keep, pltpu.roll(a, TC - s, 0), 1.0)
            b_sh = jnp.where(keep, pltpu.roll(b, TC - s, 0), 0.0)
        else:
            keep = row >= s
            a_sh = jnp.where(keep, pltpu.roll(a, s, 0), 1.0)
            b_sh = jnp.where(keep, pltpu.roll(b, s, 0), 0.0)
        b = b + a * b_sh
        a = a * a_sh
        s *= 2
    h = b + a * st_ref[...]
    h_ref[0] = h
    last = h[0:1, :] if reverse else h[TC - 1:TC, :]
    st_ref[...] = last
    hT_ref[0] = last


def lru_scan_dir(xc, h0, wr_bd, wi_bd, br, bi, sp, reverse):
    Bn, L, W = xc.shape
    TC = min(LRU_TC, L)
    n = L // TC
    tmap = (lambda b, c: (b, n - 1 - c, 0)) if reverse else (lambda b, c: (b, c, 0))
    wspec = pl.BlockSpec((W, W), lambda b, c: (0, 0))
    vspec = pl.BlockSpec((1, W), lambda b, c: (0, 0))
    sspec = pl.BlockSpec((1, 1, W), lambda b, c: (b, 0, 0))
    return pl.pallas_call(
        functools.partial(_lru_body, reverse=reverse),
        grid=(Bn, n),
        in_specs=[pl.BlockSpec((1, TC, W), tmap), sspec, wspec, wspec, vspec, vspec, vspec],
        out_specs=[pl.BlockSpec((1, TC, W), tmap), sspec],
        out_shape=[jax.ShapeDtypeStruct((Bn, L, W), F32), jax.ShapeDtypeStruct((Bn, 1, W), F32)],
        scratch_shapes=[pltpu.VMEM((1, W), F32)],
        compiler_params=pltpu.CompilerParams(dimension_semantics=("parallel", "arbitrary")),
        name="rglru_scan",
    )(xc, h0, wr_bd, wi_bd, br, bi, sp)


def head_block_diag(w):
    D2, H, N, _ = w.shape
    return jnp.einsum('dhij,hg->dhigj', w, jnp.eye(H, dtype=w.dtype)).reshape(D2, H * N, H * N)


def mixer_rglru(f_c, f_l, p, need_ctx):
    (x_c, g_c), (x_l, g_l) = f_c, f_l
    xc_c = dwconv(x_c, p['lru_conv_w'], p['lru_conv_b'], CONV_W // 2).astype(F32)
    xc_l = dwconv(x_l, p['lru_conv_w'], p['lru_conv_b'], CONV_W // 2).astype(F32)
    h0 = jnp.zeros((xc_l.shape[0], 1, W_LRU), F32)
    wr, wi = head_block_diag(p['lru_wr']), head_block_diag(p['lru_wi'])
    sp = jax.nn.softplus(-p['lru_lam'])
    hs_c, hs_l = [], []
    for d in range(2):
        gates = (wr[d], wi[d], p['lru_br'][d][None], p['lru_bi'][d][None], sp[d][None])
        hc, st = lru_scan_dir(xc_c, h0, *gates, reverse=(d == 1))
        hl, _ = lru_scan_dir(xc_l, st, *gates, reverse=(d == 1))
        hs_c.append(hc)
        hs_l.append(hl)
    y_l = jax.nn.gelu(g_l.astype(F32)) * (hs_l[0] + hs_l[1])
    y_c = jax.nn.gelu(g_c.astype(F32)) * (hs_c[0] + hs_c[1]) if need_ctx else None
    return y_c, y_l


def rwkv_shift(f, p):
    r, k, v, zw, za, zg = [t.astype(F32) for t in f]
    mu = p['rwkv_mu']
    return (token_shift(r, mu[0]), token_shift(k, mu[1]), token_shift(v, mu[2]), zw, za, zg)


def rwkv_dir_inputs(f, p, d):
    r, k, v, zw, za, _ = f
    w_log = -jax.nn.softplus(-(p['rwkv_w0'][d] + jnp.tanh(zw) @ p['rwkv_w2'][d])) - 0.5
    decay = jnp.exp(-jnp.exp(w_log))
    a = jax.nn.sigmoid(p['rwkv_a0'][d] + za @ p['rwkv_a2'][d])
    kk = split_heads(k * p['rwkv_kk'], H_RWKV)
    kk = kk / jnp.maximum(jnp.linalg.norm(kk, axis=-1, keepdims=True), 1e-12)
    kd = k * (1 + (a - 1) * p['rwkv_ka'])
    return [split_heads(r, H_RWKV), split_heads(decay, H_RWKV), split_heads(kd, H_RWKV),
            split_heads(v, H_RWKV), kk, kk * split_heads(a, H_RWKV)]


RWKV_TC = HEAD_DIM
RWKV_PAIRS = 32


def _rwkv_body(kk_ref, v_ref, r_ref, w_ref, kka_ref, k_ref, y_ref, s_ref):
    NP, N = RWKV_PAIRS, HEAD_DIM
    rev = pl.program_id(0)

    @pl.when(pl.program_id(1) == 0)
    def _():
        s_ref[...] = jnp.zeros_like(s_ref)

    y_ref[...] = jnp.zeros_like(y_ref)
    lane = lax.broadcasted_iota(jnp.int32, (N, 2 * N), 1)
    row = lax.broadcasted_iota(jnp.int32, (N, 2 * N), 0)
    diag = (lane % N) == row
    seg4 = lax.broadcasted_iota(jnp.int32, (4 * N, 4 * N), 0) // N == \
        lax.broadcasted_iota(jnp.int32, (4 * N, 4 * N), 1) // N
    ones4 = jnp.where(seg4, 1.0, 0.0).astype(jnp.bfloat16)

    def segsum_bcast(x):
        H = x.shape[0] // 2
        xb = x.astype(jnp.bfloat16)
        lhs = jnp.concatenate([xb[:H].reshape(H * N, 2 * N), xb[H:].reshape(H * N, 2 * N)], axis=-1)
        out = jnp.dot(lhs, ones4, preferred_element_type=F32)
        return jnp.concatenate([out[:, :2 * N].reshape(H, N, 2 * N), out[:, 2 * N:].reshape(H, N, 2 * N)], axis=0)

    def rowv(ref, t):
        x = ref[:, pl.ds(t, 1), :] if len(ref.shape) == 3 else ref[0, :, pl.ds(t, 1), :]
        return jnp.concatenate([x[:, :, :2 * N], x[:, :, 2 * N:]], axis=0)

    def step(i, carry):
        t = i + rev * (RWKV_TC - 1 - 2 * i)
        tp = jnp.clip(t - 1 + 2 * rev, 0, RWKV_TC - 1)
        S = s_ref[...]
        R = segsum_bcast(jnp.concatenate([S * rowv(kk_ref, t), jnp.where(diag, rowv(v_ref, t), 0.0),
                                          S * rowv(r_ref, tp)], axis=0))
        y_ref[0, :, 0] = jnp.where(((lane % N) == tp) & (i > 0), R[2 * NP:], y_ref[0, :, 0])
        s_ref[...] = S * rowv(w_ref, t) - R[:NP] * rowv(kka_ref, t) + R[NP:2 * NP] * rowv(k_ref, t)
        return carry

    lax.fori_loop(0, RWKV_TC, step, 0)
    t_last = (RWKV_TC - 1) * (1 - rev)
    Y = segsum_bcast(s_ref[...] * rowv(r_ref, t_last))
    y_ref[0, :, 0] = jnp.where((lane % N) == t_last, Y, y_ref[0, :, 0])


def rwkv_scan_bidir(kk, v, r, w, kka, k, n_ctx_blocks):
    Bn, L, C = kk.shape
    TC, N = RWKV_TC, HEAD_DIM
    n = L // TC
    assert RWKV_PAIRS == Bn * C // (2 * N)

    def blk(d, c):
        back = jnp.where(c < n_ctx_blocks, n_ctx_blocks - 1 - c, n + n_ctx_blocks - 1 - c)
        return jnp.where(d == 0, c, back)

    shared = pl.BlockSpec((Bn, TC, C), lambda d, c: (0, blk(d, c), 0))
    per_dir = pl.BlockSpec((1, Bn, TC, C), lambda d, c: (d, 0, blk(d, c), 0))
    return pl.pallas_call(
        _rwkv_body,
        grid=(2, n),
        in_specs=[shared, shared, shared, per_dir, per_dir, per_dir],
        out_specs=pl.BlockSpec((1, RWKV_PAIRS, 1, N, 2 * N), lambda d, c: (d, 0, blk(d, c), 0, 0)),
        out_shape=jax.ShapeDtypeStruct((2, RWKV_PAIRS, n, N, 2 * N), F32),
        scratch_shapes=[pltpu.VMEM((RWKV_PAIRS, N, 2 * N), F32)],
        compiler_params=pltpu.CompilerParams(dimension_semantics=("parallel", "arbitrary"),
                                             vmem_limit_bytes=40 * 1024 * 1024),
        name="rwkv7_scan",
    )(kk, v, r, w, kka, k)


def rwkv7_scan_bidir(ins_c, ins_l):
    Bn, Lc = ins_c[0][0].shape[:2]
    Ll = ins_l[0][0].shape[1]
    L = Lc + Ll
    cat = lambda d, j: jnp.concatenate([ins_c[d][j], ins_l[d][j]], axis=1).reshape(Bn, L, W_RWKV)
    per_dir = lambda j: jnp.stack([cat(0, j), cat(1, j)])
    yT = rwkv_scan_bidir(cat(0, 4), cat(0, 3), cat(0, 0), per_dir(1), per_dir(5), per_dir(2), Lc // RWKV_TC)
    halves = W_RWKV // (2 * HEAD_DIM)
    y = yT.reshape(2, halves, Bn, L // HEAD_DIM, HEAD_DIM, 2, HEAD_DIM)
    y = y.transpose(0, 2, 3, 6, 1, 5, 4).reshape(2, Bn, L, H_RWKV, HEAD_DIM)
    return ([y[d, :, :Lc] for d in range(2)], [y[d, :, Lc:] for d in range(2)])


def rwkv_bonus(ins, p):
    r, _, kd, v = ins[:4]
    return jnp.sum(r * kd * p['rwkv_rk'], axis=-1, keepdims=True) * v


def rwkv_out(y, bonus, zg, p):
    Bn, L, H, N = y.shape
    mu = jnp.mean(y, axis=-1, keepdims=True)
    var = jnp.mean(jnp.square(y - mu), axis=-1, keepdims=True)
    yn = ((y - mu) * lax.rsqrt(var + RWKV_GN_EPS)).reshape(Bn, L, H * N) * p['rwkv_ln_g'] + p['rwkv_ln_b']
    g = jax.nn.sigmoid(zg) @ p['rwkv_g2']
    return (yn + bonus.reshape(Bn, L, H * N)) * g


def mixer_rwkv(f_c, f_l, p, need_ctx):
    fc, fl = rwkv_shift(f_c, p), rwkv_shift(f_l, p)
    ins_c = [rwkv_dir_inputs(fc, p, d) for d in range(2)]
    ins_l = [rwkv_dir_inputs(fl, p, d) for d in range(2)]
    o_c, o_l = rwkv7_scan_bidir(ins_c, ins_l)
    y_c, y_l = [], []
    for d in range(2):
        y_l.append((o_l[d], rwkv_bonus(ins_l[d], p)))
        if need_ctx:
            y_c.append((o_c[d], rwkv_bonus(ins_c[d], p)))
    out_l = rwkv_out(y_l[0][0] + y_l[1][0], y_l[0][1] + y_l[1][1], fl[5], p)
    out_c = rwkv_out(y_c[0][0] + y_c[1][0], y_c[0][1] + y_c[1][1], fc[5], p) if need_ctx else None
    return out_c, out_l


def mlstm_prep(f, p):
    q, k, v, o, gz = f
    Bn, L, _ = q.shape
    qk = jax.nn.silu(dwconv(jnp.concatenate([q, k], axis=-1), p['mlstm_conv_w'], p['mlstm_conv_b'], CONV_W // 2)).astype(F32)
    q, k = jnp.split(qk, 2, axis=-1)
    gates = gz.astype(F32).reshape(Bn, L, 2, 2, H_MLSTM) + p['mlstm_gate_b']
    return q * HEAD_DIM ** -0.5, k, v.astype(F32), o.astype(F32), gates


MLSTM_ROWS = 2


def _mlstm_body(q_ref, k_ref, v_ref, gc_ref, gr_ref, h_ref, c_ref, n_ref, m_ref):
    T, N, H = MLSTM_CHUNK, HEAD_DIM, H_MLSTM
    rev = pl.program_id(0)

    @pl.when(pl.program_id(2) == 0)
    def _():
        c_ref[...] = jnp.zeros_like(c_ref)
        n_ref[...] = jnp.zeros_like(n_ref)
        m_ref[...] = jnp.zeros_like(m_ref)

    ti = lax.broadcasted_iota(jnp.int32, (T, T), 0)
    si = lax.broadcasted_iota(jnp.int32, (T, T), 1)
    mask = (si - ti) * (1 - 2 * rev) <= 0
    tri = jnp.where(mask, 1.0, 0.0)
    hp = lax.Precision.HIGHEST
    gcs = [gc_ref[0, r, 0] for r in range(MLSTM_ROWS)]
    grs = [gr_ref[0, r, 0] for r in range(MLSTM_ROWS)]
    b_cols = [jnp.dot(tri, g[:, H:], precision=hp, preferred_element_type=F32) for g in gcs]
    b_rows = [lax.dot_general(g[H:], tri, (((1,), (1,)), ((), ())), precision=hp, preferred_element_type=F32)
              for g in grs]
    b_tots = [jnp.sum(g[:, H:], axis=0, keepdims=True) for g in gcs]
    units = [(r, h) for r in range(MLSTM_ROWS) for h in range(H)]
    hs = range(len(units))
    sl = [slice(h * N, (h + 1) * N) for _, h in units]
    qh = [q_ref[units[h][0], :, sl[h]] for h in hs]
    kh = [k_ref[units[h][0], :, sl[h]] for h in hs]
    vh = [v_ref[units[h][0], :, sl[h]] for h in hs]
    C = [c_ref[h] for h in hs]
    n = [n_ref[h] for h in hs]
    m = [m_ref[h][:, 0:1] for h in hs]
    bc = [b_cols[r][:, h:h + 1] for r, h in units]
    br = [b_rows[r][h:h + 1, :] for r, h in units]
    ic = [gcs[r][:, h:h + 1] for r, h in units]
    ir = [grs[r][h:h + 1, :] for r, h in units]
    bT = [b_tots[r][:, h:h + 1] for r, h in units]
    nt = (((1,), (1,)), ((), ()))
    qk = [lax.dot_general(qh[h], kh[h], nt, preferred_element_type=F32) for h in hs]
    qc = [lax.dot_general(qh[h], C[h], nt, preferred_element_type=F32) for h in hs]
    logd = [jnp.where(mask, bc[h] - br[h] + ir[h], -jnp.inf) for h in hs]
    inter = [bc[h] + m[h] for h in hs]
    mt = [jnp.maximum(inter[h], jnp.max(logd[h], axis=-1, keepdims=True)) for h in hs]
    s = [qk[h] * jnp.exp(logd[h] - mt[h]) for h in hs]
    e_inter = [jnp.exp(inter[h] - mt[h]) for h in hs]
    num = [jnp.dot(s[h], vh[h], preferred_element_type=F32) + e_inter[h] * qc[h] for h in hs]
    den = [jnp.sum(s[h], axis=-1, keepdims=True) + e_inter[h] * jnp.sum(qh[h] * n[h], axis=-1, keepdims=True)
           for h in hs]
    out = [num[h] / jnp.maximum(jnp.abs(den[h]), jnp.exp(-mt[h])) for h in hs]
    for r in range(MLSTM_ROWS):
        h_ref[0, r] = jnp.concatenate(out[r * H:(r + 1) * H], axis=-1)
    m_new = [jnp.maximum(bT[h] + m[h], jnp.max(bT[h] - br[h] + ir[h], axis=-1, keepdims=True)) for h in hs]
    w_col = [jnp.exp(bT[h] - bc[h] + ic[h] - m_new[h]) for h in hs]
    dec = [jnp.exp(bT[h] + m[h] - m_new[h]) for h in hs]
    vk = [lax.dot_general(vh[h] * w_col[h], kh[h], (((0,), (0,)), ((), ())), preferred_element_type=F32) for h in hs]
    for h in hs:
        c_ref[h] = dec[h] * C[h] + vk[h]
        n_ref[h] = dec[h] * n[h] + jnp.sum(kh[h] * w_col[h], axis=0, keepdims=True)
        m_ref[h] = jnp.broadcast_to(m_new[h], (1, 128))


def mlstm_scan_bidir(q, k, v, gcol, grow, n_ctx_blocks):
    Bn, L, W = q.shape
    T = MLSTM_CHUNK
    n = L // T

    def blk(d, c):
        back = jnp.where(c < n_ctx_blocks, n_ctx_blocks - 1 - c, n + n_ctx_blocks - 1 - c)
        return jnp.where(d == 0, c, back)

    R = MLSTM_ROWS
    xspec = pl.BlockSpec((R, T, W), lambda d, b, c: (b, blk(d, c), 0))
    gcs = pl.BlockSpec((1, R, 1, T, 2 * H_MLSTM), lambda d, b, c: (d, b, blk(d, c), 0, 0))
    grs = pl.BlockSpec((1, R, 1, 2 * H_MLSTM, T), lambda d, b, c: (d, b, blk(d, c), 0, 0))
    return pl.pallas_call(
        _mlstm_body,
        grid=(2, Bn // R, n),
        in_specs=[xspec, xspec, xspec, gcs, grs],
        out_specs=pl.BlockSpec((1, R, T, W), lambda d, b, c: (d, b, blk(d, c), 0)),
        out_shape=jax.ShapeDtypeStruct((2, Bn, L, W), F32),
        scratch_shapes=[pltpu.VMEM((R * H_MLSTM, HEAD_DIM, HEAD_DIM), F32), pltpu.VMEM((R * H_MLSTM, 1, HEAD_DIM), F32),
                        pltpu.VMEM((R * H_MLSTM, 1, 128), F32)],
        compiler_params=pltpu.CompilerParams(dimension_semantics=("parallel", "parallel", "arbitrary")),
        name="mlstm_chunkwise",
    )(q, k, v, gcol, grow)


def mixer_mlstm(f_c, f_l, p, need_ctx):
    f_l = [to_colmajor(t) for t in f_l]
    qc, kc, vc, oc, gc = mlstm_prep(f_c, p)
    ql, kl, vl, ol, gl = mlstm_prep(f_l, p)
    Bn, Lc = qc.shape[:2]
    cat = lambda a, b: jnp.concatenate([a, b], axis=1)
    g = cat(gc, gl)
    L = g.shape[1]
    g = jnp.concatenate([g[:, :, :, 0], jax.nn.log_sigmoid(g[:, :, :, 1])], axis=-1)
    gcol = g.transpose(2, 0, 1, 3).reshape(2, Bn, L // MLSTM_CHUNK, MLSTM_CHUNK, 2 * H_MLSTM)
    h = mlstm_scan_bidir(cat(qc, ql), cat(kc, kl), cat(vc, vl), gcol, gcol.transpose(0, 1, 2, 4, 3),
                         Lc // MLSTM_CHUNK)
    hs = h[0] + h[1]
    y_l = from_colmajor(jax.nn.sigmoid(ol) * hs[:, Lc:])
    y_c = jax.nn.sigmoid(oc) * hs[:, :Lc] if need_ctx else None
    return y_c, y_l


def hyena_spectrum(L, p):
    pos = jnp.arange(L, dtype=F32)
    t = pos / (L - 1)
    bands = (HYENA_EMB - 1) // 2
    freqs = jnp.linspace(1e-4, bands - 1, bands, dtype=F32)
    ang = (2 * math.pi / L) * pos[:, None] * freqs[None, :]
    z = jnp.concatenate([t[:, None], jnp.cos(ang), -jnp.sin(ang)], axis=-1)
    h = jnp.sin(p['hy_freq'][0] * (z @ p['hy_w1'] + p['hy_b1']))
    h = jnp.sin(p['hy_freq'][1] * (h @ p['hy_w2'] + p['hy_b2']))
    h = (h @ p['hy_w3']).astype(F32).reshape(L, HYENA_ORDER, 2, W_HYENA)
    deltas = jnp.abs(jnp.linspace(math.log(HYENA_TARGET) / HYENA_SLOW, math.log(HYENA_TARGET) / HYENA_FAST,
                                  W_HYENA, dtype=F32))
    h = h * jnp.exp(-t[:, None, None, None] * deltas)
    fwd, bwd = h[:, :, 0], h[:, :, 1]
    two = jnp.concatenate([fwd, jnp.zeros_like(fwd[:1]), jnp.flip(bwd[1:], axis=0)], axis=0)
    two = two / (jnp.sum(jnp.abs(two), axis=0, keepdims=True) + EPS)
    return jnp.fft.rfft(two, axis=0)


def long_conv(u, spec, bias):
    L = u.shape[1]
    y = jnp.fft.irfft(jnp.fft.rfft(u, n=2 * L, axis=1) * spec, n=2 * L, axis=1)[:, :L]
    return y + u * bias


def mixer_hyena(f, p):
    u = dwconv(jnp.concatenate(f, axis=-1), p['hy_conv_w'], p['hy_conv_b'], HYENA_SHORT // 2).astype(F32)
    v, x1, x2 = jnp.split(u, 3, axis=-1)
    spec = hyena_spectrum(u.shape[1], p)
    z = x1 * long_conv(v, spec[:, 0], p['hy_bias'][0])
    return x2 * long_conv(z, spec[:, 1], p['hy_bias'][1])


PEER_SEL = PEER_HEADS * PEER_TOPK
PEER_HALF = D_MODEL // 2
PEER_CHUNK = 2048
PEER_FIRST_CHUNK = 512
PEER_TT = 16
SC_GATHER_ROWS = 64


def pack_bf16_halves(tab):
    bits = lax.bitcast_convert_type(tab.astype(jnp.bfloat16), jnp.uint16).astype(jnp.uint32)
    half = tab.shape[1] // 2
    return bits[:, :half] | (bits[:, half:] << 16)


def sc_gather_pair(u_tab, v_tab, idx):
    n = idx.shape[0]
    W = u_tab.shape[1]
    info = plsc.get_sparse_core_info()
    nc, ns = info.num_cores, info.num_subcores
    per_w = n // (nc * ns)
    G = SC_GATHER_ROWS
    assert per_w * nc * ns == n and per_w % G == 0
    nsteps = per_w // G
    mesh = plsc.VectorSubcoreMesh(core_axis_name="c", subcore_axis_name="s")
    out = jax.ShapeDtypeStruct((n, W), u_tab.dtype)

    def body(u_hbm, v_hbm, idx_hbm, uo_hbm, vo_hbm, idx_v, urows, vrows, usem, vsem):
        wid = lax.axis_index("s") * nc + lax.axis_index("c")
        base = wid * per_w
        pltpu.sync_copy(idx_hbm.at[pl.ds(base, per_w)], idx_v)

        @pl.loop(0, nsteps)
        def _(i):
            off = pl.multiple_of(i * G, G)
            ids = idx_v.at[pl.ds(off, G)]
            cu = pltpu.async_copy(u_hbm.at[ids], urows, usem)
            cv = pltpu.async_copy(v_hbm.at[ids], vrows, vsem)
            cu.wait()
            pltpu.sync_copy(urows, uo_hbm.at[pl.ds(base + off, G)])
            cv.wait()
            pltpu.sync_copy(vrows, vo_hbm.at[pl.ds(base + off, G)])

    fn = pl.kernel(body, out_type=(out, out), mesh=mesh,
                   scratch_types=[pltpu.VMEM((per_w,), jnp.int32),
                                  pltpu.VMEM((G, W), u_tab.dtype), pltpu.VMEM((G, W), u_tab.dtype),
                                  pltpu.SemaphoreType.DMA, pltpu.SemaphoreType.DMA],
                   name="peer_sc_gather")
    return fn(u_tab, v_tab, idx)


def _unpack_halves(w):
    lo = lax.bitcast_convert_type(w << 16, F32)
    hi = lax.bitcast_convert_type(w & jnp.uint32(0xFFFF0000), F32)
    return lo, hi


def _peer_apply_body(x_ref, g_ref, ug_ref, vg_ref, o_ref):
    TT = x_ref.shape[0]
    gpad = jnp.concatenate([g_ref[...], jnp.zeros((PEER_SEL - TT, PEER_SEL), F32)], axis=0)
    gT = gpad.T
    for t in range(TT):
        rows = pl.ds(t * PEER_SEL, PEER_SEL)
        ulo, uhi = _unpack_halves(ug_ref[rows, :])
        xl = x_ref[pl.ds(t, 1), 0:PEER_HALF]
        xh = x_ref[pl.ds(t, 1), PEER_HALF:D_MODEL]
        dots = jnp.sum(ulo * xl + uhi * xh, axis=-1, keepdims=True)
        w = jax.nn.gelu(dots) * gT[:, t:t + 1]
        vlo, vhi = _unpack_halves(vg_ref[rows, :])
        o_ref[pl.ds(t, 1), 0:PEER_HALF] = jnp.sum(vlo * w, axis=0, keepdims=True)
        o_ref[pl.ds(t, 1), PEER_HALF:D_MODEL] = jnp.sum(vhi * w, axis=0, keepdims=True)


def peer_apply(x, gate, ug, vg):
    T = x.shape[0]
    TT = PEER_TT
    return pl.pallas_call(
        _peer_apply_body,
        grid=(T // TT,),
        in_specs=[pl.BlockSpec((TT, D_MODEL), lambda i: (i, 0)),
                  pl.BlockSpec((TT, PEER_SEL), lambda i: (i, 0)),
                  pl.BlockSpec((TT * PEER_SEL, PEER_HALF), lambda i: (i, 0)),
                  pl.BlockSpec((TT * PEER_SEL, PEER_HALF), lambda i: (i, 0))],
        out_specs=pl.BlockSpec((TT, D_MODEL), lambda i: (i, 0)),
        out_shape=jax.ShapeDtypeStruct((T, D_MODEL), F32),
        compiler_params=pltpu.CompilerParams(dimension_semantics=("parallel",),
                                             vmem_limit_bytes=40 * 1024 * 1024),
        name="peer_apply",
    )(x, gate, ug, vg)


ROUTE_TT = 1024
ROUTE_PAIR_ROWS = sum(-(-(PEER_TOPK // (i + 1)) // 8) * 8 for i in range(PEER_TOPK // 2)) + PEER_TOPK // 2


def _peer_route_body(s_ref, e_ref, g_ref, xs_ref, sv_ref, si_ref, cand_ref, cidx_ref, pf_ref, ts_ref):
    K, NK, TT = PEER_TOPK, PEER_NKEYS, s_ref.shape[-1]
    NEG = -jnp.inf
    xs_ref[...] = s_ref[0]
    kiota = lax.broadcasted_iota(jnp.int32, (NK, TT), 0).astype(F32)

    def half_topk(it, carry):
        for c in range(2):
            x = xs_ref[c]
            m = jnp.max(x, axis=0, keepdims=True)
            idx = jnp.min(jnp.where(x == m, kiota, float(NK)), axis=0, keepdims=True)
            xs_ref[c] = jnp.where(kiota == idx, NEG, x)
            sv_ref[c, pl.ds(it, 1), :] = m
            si_ref[c, pl.ds(it, 1), :] = idx
        return carry

    lax.fori_loop(0, K, half_topk, 0)

    jiota = lambda rows: lax.broadcasted_iota(jnp.int32, (rows, TT), 0).astype(F32)
    off = 0
    for i in range(K // 2):
        n = K // (i + 1)
        rows = -(-n // 8) * 8
        ok = jiota(rows) < float(n)
        cand_ref[pl.ds(off, rows), :] = jnp.where(ok, sv_ref[0, i:i + 1, :] + sv_ref[1, 0:rows, :], NEG)
        cidx_ref[pl.ds(off, rows), :] = si_ref[0, i:i + 1, :] * float(NK) + si_ref[1, 0:rows, :]
        pf_ref[pl.ds(off, rows), :] = jnp.where(ok, float(i * K) + jiota(rows), float(K * K))
        off += rows
    cand_ref[pl.ds(off, K // 2), :] = sv_ref[0, K // 2:K, :] + sv_ref[1, 0:1, :]
    cidx_ref[pl.ds(off, K // 2), :] = si_ref[0, K // 2:K, :] * float(NK) + si_ref[1, 0:1, :]
    pf_ref[pl.ds(off, K // 2), :] = (float(K // 2) + jiota(K // 2)) * float(K)
    piota = pf_ref[...]

    def pair_topk(it, carry):
        x = cand_ref[...]
        m = jnp.max(x, axis=0, keepdims=True)
        pos = jnp.min(jnp.where(x == m, piota, float(K * K)), axis=0, keepdims=True)
        sel = piota == pos
        cand_ref[...] = jnp.where(sel, NEG, x)
        ts_ref[pl.ds(it, 1), :] = m
        e_ref[0, pl.ds(it, 1), :] = jnp.max(jnp.where(sel, cidx_ref[...], -1.0), axis=0,
                                            keepdims=True).astype(jnp.int32)
        return carry

    lax.fori_loop(0, K, pair_topk, 0)
    ts = ts_ref[...]
    ex = jnp.exp(ts - ts[0:1, :])
    g_ref[0] = ex / jnp.sum(ex, axis=0, keepdims=True)


def peer_route_topk(sT):
    H, _, NK, N = sT.shape
    TT, K = ROUTE_TT, PEER_TOPK
    out_spec = pl.BlockSpec((1, K, TT), lambda h, i: (h, 0, i))
    return pl.pallas_call(
        _peer_route_body,
        grid=(H, N // TT),
        in_specs=[pl.BlockSpec((1, 2, NK, TT), lambda h, i: (h, 0, 0, i))],
        out_specs=[out_spec, out_spec],
        out_shape=[jax.ShapeDtypeStruct((H, K, N), jnp.int32), jax.ShapeDtypeStruct((H, K, N), F32)],
        scratch_shapes=[pltpu.VMEM((2, NK, TT), F32), pltpu.VMEM((2, K, TT), F32), pltpu.VMEM((2, K, TT), F32),
                        pltpu.VMEM((ROUTE_PAIR_ROWS, TT), F32), pltpu.VMEM((ROUTE_PAIR_ROWS, TT), F32),
                        pltpu.VMEM((ROUTE_PAIR_ROWS, TT), F32), pltpu.VMEM((K, TT), F32)],
        compiler_params=pltpu.CompilerParams(dimension_semantics=("parallel", "parallel")),
        name="peer_route_topk",
    )(sT)


def peer_route(xf, p):
    N = xf.shape[0]
    wq, keys = p['peer_wq'], p['peer_keys'].astype(F32)
    q = (xf @ wq).astype(F32).reshape(N, PEER_HEADS, 2, PEER_DQ // 2)
    sT = jnp.einsum('thcq,hckq->hckt', q, keys)
    eT, gT = peer_route_topk(sT)
    return (eT.transpose(2, 0, 1).reshape(N, PEER_SEL), gT.transpose(2, 0, 1).reshape(N, PEER_SEL))


def peer_ffn(h, p):
    Bn, L, D = h.shape
    N = Bn * L
    xf = h.reshape(N, D)
    eidx, gate = peer_route(xf, p)
    u_pk, v_pk = p['peer_u_pk'], p['peer_v_pk']
    outs, c0, size = [], 0, PEER_FIRST_CHUNK
    while c0 < N:
        rows = slice(c0, min(N, c0 + size))
        ug, vg = sc_gather_pair(u_pk, v_pk, eidx[rows].reshape(-1))
        outs.append(peer_apply(xf[rows], gate[rows], ug, vg))
        c0, size = rows.stop, min(2 * size, PEER_CHUNK)
    return jnp.concatenate(outs, axis=0).reshape(Bn, L, D)


def merge_groups(ys, g, dtype):
    outs, off = [], 0
    for y, w in zip(ys, GROUP_WIDTHS):
        outs.append(rmsnorm(y, g[off:off + w]).astype(dtype))
        off += w
    return jnp.concatenate(outs, axis=-1)


def token_mixers(h_c, h_l, p, need_ctx):
    zc = split_cols(h_c @ p['w_in'])
    zl = split_cols(h_l @ p['w_in'])
    a_c, a_l = mixer_rglru(zc[0:2], zl[0:2], p, need_ctx)
    b_c, b_l = mixer_rwkv(zc[2:8], zl[2:8], p, need_ctx)
    m_c, m_l = mixer_mlstm(zc[8:13], zl[8:13], p, need_ctx)
    d_l = mixer_hyena(zl[13:16], p)
    o_l = merge_groups([a_l, b_l, m_l, d_l], p['grp_g'], h_l.dtype) @ p['w_out']
    if not need_ctx:
        return None, o_l
    d_c = mixer_hyena(zc[13:16], p)
    o_c = merge_groups([a_c, b_c, m_c, d_c], p['grp_g'], h_c.dtype) @ p['w_out']
    return o_c, o_l


def trunk_layer(x_l, x_c, c, c_ctx, p, need_ctx):
    Bn = c.shape[0]
    mod_l = (jax.nn.silu(c) @ p['ada_w'] + p['ada_b']).reshape(Bn, 6, 1, D_MODEL)
    mod_c = (jax.nn.silu(c_ctx) @ p['ada_w'] + p['ada_b']).reshape(6, 1, 1, D_MODEL)
    h_l = modulate(rmsnorm(x_l, p['norm1_g']), mod_l[:, 0], mod_l[:, 1])
    h_c = modulate(rmsnorm(x_c, p['norm1_g']), mod_c[0], mod_c[1])
    o_c, o_l = token_mixers(h_c, h_l, p, need_ctx)
    x_l = x_l + mod_l[:, 2] * o_l
    x_l = x_l + mod_l[:, 5] * peer_ffn(modulate(rmsnorm(x_l, p['norm2_g']), mod_l[:, 3], mod_l[:, 4]), p)
    if need_ctx:
        x_c = x_c + mod_c[2] * o_c
        x_c = x_c + mod_c[5] * peer_ffn(modulate(rmsnorm(x_c, p['norm2_g']), mod_c[3], mod_c[4]), p)
    return x_l, x_c


def _final_norm_body(x_ref, g_ref, o_ref):
    xf = x_ref[...]
    y = xf * lax.rsqrt(jnp.mean(xf * xf, axis=-1, keepdims=True) + EPS)
    o_ref[...] = y * g_ref[...]


def final_rmsnorm(x, g):
    Bn, L, D = x.shape
    rows = Bn * L
    tile = 1024
    out = pl.pallas_call(
        _final_norm_body,
        grid=(rows // tile,),
        in_specs=[pl.BlockSpec((tile, D), lambda i: (i, 0)), pl.BlockSpec((1, D), lambda i: (0, 0))],
        out_specs=pl.BlockSpec((tile, D), lambda i: (i, 0)),
        out_shape=jax.ShapeDtypeStruct((rows, D), x.dtype),
        name="final_rmsnorm",
    )(x.reshape(rows, D), g.reshape(1, D))
    return out.reshape(Bn, L, D)


def kernel(x, c, ctx, c_ctx, ada_w, ada_b, norm1_g, norm2_g, w_in, w_out, grp_g,
           lru_conv_w, lru_conv_b, lru_wr, lru_br, lru_wi, lru_bi, lru_lam,
           rwkv_mu, rwkv_w0, rwkv_w2, rwkv_a0, rwkv_a2, rwkv_g2, rwkv_kk, rwkv_ka, rwkv_rk, rwkv_ln_g, rwkv_ln_b,
           mlstm_conv_w, mlstm_conv_b, mlstm_gate_b,
           hy_conv_w, hy_conv_b, hy_w1, hy_b1, hy_w2, hy_b2, hy_w3, hy_freq, hy_bias,
           peer_wq, peer_keys, peer_u, peer_v, final_g):
    x_l, x_c = x, ctx
    for i in range(DEPTH):
        p = dict(ada_w=ada_w[i], ada_b=ada_b[i], norm1_g=norm1_g[i], norm2_g=norm2_g[i], w_in=w_in[i],
                 w_out=w_out[i], grp_g=grp_g[i],
                 lru_conv_w=lru_conv_w[i], lru_conv_b=lru_conv_b[i], lru_wr=lru_wr[i], lru_br=lru_br[i],
                 lru_wi=lru_wi[i], lru_bi=lru_bi[i], lru_lam=lru_lam[i],
                 rwkv_mu=rwkv_mu[i], rwkv_w0=rwkv_w0[i], rwkv_w2=rwkv_w2[i], rwkv_a0=rwkv_a0[i],
                 rwkv_a2=rwkv_a2[i], rwkv_g2=rwkv_g2[i], rwkv_kk=rwkv_kk[i], rwkv_ka=rwkv_ka[i],
                 rwkv_rk=rwkv_rk[i], rwkv_ln_g=rwkv_ln_g[i], rwkv_ln_b=rwkv_ln_b[i],
                 mlstm_conv_w=mlstm_conv_w[i], mlstm_conv_b=mlstm_conv_b[i], mlstm_gate_b=mlstm_gate_b[i],
                 hy_conv_w=hy_conv_w[i], hy_conv_b=hy_conv_b[i], hy_w1=hy_w1[i], hy_b1=hy_b1[i],
                 hy_w2=hy_w2[i], hy_b2=hy_b2[i], hy_w3=hy_w3[i], hy_freq=hy_freq[i], hy_bias=hy_bias[i],
                 peer_wq=peer_wq[i], peer_keys=peer_keys[i],
                 peer_u_pk=pack_bf16_halves(peer_u[i]), peer_v_pk=pack_bf16_halves(peer_v[i]))
        x_l, x_c = trunk_layer(x_l, x_c, c, c_ctx, p, i < DEPTH - 1)
    return final_rmsnorm(x_l, final_g)
```

```python
import functools
import math
import jax, jax.numpy as jnp
from jax import lax
import numpy as np
from jax.experimental import pallas as pl
from jax.experimental.pallas import tpu as pltpu
from jax.experimental.pallas import tpu_sc as plsc

D_MODEL = 1024
DEPTH = 2

GRID_W = 64
F32 = jnp.float32
V7X_VMEM_BYTES = 64 * 1024 * 1024
V7X_SUBLANES = 8
BIG_BLOCK_VMEM_LIMIT = V7X_VMEM_BYTES * 5 // 8
EPS = 1e-6
HEAD_DIM = 64
D_MIX = D_MODEL
W_LRU = D_MIX // 4
W_RWKV = D_MIX // 4
W_MLSTM = D_MIX // 4
W_HYENA = D_MIX - W_LRU - W_RWKV - W_MLSTM
GROUP_WIDTHS = (W_LRU, W_RWKV, W_MLSTM, W_HYENA)
H_RWKV = W_RWKV // HEAD_DIM
H_MLSTM = W_MLSTM // HEAD_DIM
CONV_W = 4
LRU_C = 8.0
RWKV_LORA_W = 32
RWKV_LORA_A = 32
RWKV_LORA_G = 64
RWKV_GN_EPS = 64e-5
MLSTM_CHUNK = 64
HYENA_ORDER = 2
HYENA_SHORT = 3
HYENA_EMB = 33
HYENA_TARGET = 1e-2
HYENA_FAST = 0.3
HYENA_SLOW = 1.5
PEER_HEADS = 8
PEER_NKEYS = 128
PEER_TOPK = 16
PEER_DQ = 256
IN_SPLITS = (W_LRU, W_LRU, W_RWKV, W_RWKV, W_RWKV, RWKV_LORA_W, RWKV_LORA_A, RWKV_LORA_G, W_MLSTM, W_MLSTM, W_MLSTM, W_MLSTM, 4 * H_MLSTM, W_HYENA, W_HYENA, W_HYENA)


def rmsnorm(x, g):
    xf = x.astype(F32)
    y = xf * lax.rsqrt(jnp.mean(xf * xf, axis=-1, keepdims=True) + EPS)
    return (y * g.astype(F32)).astype(x.dtype)


def modulate(h, shift, scale):
    return h * (1 + scale) + shift


def split_cols(z):
    offs = np.cumsum(IN_SPLITS)[:-1].tolist()
    return jnp.split(z, offs, axis=-1)


def dwconv(x, w, b, pad_left):
    K = w.shape[0]
    y = lax.conv_general_dilated(x, w[:, None, :].astype(x.dtype), (1,), [(pad_left, K - 1 - pad_left)],
                                 dimension_numbers=('NWC', 'WIO', 'NWC'), feature_group_count=x.shape[-1])
    return y + b.astype(x.dtype)


def token_shift(x, mu):
    prev = jnp.pad(x, ((0, 0), (1, 0), (0, 0)))[:, :-1]
    nxt = jnp.pad(x, ((0, 0), (0, 1), (0, 0)))[:, 1:]
    return x + mu[0] * (prev - x) + mu[1] * (nxt - x)


def split_heads(t, h):
    Bn, L, W = t.shape
    return t.reshape(Bn, L, h, W // h)


def to_colmajor(a):
    Bn, L, C = a.shape
    rows = L // GRID_W
    return a.reshape(Bn, rows, GRID_W, C).transpose(0, 2, 1, 3).reshape(Bn, L, C)


def from_colmajor(a):
    Bn, L, C = a.shape
    rows = L // GRID_W
    return a.reshape(Bn, GRID_W, rows, C).transpose(0, 2, 1, 3).reshape(Bn, L, C)


LRU_TC = 256


def _expm1(z):
    u = jnp.exp(z)
    return jnp.where(u == 1.0, z, (u - 1.0) * z / jnp.where(u == 1.0, 1.0, jnp.log(u)))


def _lru_body(x_ref, h0_ref, wr_ref, wi_ref, br_ref, bi_ref, sp_ref, h_ref, hT_ref, st_ref, *, reverse):
    TC, W = x_ref.shape[1], x_ref.shape[2]

    @pl.when(pl.program_id(1) == 0)
    def _():
        st_ref[...] = h0_ref[0]

    x = x_ref[0]
    r = jax.nn.sigmoid(jnp.dot(x, wr_ref[...], preferred_element_type=F32) + br_ref[...])
    i = jax.nn.sigmoid(jnp.dot(x, wi_ref[...], preferred_element_type=F32) + bi_ref[...])
    log_a = -LRU_C * r * sp_ref[...]
    a = jnp.exp(log_a)
    b = jnp.sqrt(-_expm1(2.0 * log_a)) * (i * x)
    row = lax.broadcasted_iota(jnp.int32, (TC, W), 0)
    s = 1
    while s < TC:
        if reverse:
            keep = row < TC - s
            a_sh = jnp.where(keep, pltpu.roll(a, TC - s, 0), 1.0)
            b_sh = jnp.where(keep, pltpu.roll(b, TC - s, 0), 0.0)
        else:
            keep = row >= s
            a_sh = jnp.where(keep, pltpu.roll(a, s, 0), 1.0)
            b_sh = jnp.where(keep, pltpu.roll(b, s, 0), 0.0)
        b = b + a * b_sh
        a = a * a_sh
        s *= 2
    h = b + a * st_ref[...]
    h_ref[0] = h
    last = h[0:1, :] if reverse else h[TC - 1:TC, :]
    st_ref[...] = last
    hT_ref[0] = last


def lru_scan_dir(xc, h0, wr_bd, wi_bd, br, bi, sp, reverse):
    Bn, L, W = xc.shape
    TC = min(LRU_TC, L)
    n = L // TC
    tmap = (lambda b, c: (b, n - 1 - c, 0)) if reverse else (lambda b, c: (b, c, 0))
    wspec = pl.BlockSpec((W, W), lambda b, c: (0, 0))
    vspec = pl.BlockSpec((1, W), lambda b, c: (0, 0))
    sspec = pl.BlockSpec((1, 1, W), lambda b, c: (b, 0, 0))
    return pl.pallas_call(
        functools.partial(_lru_body, reverse=reverse),
        grid=(Bn, n),
        in_specs=[pl.BlockSpec((1, TC, W), tmap), sspec, wspec, wspec, vspec, vspec, vspec],
        out_specs=[pl.BlockSpec((1, TC, W), tmap), sspec],
        out_shape=[jax.ShapeDtypeStruct((Bn, L, W), F32), jax.ShapeDtypeStruct((Bn, 1, W), F32)],
        scratch_shapes=[pltpu.VMEM((1, W), F32)],
        compiler_params=pltpu.CompilerParams(dimension_semantics=("parallel", "arbitrary")),
        name="rglru_scan",
    )(xc, h0, wr_bd, wi_bd, br, bi, sp)


def head_block_diag(w):
    D2, H, N, _ = w.shape
    return jnp.einsum('dhij,hg->dhigj', w, jnp.eye(H, dtype=w.dtype)).reshape(D2, H * N, H * N)


def mixer_rglru(f_c, f_l, p, need_ctx):
    (x_c, g_c), (x_l, g_l) = f_c, f_l
    xc_c = dwconv(x_c, p['lru_conv_w'], p['lru_conv_b'], CONV_W // 2).astype(F32)
    xc_l = dwconv(x_l, p['lru_conv_w'], p['lru_conv_b'], CONV_W // 2).astype(F32)
    h0 = jnp.zeros((xc_l.shape[0], 1, W_LRU), F32)
    wr, wi = head_block_diag(p['lru_wr']), head_block_diag(p['lru_wi'])
    sp = jax.nn.softplus(-p['lru_lam'])
    hs_c, hs_l = [], []
    for d in range(2):
        gates = (wr[d], wi[d], p['lru_br'][d][None], p['lru_bi'][d][None], sp[d][None])
        hc, st = lru_scan_dir(xc_c, h0, *gates, reverse=(d == 1))
        hl, _ = lru_scan_dir(xc_l, st, *gates, reverse=(d == 1))
        hs_c.append(hc)
        hs_l.append(hl)
    y_l = jax.nn.gelu(g_l.astype(F32)) * (hs_l[0] + hs_l[1])
    y_c = jax.nn.gelu(g_c.astype(F32)) * (hs_c[0] + hs_c[1]) if need_ctx else None
    return y_c, y_l


def rwkv_shift(f, p):
    r, k, v, zw, za, zg = [t.astype(F32) for t in f]
    mu = p['rwkv_mu']
    return (token_shift(r, mu[0]), token_shift(k, mu[1]), token_shift(v, mu[2]), zw, za, zg)


def rwkv_dir_inputs(f, p, d):
    r, k, v, zw, za, _ = f
    w_log = -jax.nn.softplus(-(p['rwkv_w0'][d] + jnp.tanh(zw) @ p['rwkv_w2'][d])) - 0.5
    decay = jnp.exp(-jnp.exp(w_log))
    a = jax.nn.sigmoid(p['rwkv_a0'][d] + za @ p['rwkv_a2'][d])
    kk = split_heads(k * p['rwkv_kk'], H_RWKV)
    kk = kk / jnp.maximum(jnp.linalg.norm(kk, axis=-1, keepdims=True), 1e-12)
    kd = k * (1 + (a - 1) * p['rwkv_ka'])
    return [split_heads(r, H_RWKV), split_heads(decay, H_RWKV), split_heads(kd, H_RWKV),
            split_heads(v, H_RWKV), kk, kk * split_heads(a, H_RWKV)]


RWKV_TC = HEAD_DIM
RWKV_PAIRS = 32


def _rwkv_body(kk_ref, v_ref, r_ref, w_ref, kka_ref, k_ref, y_ref, s_ref):
    NP, N = RWKV_PAIRS, HEAD_DIM
    rev = pl.program_id(0)

    @pl.when(pl.program_id(1) == 0)
    def _():
        s_ref[...] = jnp.zeros_like(s_ref)

    y_ref[...] = jnp.zeros_like(y_ref)
    lane = lax.broadcasted_iota(jnp.int32, (N, 2 * N), 1)
    row = lax.broadcasted_iota(jnp.int32, (N, 2 * N), 0)
    diag = (lane % N) == row
    seg4 = lax.broadcasted_iota(jnp.int32, (4 * N, 4 * N), 0) // N == \
        lax.broadcasted_iota(jnp.int32, (4 * N, 4 * N), 1) // N
    ones4 = jnp.where(seg4, 1.0, 0.0).astype(jnp.bfloat16)

    def segsum_bcast(x):
        H = x.shape[0] // 2
        xb = x.astype(jnp.bfloat16)
        lhs = jnp.concatenate([xb[:H].reshape(H * N, 2 * N), xb[H:].reshape(H * N, 2 * N)], axis=-1)
        out = jnp.dot(lhs, ones4, preferred_element_type=F32)
        return jnp.concatenate([out[:, :2 * N].reshape(H, N, 2 * N), out[:, 2 * N:].reshape(H, N, 2 * N)], axis=0)

    def rowv(ref, t):
        x = ref[:, pl.ds(t, 1), :] if len(ref.shape) == 3 else ref[0, :, pl.ds(t, 1), :]
        return jnp.concatenate([x[:, :, :2 * N], x[:, :, 2 * N:]], axis=0)

    def step(i, carry):
        t = i + rev * (RWKV_TC - 1 - 2 * i)
        tp = jnp.clip(t - 1 + 2 * rev, 0, RWKV_TC - 1)
        S = s_ref[...]
        R = segsum_bcast(jnp.concatenate([S * rowv(kk_ref, t), jnp.where(diag, rowv(v_ref, t), 0.0),
                                          S * rowv(r_ref, tp)], axis=0))
        y_ref[0, :, 0] = jnp.where(((lane % N) == tp) & (i > 0), R[2 * NP:], y_ref[0, :, 0])
        s_ref[...] = S * rowv(w_ref, t) - R[:NP] * rowv(kka_ref, t) + R[NP:2 * NP] * rowv(k_ref, t)
        return carry

    lax.fori_loop(0, RWKV_TC, step, 0)
    t_last = (RWKV_TC - 1) * (1 - rev)
    Y = segsum_bcast(s_ref[...] * rowv(r_ref, t_last))
    y_ref[0, :, 0] = jnp.where((lane % N) == t_last, Y, y_ref[0, :, 0])


def rwkv_scan_bidir(kk, v, r, w, kka, k, n_ctx_blocks):
    Bn, L, C = kk.shape
    TC, N = RWKV_TC, HEAD_DIM
    n = L // TC
    assert RWKV_PAIRS == Bn * C // (2 * N)

    def blk(d, c):
        back = jnp.where(c < n_ctx_blocks, n_ctx_blocks - 1 - c, n + n_ctx_blocks - 1 - c)
        return jnp.where(d == 0, c, back)

    shared = pl.BlockSpec((Bn, TC, C), lambda d, c: (0, blk(d, c), 0))
    per_dir = pl.BlockSpec((1, Bn, TC, C), lambda d, c: (d, 0, blk(d, c), 0))
    return pl.pallas_call(
        _rwkv_body,
        grid=(2, n),
        in_specs=[shared, shared, shared, per_dir, per_dir, per_dir],
        out_specs=pl.BlockSpec((1, RWKV_PAIRS, 1, N, 2 * N), lambda d, c: (d, 0, blk(d, c), 0, 0)),
        out_shape=jax.ShapeDtypeStruct((2, RWKV_PAIRS, n, N, 2 * N), F32),
        scratch_shapes=[pltpu.VMEM((RWKV_PAIRS, N, 2 * N), F32)],
        compiler_params=pltpu.CompilerParams(dimension_semantics=("parallel", "arbitrary"),
                                             vmem_limit_bytes=BIG_BLOCK_VMEM_LIMIT),
        name="rwkv7_scan",
    )(kk, v, r, w, kka, k)


def rwkv7_scan_bidir(ins_c, ins_l):
    Bn, Lc = ins_c[0][0].shape[:2]
    Ll = ins_l[0][0].shape[1]
    L = Lc + Ll
    cat = lambda d, j: jnp.concatenate([ins_c[d][j], ins_l[d][j]], axis=1).reshape(Bn, L, W_RWKV)
    per_dir = lambda j: jnp.stack([cat(0, j), cat(1, j)])
    yT = rwkv_scan_bidir(cat(0, 4), cat(0, 3), cat(0, 0), per_dir(1), per_dir(5), per_dir(2), Lc // RWKV_TC)
    halves = W_RWKV // (2 * HEAD_DIM)
    y = yT.reshape(2, halves, Bn, L // HEAD_DIM, HEAD_DIM, 2, HEAD_DIM)
    y = y.transpose(0, 2, 3, 6, 1, 5, 4).reshape(2, Bn, L, H_RWKV, HEAD_DIM)
    return ([y[d, :, :Lc] for d in range(2)], [y[d, :, Lc:] for d in range(2)])


def rwkv_bonus(ins, p):
    r, _, kd, v = ins[:4]
    return jnp.sum(r * kd * p['rwkv_rk'], axis=-1, keepdims=True) * v


def rwkv_out(y, bonus, zg, p):
    Bn, L, H, N = y.shape
    mu = jnp.mean(y, axis=-1, keepdims=True)
    var = jnp.mean(jnp.square(y - mu), axis=-1, keepdims=True)
    yn = ((y - mu) * lax.rsqrt(var + RWKV_GN_EPS)).reshape(Bn, L, H * N) * p['rwkv_ln_g'] + p['rwkv_ln_b']
    g = jax.nn.sigmoid(zg) @ p['rwkv_g2']
    return (yn + bonus.reshape(Bn, L, H * N)) * g


def mixer_rwkv(f_c, f_l, p, need_ctx):
    fc, fl = rwkv_shift(f_c, p), rwkv_shift(f_l, p)
    ins_c = [rwkv_dir_inputs(fc, p, d) for d in range(2)]
    ins_l = [rwkv_dir_inputs(fl, p, d) for d in range(2)]
    o_c, o_l = rwkv7_scan_bidir(ins_c, ins_l)
    y_c, y_l = [], []
    for d in range(2):
        y_l.append((o_l[d], rwkv_bonus(ins_l[d], p)))
        if need_ctx:
            y_c.append((o_c[d], rwkv_bonus(ins_c[d], p)))
    out_l = rwkv_out(y_l[0][0] + y_l[1][0], y_l[0][1] + y_l[1][1], fl[5], p)
    out_c = rwkv_out(y_c[0][0] + y_c[1][0], y_c[0][1] + y_c[1][1], fc[5], p) if need_ctx else None
    return out_c, out_l


def mlstm_prep(f, p):
    q, k, v, o, gz = f
    Bn, L, _ = q.shape
    qk = jax.nn.silu(dwconv(jnp.concatenate([q, k], axis=-1), p['mlstm_conv_w'], p['mlstm_conv_b'], CONV_W // 2)).astype(F32)
    q, k = jnp.split(qk, 2, axis=-1)
    gates = gz.astype(F32).reshape(Bn, L, 2, 2, H_MLSTM) + p['mlstm_gate_b']
    return q * HEAD_DIM ** -0.5, k, v.astype(F32), o.astype(F32), gates


MLSTM_ROWS = 2


def _mlstm_body(q_ref, k_ref, v_ref, gc_ref, gr_ref, h_ref, c_ref, n_ref, m_ref):
    T, N, H = MLSTM_CHUNK, HEAD_DIM, H_MLSTM
    rev = pl.program_id(0)

    @pl.when(pl.program_id(2) == 0)
    def _():
        c_ref[...] = jnp.zeros_like(c_ref)
        n_ref[...] = jnp.zeros_like(n_ref)
        m_ref[...] = jnp.zeros_like(m_ref)

    ti = lax.broadcasted_iota(jnp.int32, (T, T), 0)
    si = lax.broadcasted_iota(jnp.int32, (T, T), 1)
    mask = (si - ti) * (1 - 2 * rev) <= 0
    tri = jnp.where(mask, 1.0, 0.0)
    hp = lax.Precision.HIGHEST
    gcs = [gc_ref[0, r, 0] for r in range(MLSTM_ROWS)]
    grs = [gr_ref[0, r, 0] for r in range(MLSTM_ROWS)]
    b_cols = [jnp.dot(tri, g[:, H:], precision=hp, preferred_element_type=F32) for g in gcs]
    b_rows = [lax.dot_general(g[H:], tri, (((1,), (1,)), ((), ())), precision=hp, preferred_element_type=F32)
              for g in grs]
    b_tots = [jnp.sum(g[:, H:], axis=0, keepdims=True) for g in gcs]
    units = [(r, h) for r in range(MLSTM_ROWS) for h in range(H)]
    hs = range(len(units))
    sl = [slice(h * N, (h + 1) * N) for _, h in units]
    qh = [q_ref[units[h][0], :, sl[h]] for h in hs]
    kh = [k_ref[units[h][0], :, sl[h]] for h in hs]
    vh = [v_ref[units[h][0], :, sl[h]] for h in hs]
    C = [c_ref[h] for h in hs]
    n = [n_ref[h] for h in hs]
    m = [m_ref[h][:, 0:1] for h in hs]
    bc = [b_cols[r][:, h:h + 1] for r, h in units]
    br = [b_rows[r][h:h + 1, :] for r, h in units]
    ic = [gcs[r][:, h:h + 1] for r, h in units]
    ir = [grs[r][h:h + 1, :] for r, h in units]
    bT = [b_tots[r][:, h:h + 1] for r, h in units]
    nt = (((1,), (1,)), ((), ()))
    qk = [lax.dot_general(qh[h], kh[h], nt, preferred_element_type=F32) for h in hs]
    qc = [lax.dot_general(qh[h], C[h], nt, preferred_element_type=F32) for h in hs]
    logd = [jnp.where(mask, bc[h] - br[h] + ir[h], -jnp.inf) for h in hs]
    inter = [bc[h] + m[h] for h in hs]
    mt = [jnp.maximum(inter[h], jnp.max(logd[h], axis=-1, keepdims=True)) for h in hs]
    s = [qk[h] * jnp.exp(logd[h] - mt[h]) for h in hs]
    e_inter = [jnp.exp(inter[h] - mt[h]) for h in hs]
    num = [jnp.dot(s[h], vh[h], preferred_element_type=F32) + e_inter[h] * qc[h] for h in hs]
    den = [jnp.sum(s[h], axis=-1, keepdims=True) + e_inter[h] * jnp.sum(qh[h] * n[h], axis=-1, keepdims=True)
           for h in hs]
    out = [num[h] / jnp.maximum(jnp.abs(den[h]), jnp.exp(-mt[h])) for h in hs]
    for r in range(MLSTM_ROWS):
        h_ref[0, r] = jnp.concatenate(out[r * H:(r + 1) * H], axis=-1)
    m_new = [jnp.maximum(bT[h] + m[h], jnp.max(bT[h] - br[h] + ir[h], axis=-1, keepdims=True)) for h in hs]
    w_col = [jnp.exp(bT[h] - bc[h] + ic[h] - m_new[h]) for h in hs]
    dec = [jnp.exp(bT[h] + m[h] - m_new[h]) for h in hs]
    vk = [lax.dot_general(vh[h] * w_col[h], kh[h], (((0,), (0,)), ((), ())), preferred_element_type=F32) for h in hs]
    for h in hs:
        c_ref[h] = dec[h] * C[h] + vk[h]
        n_ref[h] = dec[h] * n[h] + jnp.sum(kh[h] * w_col[h], axis=0, keepdims=True)
        m_ref[h] = jnp.broadcast_to(m_new[h], (1, 128))


def mlstm_scan_bidir(q, k, v, gcol, grow, n_ctx_blocks):
    Bn, L, W = q.shape
    T = MLSTM_CHUNK
    n = L // T

    def blk(d, c):
        back = jnp.where(c < n_ctx_blocks, n_ctx_blocks - 1 - c, n + n_ctx_blocks - 1 - c)
        return jnp.where(d == 0, c, back)

    R = MLSTM_ROWS
    xspec = pl.BlockSpec((R, T, W), lambda d, b, c: (b, blk(d, c), 0))
    gcs = pl.BlockSpec((1, R, 1, T, 2 * H_MLSTM), lambda d, b, c: (d, b, blk(d, c), 0, 0))
    grs = pl.BlockSpec((1, R, 1, 2 * H_MLSTM, T), lambda d, b, c: (d, b, blk(d, c), 0, 0))
    return pl.pallas_call(
        _mlstm_body,
        grid=(2, Bn // R, n),
        in_specs=[xspec, xspec, xspec, gcs, grs],
        out_specs=pl.BlockSpec((1, R, T, W), lambda d, b, c: (d, b, blk(d, c), 0)),
        out_shape=jax.ShapeDtypeStruct((2, Bn, L, W), F32),
        scratch_shapes=[pltpu.VMEM((R * H_MLSTM, HEAD_DIM, HEAD_DIM), F32), pltpu.VMEM((R * H_MLSTM, 1, HEAD_DIM), F32),
                        pltpu.VMEM((R * H_MLSTM, 1, 128), F32)],
        compiler_params=pltpu.CompilerParams(dimension_semantics=("parallel", "parallel", "arbitrary")),
        name="mlstm_chunkwise",
    )(q, k, v, gcol, grow)


def mixer_mlstm(f_c, f_l, p, need_ctx):
    f_l = [to_colmajor(t) for t in f_l]
    qc, kc, vc, oc, gc = mlstm_prep(f_c, p)
    ql, kl, vl, ol, gl = mlstm_prep(f_l, p)
    Bn, Lc = qc.shape[:2]
    cat = lambda a, b: jnp.concatenate([a, b], axis=1)
    g = cat(gc, gl)
    L = g.shape[1]
    g = jnp.concatenate([g[:, :, :, 0], jax.nn.log_sigmoid(g[:, :, :, 1])], axis=-1)
    gcol = g.transpose(2, 0, 1, 3).reshape(2, Bn, L // MLSTM_CHUNK, MLSTM_CHUNK, 2 * H_MLSTM)
    h = mlstm_scan_bidir(cat(qc, ql), cat(kc, kl), cat(vc, vl), gcol, gcol.transpose(0, 1, 2, 4, 3),
                         Lc // MLSTM_CHUNK)
    hs = h[0] + h[1]
    y_l = from_colmajor(jax.nn.sigmoid(ol) * hs[:, Lc:])
    y_c = jax.nn.sigmoid(oc) * hs[:, :Lc] if need_ctx else None
    return y_c, y_l


def hyena_spectrum(L, p):
    pos = jnp.arange(L, dtype=F32)
    t = pos / (L - 1)
    bands = (HYENA_EMB - 1) // 2
    freqs = jnp.linspace(1e-4, bands - 1, bands, dtype=F32)
    ang = (2 * math.pi / L) * pos[:, None] * freqs[None, :]
    z = jnp.concatenate([t[:, None], jnp.cos(ang), -jnp.sin(ang)], axis=-1)
    h = jnp.sin(p['hy_freq'][0] * (z @ p['hy_w1'] + p['hy_b1']))
    h = jnp.sin(p['hy_freq'][1] * (h @ p['hy_w2'] + p['hy_b2']))
    h = (h @ p['hy_w3']).astype(F32).reshape(L, HYENA_ORDER, 2, W_HYENA)
    deltas = jnp.abs(jnp.linspace(math.log(HYENA_TARGET) / HYENA_SLOW, math.log(HYENA_TARGET) / HYENA_FAST,
                                  W_HYENA, dtype=F32))
    h = h * jnp.exp(-t[:, None, None, None] * deltas)
    fwd, bwd = h[:, :, 0], h[:, :, 1]
    two = jnp.concatenate([fwd, jnp.zeros_like(fwd[:1]), jnp.flip(bwd[1:], axis=0)], axis=0)
    two = two / (jnp.sum(jnp.abs(two), axis=0, keepdims=True) + EPS)
    return jnp.fft.rfft(two, axis=0)


def long_conv(u, spec, bias):
    L = u.shape[1]
    y = jnp.fft.irfft(jnp.fft.rfft(u, n=2 * L, axis=1) * spec, n=2 * L, axis=1)[:, :L]
    return y + u * bias


def mixer_hyena(f, p):
    u = dwconv(jnp.concatenate(f, axis=-1), p['hy_conv_w'], p['hy_conv_b'], HYENA_SHORT // 2).astype(F32)
    v, x1, x2 = jnp.split(u, 3, axis=-1)
    spec = hyena_spectrum(u.shape[1], p)
    z = x1 * long_conv(v, spec[:, 0], p['hy_bias'][0])
    return x2 * long_conv(z, spec[:, 1], p['hy_bias'][1])


PEER_SEL = PEER_HEADS * PEER_TOPK
PEER_HALF = D_MODEL // 2
PEER_CHUNK = 2048
PEER_FIRST_CHUNK = 512
PEER_TT = 16
SC_GATHER_ROWS = 64


def pack_bf16_halves(tab):
    bits = lax.bitcast_convert_type(tab.astype(jnp.bfloat16), jnp.uint16).astype(jnp.uint32)
    half = tab.shape[1] // 2
    return bits[:, :half] | (bits[:, half:] << 16)


def sc_gather_pair(u_tab, v_tab, idx):
    n = idx.shape[0]
    W = u_tab.shape[1]
    info = plsc.get_sparse_core_info()
    nc, ns = info.num_cores, info.num_subcores
    per_w = n // (nc * ns)
    G = SC_GATHER_ROWS
    assert per_w * nc * ns == n and per_w % G == 0
    nsteps = per_w // G
    mesh = plsc.VectorSubcoreMesh(core_axis_name="c", subcore_axis_name="s")
    out = jax.ShapeDtypeStruct((n, W), u_tab.dtype)

    def body(u_hbm, v_hbm, idx_hbm, uo_hbm, vo_hbm, idx_v, urows, vrows, usem, vsem):
        wid = lax.axis_index("s") * nc + lax.axis_index("c")
        base = wid * per_w
        pltpu.sync_copy(idx_hbm.at[pl.ds(base, per_w)], idx_v)

        @pl.loop(0, nsteps)
        def _(i):
            off = pl.multiple_of(i * G, G)
            ids = idx_v.at[pl.ds(off, G)]
            cu = pltpu.async_copy(u_hbm.at[ids], urows, usem)
            cv = pltpu.async_copy(v_hbm.at[ids], vrows, vsem)
            cu.wait()
            pltpu.sync_copy(urows, uo_hbm.at[pl.ds(base + off, G)])
            cv.wait()
            pltpu.sync_copy(vrows, vo_hbm.at[pl.ds(base + off, G)])

    fn = pl.kernel(body, out_type=(out, out), mesh=mesh,
                   scratch_types=[pltpu.VMEM((per_w,), jnp.int32),
                                  pltpu.VMEM((G, W), u_tab.dtype), pltpu.VMEM((G, W), u_tab.dtype),
                                  pltpu.SemaphoreType.DMA, pltpu.SemaphoreType.DMA],
                   name="peer_sc_gather")
    return fn(u_tab, v_tab, idx)


def _unpack_halves(w):
    lo = lax.bitcast_convert_type(w << 16, F32)
    hi = lax.bitcast_convert_type(w & jnp.uint32(0xFFFF0000), F32)
    return lo, hi


def _peer_apply_body(x_ref, g_ref, ug_ref, vg_ref, o_ref):
    TT = x_ref.shape[0]
    gpad = jnp.concatenate([g_ref[...], jnp.zeros((PEER_SEL - TT, PEER_SEL), F32)], axis=0)
    gT = gpad.T
    for t in range(TT):
        rows = pl.ds(t * PEER_SEL, PEER_SEL)
        ulo, uhi = _unpack_halves(ug_ref[rows, :])
        xl = x_ref[pl.ds(t, 1), 0:PEER_HALF]
        xh = x_ref[pl.ds(t, 1), PEER_HALF:D_MODEL]
        dots = jnp.sum(ulo * xl + uhi * xh, axis=-1, keepdims=True)
        w = jax.nn.gelu(dots) * gT[:, t:t + 1]
        vlo, vhi = _unpack_halves(vg_ref[rows, :])
        o_ref[pl.ds(t, 1), 0:PEER_HALF] = jnp.sum(vlo * w, axis=0, keepdims=True)
        o_ref[pl.ds(t, 1), PEER_HALF:D_MODEL] = jnp.sum(vhi * w, axis=0, keepdims=True)


def peer_apply(x, gate, ug, vg):
    T = x.shape[0]
    TT = PEER_TT
    return pl.pallas_call(
        _peer_apply_body,
        grid=(T // TT,),
        in_specs=[pl.BlockSpec((TT, D_MODEL), lambda i: (i, 0)),
                  pl.BlockSpec((TT, PEER_SEL), lambda i: (i, 0)),
                  pl.BlockSpec((TT * PEER_SEL, PEER_HALF), lambda i: (i, 0)),
                  pl.BlockSpec((TT * PEER_SEL, PEER_HALF), lambda i: (i, 0))],
        out_specs=pl.BlockSpec((TT, D_MODEL), lambda i: (i, 0)),
        out_shape=jax.ShapeDtypeStruct((T, D_MODEL), F32),
        compiler_params=pltpu.CompilerParams(dimension_semantics=("parallel",),
                                             vmem_limit_bytes=BIG_BLOCK_VMEM_LIMIT),
        name="peer_apply",
    )(x, gate, ug, vg)


ROUTE_TT = 1024
ROUTE_PAIR_ROWS = sum(-(-(PEER_TOPK // (i + 1)) // V7X_SUBLANES) * V7X_SUBLANES
                      for i in range(PEER_TOPK // 2)) + PEER_TOPK // 2


def _peer_route_body(s_ref, e_ref, g_ref, xs_ref, sv_ref, si_ref, cand_ref, cidx_ref, pf_ref, ts_ref):
    K, NK, TT = PEER_TOPK, PEER_NKEYS, s_ref.shape[-1]
    NEG = -jnp.inf
    xs_ref[...] = s_ref[0]
    kiota = lax.broadcasted_iota(jnp.int32, (NK, TT), 0).astype(F32)

    def half_topk(it, carry):
        for c in range(2):
            x = xs_ref[c]
            m = jnp.max(x, axis=0, keepdims=True)
            idx = jnp.min(jnp.where(x == m, kiota, float(NK)), axis=0, keepdims=True)
            xs_ref[c] = jnp.where(kiota == idx, NEG, x)
            sv_ref[c, pl.ds(it, 1), :] = m
            si_ref[c, pl.ds(it, 1), :] = idx
        return carry

    lax.fori_loop(0, K, half_topk, 0)

    jiota = lambda rows: lax.broadcasted_iota(jnp.int32, (rows, TT), 0).astype(F32)
    off = 0
    for i in range(K // 2):
        n = K // (i + 1)
        rows = -(-n // V7X_SUBLANES) * V7X_SUBLANES
        ok = jiota(rows) < float(n)
        cand_ref[pl.ds(off, rows), :] = jnp.where(ok, sv_ref[0, i:i + 1, :] + sv_ref[1, 0:rows, :], NEG)
        cidx_ref[pl.ds(off, rows), :] = si_ref[0, i:i + 1, :] * float(NK) + si_ref[1, 0:rows, :]
        pf_ref[pl.ds(off, rows), :] = jnp.where(ok, float(i * K) + jiota(rows), float(K * K))
        off += rows
    cand_ref[pl.ds(off, K // 2), :] = sv_ref[0, K // 2:K, :] + sv_ref[1, 0:1, :]
    cidx_ref[pl.ds(off, K // 2), :] = si_ref[0, K // 2:K, :] * float(NK) + si_ref[1, 0:1, :]
    pf_ref[pl.ds(off, K // 2), :] = (float(K // 2) + jiota(K // 2)) * float(K)
    piota = pf_ref[...]

    def pair_topk(it, carry):
        x = cand_ref[...]
        m = jnp.max(x, axis=0, keepdims=True)
        pos = jnp.min(jnp.where(x == m, piota, float(K * K)), axis=0, keepdims=True)
        sel = piota == pos
        cand_ref[...] = jnp.where(sel, NEG, x)
        ts_ref[pl.ds(it, 1), :] = m
        e_ref[0, pl.ds(it, 1), :] = jnp.max(jnp.where(sel, cidx_ref[...], -1.0), axis=0,
                                            keepdims=True).astype(jnp.int32)
        return carry

    lax.fori_loop(0, K, pair_topk, 0)
    ts = ts_ref[...]
    ex = jnp.exp(ts - ts[0:1, :])
    g_ref[0] = ex / jnp.sum(ex, axis=0, keepdims=True)


def peer_route_topk(sT):
    H, _, NK, N = sT.shape
    TT, K = ROUTE_TT, PEER_TOPK
    out_spec = pl.BlockSpec((1, K, TT), lambda h, i: (h, 0, i))
    return pl.pallas_call(
        _peer_route_body,
        grid=(H, N // TT),
        in_specs=[pl.BlockSpec((1, 2, NK, TT), lambda h, i: (h, 0, 0, i))],
        out_specs=[out_spec, out_spec],
        out_shape=[jax.ShapeDtypeStruct((H, K, N), jnp.int32), jax.ShapeDtypeStruct((H, K, N), F32)],
        scratch_shapes=[pltpu.VMEM((2, NK, TT), F32), pltpu.VMEM((2, K, TT), F32), pltpu.VMEM((2, K, TT), F32),
                        pltpu.VMEM((ROUTE_PAIR_ROWS, TT), F32), pltpu.VMEM((ROUTE_PAIR_ROWS, TT), F32),
                        pltpu.VMEM((ROUTE_PAIR_ROWS, TT), F32), pltpu.VMEM((K, TT), F32)],
        compiler_params=pltpu.CompilerParams(dimension_semantics=("parallel", "parallel")),
        name="peer_route_topk",
    )(sT)


def peer_route(xf, p):
    N = xf.shape[0]
    wq, keys = p['peer_wq'], p['peer_keys'].astype(F32)
    q = (xf @ wq).astype(F32).reshape(N, PEER_HEADS, 2, PEER_DQ // 2)
    sT = jnp.einsum('thcq,hckq->hckt', q, keys)
    eT, gT = peer_route_topk(sT)
    return (eT.transpose(2, 0, 1).reshape(N, PEER_SEL), gT.transpose(2, 0, 1).reshape(N, PEER_SEL))


def peer_ffn(h, p):
    Bn, L, D = h.shape
    N = Bn * L
    xf = h.reshape(N, D)
    eidx, gate = peer_route(xf, p)
    u_pk, v_pk = p['peer_u_pk'], p['peer_v_pk']
    outs, c0, size = [], 0, PEER_FIRST_CHUNK
    while c0 < N:
        rows = slice(c0, min(N, c0 + size))
        ug, vg = sc_gather_pair(u_pk, v_pk, eidx[rows].reshape(-1))
        outs.append(peer_apply(xf[rows], gate[rows], ug, vg))
        c0, size = rows.stop, min(2 * size, PEER_CHUNK)
    return jnp.concatenate(outs, axis=0).reshape(Bn, L, D)


def merge_groups(ys, g, dtype):
    outs, off = [], 0
    for y, w in zip(ys, GROUP_WIDTHS):
        outs.append(rmsnorm(y, g[off:off + w]).astype(dtype))
        off += w
    return jnp.concatenate(outs, axis=-1)


def token_mixers(h_c, h_l, p, need_ctx):
    zc = split_cols(h_c @ p['w_in'])
    zl = split_cols(h_l @ p['w_in'])
    a_c, a_l = mixer_rglru(zc[0:2], zl[0:2], p, need_ctx)
    b_c, b_l = mixer_rwkv(zc[2:8], zl[2:8], p, need_ctx)
    m_c, m_l = mixer_mlstm(zc[8:13], zl[8:13], p, need_ctx)
    d_l = mixer_hyena(zl[13:16], p)
    o_l = merge_groups([a_l, b_l, m_l, d_l], p['grp_g'], h_l.dtype) @ p['w_out']
    if not need_ctx:
        return None, o_l
    d_c = mixer_hyena(zc[13:16], p)
    o_c = merge_groups([a_c, b_c, m_c, d_c], p['grp_g'], h_c.dtype) @ p['w_out']
    return o_c, o_l


def trunk_layer(x_l, x_c, c, c_ctx, p, need_ctx):
    Bn = c.shape[0]
    mod_l = (jax.nn.silu(c) @ p['ada_w'] + p['ada_b']).reshape(Bn, 6, 1, D_MODEL)
    mod_c = (jax.nn.silu(c_ctx) @ p['ada_w'] + p['ada_b']).reshape(6, 1, 1, D_MODEL)
    h_l = modulate(rmsnorm(x_l, p['norm1_g']), mod_l[:, 0], mod_l[:, 1])
    h_c = modulate(rmsnorm(x_c, p['norm1_g']), mod_c[0], mod_c[1])
    o_c, o_l = token_mixers(h_c, h_l, p, need_ctx)
    x_l = x_l + mod_l[:, 2] * o_l
    x_l = x_l + mod_l[:, 5] * peer_ffn(modulate(rmsnorm(x_l, p['norm2_g']), mod_l[:, 3], mod_l[:, 4]), p)
    if need_ctx:
        x_c = x_c + mod_c[2] * o_c
        x_c = x_c + mod_c[5] * peer_ffn(modulate(rmsnorm(x_c, p['norm2_g']), mod_c[3], mod_c[4]), p)
    return x_l, x_c


def _final_norm_body(x_ref, g_ref, o_ref):
    xf = x_ref[...]
    y = xf * lax.rsqrt(jnp.mean(xf * xf, axis=-1, keepdims=True) + EPS)
    o_ref[...] = y * g_ref[...]


def final_rmsnorm(x, g):
    Bn, L, D = x.shape
    rows = Bn * L
    tile = 1024
    out = pl.pallas_call(
        _final_norm_body,
        grid=(rows // tile,),
        in_specs=[pl.BlockSpec((tile, D), lambda i: (i, 0)), pl.BlockSpec((1, D), lambda i: (0, 0))],
        out_specs=pl.BlockSpec((tile, D), lambda i: (i, 0)),
        out_shape=jax.ShapeDtypeStruct((rows, D), x.dtype),
        name="final_rmsnorm",
    )(x.reshape(rows, D), g.reshape(1, D))
    return out.reshape(Bn, L, D)


def kernel(x, c, ctx, c_ctx, ada_w, ada_b, norm1_g, norm2_g, w_in, w_out, grp_g,
           lru_conv_w, lru_conv_b, lru_wr, lru_br, lru_wi, lru_bi, lru_lam,
           rwkv_mu, rwkv_w0, rwkv_w2, rwkv_a0, rwkv_a2, rwkv_g2, rwkv_kk, rwkv_ka, rwkv_rk, rwkv_ln_g, rwkv_ln_b,
           mlstm_conv_w, mlstm_conv_b, mlstm_gate_b,
           hy_conv_w, hy_conv_b, hy_w1, hy_b1, hy_w2, hy_b2, hy_w3, hy_freq, hy_bias,
           peer_wq, peer_keys, peer_u, peer_v, final_g):
    x_l, x_c = x, ctx
    for i in range(DEPTH):
        p = dict(ada_w=ada_w[i], ada_b=ada_b[i], norm1_g=norm1_g[i], norm2_g=norm2_g[i], w_in=w_in[i],
                 w_out=w_out[i], grp_g=grp_g[i],
                 lru_conv_w=lru_conv_w[i], lru_conv_b=lru_conv_b[i], lru_wr=lru_wr[i], lru_br=lru_br[i],
                 lru_wi=lru_wi[i], lru_bi=lru_bi[i], lru_lam=lru_lam[i],
                 rwkv_mu=rwkv_mu[i], rwkv_w0=rwkv_w0[i], rwkv_w2=rwkv_w2[i], rwkv_a0=rwkv_a0[i],
                 rwkv_a2=rwkv_a2[i], rwkv_g2=rwkv_g2[i], rwkv_kk=rwkv_kk[i], rwkv_ka=rwkv_ka[i],
                 rwkv_rk=rwkv_rk[i], rwkv_ln_g=rwkv_ln_g[i], rwkv_ln_b=rwkv_ln_b[i],
                 mlstm_conv_w=mlstm_conv_w[i], mlstm_conv_b=mlstm_conv_b[i], mlstm_gate_b=mlstm_gate_b[i],
                 hy_conv_w=hy_conv_w[i], hy_conv_b=hy_conv_b[i], hy_w1=hy_w1[i], hy_b1=hy_b1[i],
                 hy_w2=hy_w2[i], hy_b2=hy_b2[i], hy_w3=hy_w3[i], hy_freq=hy_freq[i], hy_bias=hy_bias[i],
                 peer_wq=peer_wq[i], peer_keys=peer_keys[i],
                 peer_u_pk=pack_bf16_halves(peer_u[i]), peer_v_pk=pack_bf16_halves(peer_v[i]))
        x_l, x_c = trunk_layer(x_l, x_c, c, c_ctx, p, i < DEPTH - 1)
    return final_rmsnorm(x_l, final_g)
```

```python
import functools
import math
import jax, jax.numpy as jnp
from jax import lax
import numpy as np
from jax.experimental import pallas as pl
from jax.experimental.pallas import tpu as pltpu
from jax.experimental.pallas import tpu_sc as plsc

D_MODEL = 1024
DEPTH = 2

GRID_W = 64
F32 = jnp.float32
V7X_VMEM_BYTES = 64 * 1024 * 1024
V7X_SUBLANES = 8
BIG_BLOCK_VMEM_LIMIT = V7X_VMEM_BYTES * 5 // 8
EPS = 1e-6
HEAD_DIM = 64
D_MIX = D_MODEL
W_LRU = D_MIX // 4
W_RWKV = D_MIX // 4
W_MLSTM = D_MIX // 4
W_HYENA = D_MIX - W_LRU - W_RWKV - W_MLSTM
GROUP_WIDTHS = (W_LRU, W_RWKV, W_MLSTM, W_HYENA)
H_RWKV = W_RWKV // HEAD_DIM
H_MLSTM = W_MLSTM // HEAD_DIM
CONV_W = 4
LRU_C = 8.0
RWKV_LORA_W = 32
RWKV_LORA_A = 32
RWKV_LORA_G = 64
RWKV_GN_EPS = 64e-5
MLSTM_CHUNK = 64
HYENA_ORDER = 2
HYENA_SHORT = 3
HYENA_EMB = 33
HYENA_TARGET = 1e-2
HYENA_FAST = 0.3
HYENA_SLOW = 1.5
PEER_HEADS = 8
PEER_NKEYS = 128
PEER_TOPK = 16
PEER_DQ = 256
IN_SPLITS = (W_LRU, W_LRU, W_RWKV, W_RWKV, W_RWKV, RWKV_LORA_W, RWKV_LORA_A, RWKV_LORA_G, W_MLSTM, W_MLSTM, W_MLSTM, W_MLSTM, 4 * H_MLSTM, W_HYENA, W_HYENA, W_HYENA)


def rmsnorm(x, g):
    xf = x.astype(F32)
    y = xf * lax.rsqrt(jnp.mean(xf * xf, axis=-1, keepdims=True) + EPS)
    return (y * g.astype(F32)).astype(x.dtype)


def modulate(h, shift, scale):
    return h * (1 + scale) + shift


def split_cols(z):
    offs = np.cumsum(IN_SPLITS)[:-1].tolist()
    return jnp.split(z, offs, axis=-1)


def dwconv(x, w, b, pad_left):
    K = w.shape[0]
    y = lax.conv_general_dilated(x, w[:, None, :].astype(x.dtype), (1,), [(pad_left, K - 1 - pad_left)],
                                 dimension_numbers=('NWC', 'WIO', 'NWC'), feature_group_count=x.shape[-1])
    return y + b.astype(x.dtype)


def token_shift(x, mu):
    prev = jnp.pad(x, ((0, 0), (1, 0), (0, 0)))[:, :-1]
    nxt = jnp.pad(x, ((0, 0), (0, 1), (0, 0)))[:, 1:]
    return x + mu[0] * (prev - x) + mu[1] * (nxt - x)


def split_heads(t, h):
    Bn, L, W = t.shape
    return t.reshape(Bn, L, h, W // h)


def to_colmajor(a):
    Bn, L, C = a.shape
    rows = L // GRID_W
    return a.reshape(Bn, rows, GRID_W, C).transpose(0, 2, 1, 3).reshape(Bn, L, C)


def from_colmajor(a):
    Bn, L, C = a.shape
    rows = L // GRID_W
    return a.reshape(Bn, GRID_W, rows, C).transpose(0, 2, 1, 3).reshape(Bn, L, C)


LRU_TC = 256


def _expm1(z):
    u = jnp.exp(z)
    return jnp.where(u == 1.0, z, (u - 1.0) * z / jnp.where(u == 1.0, 1.0, jnp.log(u)))


def _lru_body(x_ref, h0_ref, wr_ref, wi_ref, br_ref, bi_ref, sp_ref, h_ref, hT_ref, st_ref, *, reverse):
    TC, W = x_ref.shape[1], x_ref.shape[2]

    @pl.when(pl.program_id(1) == 0)
    def _():
        st_ref[...] = h0_ref[0]

    x = x_ref[0]
    r = jax.nn.sigmoid(jnp.dot(x, wr_ref[...], preferred_element_type=F32) + br_ref[...])
    i = jax.nn.sigmoid(jnp.dot(x, wi_ref[...], preferred_element_type=F32) + bi_ref[...])
    log_a = -LRU_C * r * sp_ref[...]
    a = jnp.exp(log_a)
    b = jnp.sqrt(-_expm1(2.0 * log_a)) * (i * x)
    row = lax.broadcasted_iota(jnp.int32, (TC, W), 0)
    s = 1
    while s < TC:
        if reverse:
            keep = row < TC - s
            a_sh = jnp.where(keep, pltpu.roll(a, TC - s, 0), 1.0)
            b_sh = jnp.where(keep, pltpu.roll(b, TC - s, 0), 0.0)
        else:
            keep = row >= s
            a_sh = jnp.where(keep, pltpu.roll(a, s, 0), 1.0)
            b_sh = jnp.where(keep, pltpu.roll(b, s, 0), 0.0)
        b = b + a * b_sh
        a = a * a_sh
        s *= 2
    h = b + a * st_ref[...]
    h_ref[0] = h
    last = h[0:1, :] if reverse else h[TC - 1:TC, :]
    st_ref[...] = last
    hT_ref[0] = last


def lru_scan_dir(xc, h0, wr_bd, wi_bd, br, bi, sp, reverse):
    Bn, L, W = xc.shape
    TC = min(LRU_TC, L)
    n = L // TC
    tmap = (lambda b, c: (b, n - 1 - c, 0)) if reverse else (lambda b, c: (b, c, 0))
    wspec = pl.BlockSpec((W, W), lambda b, c: (0, 0))
    vspec = pl.BlockSpec((1, W), lambda b, c: (0, 0))
    sspec = pl.BlockSpec((1, 1, W), lambda b, c: (b, 0, 0))
    return pl.pallas_call(
        functools.partial(_lru_body, reverse=reverse),
        grid=(Bn, n),
        in_specs=[pl.BlockSpec((1, TC, W), tmap), sspec, wspec, wspec, vspec, vspec, vspec],
        out_specs=[pl.BlockSpec((1, TC, W), tmap), sspec],
        out_shape=[jax.ShapeDtypeStruct((Bn, L, W), F32), jax.ShapeDtypeStruct((Bn, 1, W), F32)],
        scratch_shapes=[pltpu.VMEM((1, W), F32)],
        compiler_params=pltpu.CompilerParams(dimension_semantics=("parallel", "arbitrary")),
        name="rglru_scan",
    )(xc, h0, wr_bd, wi_bd, br, bi, sp)


def head_block_diag(w):
    D2, H, N, _ = w.shape
    return jnp.einsum('dhij,hg->dhigj', w, jnp.eye(H, dtype=w.dtype)).reshape(D2, H * N, H * N)


def mixer_rglru(f_c, f_l, p, need_ctx):
    (x_c, g_c), (x_l, g_l) = f_c, f_l
    xc_c = dwconv(x_c, p['lru_conv_w'], p['lru_conv_b'], CONV_W // 2).astype(F32)
    xc_l = dwconv(x_l, p['lru_conv_w'], p['lru_conv_b'], CONV_W // 2).astype(F32)
    h0 = jnp.zeros((xc_l.shape[0], 1, W_LRU), F32)
    wr, wi = head_block_diag(p['lru_wr']), head_block_diag(p['lru_wi'])
    sp = jax.nn.softplus(-p['lru_lam'])
    hs_c, hs_l = [], []
    for d in range(2):
        gates = (wr[d], wi[d], p['lru_br'][d][None], p['lru_bi'][d][None], sp[d][None])
        hc, st = lru_scan_dir(xc_c, h0, *gates, reverse=(d == 1))
        hl, _ = lru_scan_dir(xc_l, st, *gates, reverse=(d == 1))
        hs_c.append(hc)
        hs_l.append(hl)
    y_l = jax.nn.gelu(g_l.astype(F32)) * (hs_l[0] + hs_l[1])
    y_c = jax.nn.gelu(g_c.astype(F32)) * (hs_c[0] + hs_c[1]) if need_ctx else None
    return y_c, y_l


def rwkv_shift(f, p):
    r, k, v, zw, za, zg = [t.astype(F32) for t in f]
    mu = p['rwkv_mu']
    return (token_shift(r, mu[0]), token_shift(k, mu[1]), token_shift(v, mu[2]), zw, za, zg)


def rwkv_dir_inputs(f, p, d):
    r, k, v, zw, za, _ = f
    w_log = -jax.nn.softplus(-(p['rwkv_w0'][d] + jnp.tanh(zw) @ p['rwkv_w2'][d])) - 0.5
    decay = jnp.exp(-jnp.exp(w_log))
    a = jax.nn.sigmoid(p['rwkv_a0'][d] + za @ p['rwkv_a2'][d])
    kk = split_heads(k * p['rwkv_kk'], H_RWKV)
    kk = kk / jnp.maximum(jnp.linalg.norm(kk, axis=-1, keepdims=True), 1e-12)
    kd = k * (1 + (a - 1) * p['rwkv_ka'])
    return [split_heads(r, H_RWKV), split_heads(decay, H_RWKV), split_heads(kd, H_RWKV),
            split_heads(v, H_RWKV), kk, kk * split_heads(a, H_RWKV)]


RWKV_TC = HEAD_DIM
RWKV_PAIRS = 32


def _rwkv_body(kk_ref, v_ref, r_ref, w_ref, kka_ref, k_ref, y_ref, s_ref):
    NP, N = RWKV_PAIRS, HEAD_DIM
    rev = pl.program_id(0)

    @pl.when(pl.program_id(1) == 0)
    def _():
        s_ref[...] = jnp.zeros_like(s_ref)

    y_ref[...] = jnp.zeros_like(y_ref)
    lane = lax.broadcasted_iota(jnp.int32, (N, 2 * N), 1)
    row = lax.broadcasted_iota(jnp.int32, (N, 2 * N), 0)
    diag = (lane % N) == row
    seg4 = lax.broadcasted_iota(jnp.int32, (4 * N, 4 * N), 0) // N == \
        lax.broadcasted_iota(jnp.int32, (4 * N, 4 * N), 1) // N
    ones4 = jnp.where(seg4, 1.0, 0.0).astype(jnp.bfloat16)

    def segsum_bcast(x):
        H = x.shape[0] // 2
        xb = x.astype(jnp.bfloat16)
        lhs = jnp.concatenate([xb[:H].reshape(H * N, 2 * N), xb[H:].reshape(H * N, 2 * N)], axis=-1)
        out = jnp.dot(lhs, ones4, preferred_element_type=F32)
        return jnp.concatenate([out[:, :2 * N].reshape(H, N, 2 * N), out[:, 2 * N:].reshape(H, N, 2 * N)], axis=0)

    def rowv(ref, t):
        x = ref[:, pl.ds(t, 1), :] if len(ref.shape) == 3 else ref[0, :, pl.ds(t, 1), :]
        return jnp.concatenate([x[:, :, :2 * N], x[:, :, 2 * N:]], axis=0)

    def step(i, carry):
        t = i + rev * (RWKV_TC - 1 - 2 * i)
        tp = jnp.clip(t - 1 + 2 * rev, 0, RWKV_TC - 1)
        S = s_ref[...]
        R = segsum_bcast(jnp.concatenate([S * rowv(kk_ref, t), jnp.where(diag, rowv(v_ref, t), 0.0),
                                          S * rowv(r_ref, tp)], axis=0))
        y_ref[0, :, 0] = jnp.where(((lane % N) == tp) & (i > 0), R[2 * NP:], y_ref[0, :, 0])
        s_ref[...] = S * rowv(w_ref, t) - R[:NP] * rowv(kka_ref, t) + R[NP:2 * NP] * rowv(k_ref, t)
        return carry

    lax.fori_loop(0, RWKV_TC, step, 0)
    t_last = (RWKV_TC - 1) * (1 - rev)
    Y = segsum_bcast(s_ref[...] * rowv(r_ref, t_last))
    y_ref[0, :, 0] = jnp.where((lane % N) == t_last, Y, y_ref[0, :, 0])


def rwkv_scan_bidir(kk, v, r, w, kka, k, n_ctx_blocks):
    Bn, L, C = kk.shape
    TC, N = RWKV_TC, HEAD_DIM
    n = L // TC
    assert RWKV_PAIRS == Bn * C // (2 * N)

    def blk(d, c):
        back = jnp.where(c < n_ctx_blocks, n_ctx_blocks - 1 - c, n + n_ctx_blocks - 1 - c)
        return jnp.where(d == 0, c, back)

    shared = pl.BlockSpec((Bn, TC, C), lambda d, c: (0, blk(d, c), 0))
    per_dir = pl.BlockSpec((1, Bn, TC, C), lambda d, c: (d, 0, blk(d, c), 0))
    return pl.pallas_call(
        _rwkv_body,
        grid=(2, n),
        in_specs=[shared, shared, shared, per_dir, per_dir, per_dir],
        out_specs=pl.BlockSpec((1, RWKV_PAIRS, 1, N, 2 * N), lambda d, c: (d, 0, blk(d, c), 0, 0)),
        out_shape=jax.ShapeDtypeStruct((2, RWKV_PAIRS, n, N, 2 * N), F32),
        scratch_shapes=[pltpu.VMEM((RWKV_PAIRS, N, 2 * N), F32)],
        compiler_params=pltpu.CompilerParams(dimension_semantics=("parallel", "arbitrary"),
                                             vmem_limit_bytes=BIG_BLOCK_VMEM_LIMIT),
        name="rwkv7_scan",
    )(kk, v, r, w, kka, k)


def rwkv7_scan_bidir(ins, Lc):
    Bn, L = ins[0][0].shape[:2]
    flat = lambda d, j: ins[d][j].reshape(Bn, L, W_RWKV)
    per_dir = lambda j: jnp.stack([flat(0, j), flat(1, j)])
    yT = rwkv_scan_bidir(flat(0, 4), flat(0, 3), flat(0, 0), per_dir(1), per_dir(5), per_dir(2), Lc // RWKV_TC)
    halves = W_RWKV // (2 * HEAD_DIM)
    y = yT.reshape(2, halves, Bn, L // HEAD_DIM, HEAD_DIM, 2, HEAD_DIM)
    return y.transpose(0, 2, 3, 6, 1, 5, 4).reshape(2, Bn, L, H_RWKV, HEAD_DIM)


def rwkv_bonus(ins, p):
    r, _, kd, v = ins[:4]
    return jnp.sum(r * kd * p['rwkv_rk'], axis=-1, keepdims=True) * v


def rwkv_out(y, bonus, zg, p):
    Bn, L, H, N = y.shape
    mu = jnp.mean(y, axis=-1, keepdims=True)
    var = jnp.mean(jnp.square(y - mu), axis=-1, keepdims=True)
    yn = ((y - mu) * lax.rsqrt(var + RWKV_GN_EPS)).reshape(Bn, L, H * N) * p['rwkv_ln_g'] + p['rwkv_ln_b']
    g = jax.nn.sigmoid(zg) @ p['rwkv_g2']
    return (yn + bonus.reshape(Bn, L, H * N)) * g


def mixer_rwkv(f_c, f_l, p, need_ctx):
    fc, fl = rwkv_shift(f_c, p), rwkv_shift(f_l, p)
    Lc = fc[0].shape[1]
    f = tuple(jnp.concatenate([a, b], axis=1) for a, b in zip(fc, fl))
    ins = [rwkv_dir_inputs(f, p, d) for d in range(2)]
    y = rwkv7_scan_bidir(ins, Lc)
    out = rwkv_out(y[0] + y[1], rwkv_bonus(ins[0], p) + rwkv_bonus(ins[1], p), f[5], p)
    return (out[:, :Lc] if need_ctx else None), out[:, Lc:]


def mlstm_prep(f, p):
    q, k, v, o, gz = f
    Bn, L, _ = q.shape
    qk = jax.nn.silu(dwconv(jnp.concatenate([q, k], axis=-1), p['mlstm_conv_w'], p['mlstm_conv_b'], CONV_W // 2)).astype(F32)
    q, k = jnp.split(qk, 2, axis=-1)
    gates = gz.astype(F32).reshape(Bn, L, 2, 2, H_MLSTM) + p['mlstm_gate_b']
    return q * HEAD_DIM ** -0.5, k, v.astype(F32), o.astype(F32), gates


MLSTM_ROWS = 2


def _mlstm_body(q_ref, k_ref, v_ref, gc_ref, gr_ref, h_ref, c_ref, n_ref, m_ref):
    T, N, H = MLSTM_CHUNK, HEAD_DIM, H_MLSTM
    rev = pl.program_id(0)

    @pl.when(pl.program_id(2) == 0)
    def _():
        c_ref[...] = jnp.zeros_like(c_ref)
        n_ref[...] = jnp.zeros_like(n_ref)
        m_ref[...] = jnp.zeros_like(m_ref)

    ti = lax.broadcasted_iota(jnp.int32, (T, T), 0)
    si = lax.broadcasted_iota(jnp.int32, (T, T), 1)
    mask = (si - ti) * (1 - 2 * rev) <= 0
    tri = jnp.where(mask, 1.0, 0.0)
    hp = lax.Precision.HIGHEST
    gcs = [gc_ref[0, r, 0] for r in range(MLSTM_ROWS)]
    grs = [gr_ref[0, r, 0] for r in range(MLSTM_ROWS)]
    b_cols = [jnp.dot(tri, g[:, H:], precision=hp, preferred_element_type=F32) for g in gcs]
    b_rows = [lax.dot_general(g[H:], tri, (((1,), (1,)), ((), ())), precision=hp, preferred_element_type=F32)
              for g in grs]
    b_tots = [jnp.sum(g[:, H:], axis=0, keepdims=True) for g in gcs]
    units = [(r, h) for r in range(MLSTM_ROWS) for h in range(H)]
    hs = range(len(units))
    sl = [slice(h * N, (h + 1) * N) for _, h in units]
    qh = [q_ref[units[h][0], :, sl[h]] for h in hs]
    kh = [k_ref[units[h][0], :, sl[h]] for h in hs]
    vh = [v_ref[units[h][0], :, sl[h]] for h in hs]
    C = [c_ref[h] for h in hs]
    n = [n_ref[h] for h in hs]
    m = [m_ref[h][:, 0:1] for h in hs]
    bc = [b_cols[r][:, h:h + 1] for r, h in units]
    br = [b_rows[r][h:h + 1, :] for r, h in units]
    ic = [gcs[r][:, h:h + 1] for r, h in units]
    ir = [grs[r][h:h + 1, :] for r, h in units]
    bT = [b_tots[r][:, h:h + 1] for r, h in units]
    nt = (((1,), (1,)), ((), ()))
    qk = [lax.dot_general(qh[h], kh[h], nt, preferred_element_type=F32) for h in hs]
    qc = [lax.dot_general(qh[h], C[h], nt, preferred_element_type=F32) for h in hs]
    logd = [jnp.where(mask, bc[h] - br[h] + ir[h], -jnp.inf) for h in hs]
    inter = [bc[h] + m[h] for h in hs]
    mt = [jnp.maximum(inter[h], jnp.max(logd[h], axis=-1, keepdims=True)) for h in hs]
    s = [qk[h] * jnp.exp(logd[h] - mt[h]) for h in hs]
    e_inter = [jnp.exp(inter[h] - mt[h]) for h in hs]
    num = [jnp.dot(s[h], vh[h], preferred_element_type=F32) + e_inter[h] * qc[h] for h in hs]
    den = [jnp.sum(s[h], axis=-1, keepdims=True) + e_inter[h] * jnp.sum(qh[h] * n[h], axis=-1, keepdims=True)
           for h in hs]
    out = [num[h] / jnp.maximum(jnp.abs(den[h]), jnp.exp(-mt[h])) for h in hs]
    for r in range(MLSTM_ROWS):
        h_ref[0, r] = jnp.concatenate(out[r * H:(r + 1) * H], axis=-1)
    m_new = [jnp.maximum(bT[h] + m[h], jnp.max(bT[h] - br[h] + ir[h], axis=-1, keepdims=True)) for h in hs]
    w_col = [jnp.exp(bT[h] - bc[h] + ic[h] - m_new[h]) for h in hs]
    dec = [jnp.exp(bT[h] + m[h] - m_new[h]) for h in hs]
    vk = [lax.dot_general(vh[h] * w_col[h], kh[h], (((0,), (0,)), ((), ())), preferred_element_type=F32) for h in hs]
    for h in hs:
        c_ref[h] = dec[h] * C[h] + vk[h]
        n_ref[h] = dec[h] * n[h] + jnp.sum(kh[h] * w_col[h], axis=0, keepdims=True)
        m_ref[h] = jnp.broadcast_to(m_new[h], (1, 128))


def mlstm_scan_bidir(q, k, v, gcol, grow, n_ctx_blocks):
    Bn, L, W = q.shape
    T = MLSTM_CHUNK
    n = L // T

    def blk(d, c):
        back = jnp.where(c < n_ctx_blocks, n_ctx_blocks - 1 - c, n + n_ctx_blocks - 1 - c)
        return jnp.where(d == 0, c, back)

    R = MLSTM_ROWS
    xspec = pl.BlockSpec((R, T, W), lambda d, b, c: (b, blk(d, c), 0))
    gcs = pl.BlockSpec((1, R, 1, T, 2 * H_MLSTM), lambda d, b, c: (d, b, blk(d, c), 0, 0))
    grs = pl.BlockSpec((1, R, 1, 2 * H_MLSTM, T), lambda d, b, c: (d, b, blk(d, c), 0, 0))
    return pl.pallas_call(
        _mlstm_body,
        grid=(2, Bn // R, n),
        in_specs=[xspec, xspec, xspec, gcs, grs],
        out_specs=pl.BlockSpec((1, R, T, W), lambda d, b, c: (d, b, blk(d, c), 0)),
        out_shape=jax.ShapeDtypeStruct((2, Bn, L, W), F32),
        scratch_shapes=[pltpu.VMEM((R * H_MLSTM, HEAD_DIM, HEAD_DIM), F32), pltpu.VMEM((R * H_MLSTM, 1, HEAD_DIM), F32),
                        pltpu.VMEM((R * H_MLSTM, 1, 128), F32)],
        compiler_params=pltpu.CompilerParams(dimension_semantics=("parallel", "parallel", "arbitrary")),
        name="mlstm_chunkwise",
    )(q, k, v, gcol, grow)


def mixer_mlstm(f_c, f_l, p, need_ctx):
    f_l = [to_colmajor(t) for t in f_l]
    qc, kc, vc, oc, gc = mlstm_prep(f_c, p)
    ql, kl, vl, ol, gl = mlstm_prep(f_l, p)
    Bn, Lc = qc.shape[:2]
    cat = lambda a, b: jnp.concatenate([a, b], axis=1)
    g = cat(gc, gl)
    L = g.shape[1]
    g = jnp.concatenate([g[:, :, :, 0], jax.nn.log_sigmoid(g[:, :, :, 1])], axis=-1)
    gcol = g.transpose(2, 0, 1, 3).reshape(2, Bn, L // MLSTM_CHUNK, MLSTM_CHUNK, 2 * H_MLSTM)
    h = mlstm_scan_bidir(cat(qc, ql), cat(kc, kl), cat(vc, vl), gcol, gcol.transpose(0, 1, 2, 4, 3),
                         Lc // MLSTM_CHUNK)
    hs = h[0] + h[1]
    y_l = from_colmajor(jax.nn.sigmoid(ol) * hs[:, Lc:])
    y_c = jax.nn.sigmoid(oc) * hs[:, :Lc] if need_ctx else None
    return y_c, y_l


def hyena_spectrum(L, p):
    pos = jnp.arange(L, dtype=F32)
    t = pos / (L - 1)
    bands = (HYENA_EMB - 1) // 2
    freqs = jnp.linspace(1e-4, bands - 1, bands, dtype=F32)
    ang = (2 * math.pi / L) * pos[:, None] * freqs[None, :]
    z = jnp.concatenate([t[:, None], jnp.cos(ang), -jnp.sin(ang)], axis=-1)
    h = jnp.sin(p['hy_freq'][0] * (z @ p['hy_w1'] + p['hy_b1']))
    h = jnp.sin(p['hy_freq'][1] * (h @ p['hy_w2'] + p['hy_b2']))
    h = (h @ p['hy_w3']).astype(F32).reshape(L, HYENA_ORDER, 2, W_HYENA)
    deltas = jnp.abs(jnp.linspace(math.log(HYENA_TARGET) / HYENA_SLOW, math.log(HYENA_TARGET) / HYENA_FAST,
                                  W_HYENA, dtype=F32))
    h = h * jnp.exp(-t[:, None, None, None] * deltas)
    fwd, bwd = h[:, :, 0], h[:, :, 1]
    two = jnp.concatenate([fwd, jnp.zeros_like(fwd[:1]), jnp.flip(bwd[1:], axis=0)], axis=0)
    two = two / (jnp.sum(jnp.abs(two), axis=0, keepdims=True) + EPS)
    return jnp.fft.rfft(two, axis=0)


def long_conv(u, spec, bias):
    L = u.shape[1]
    y = jnp.fft.irfft(jnp.fft.rfft(u, n=2 * L, axis=1) * spec, n=2 * L, axis=1)[:, :L]
    return y + u * bias


def mixer_hyena(f, p):
    u = dwconv(jnp.concatenate(f, axis=-1), p['hy_conv_w'], p['hy_conv_b'], HYENA_SHORT // 2).astype(F32)
    v, x1, x2 = jnp.split(u, 3, axis=-1)
    spec = hyena_spectrum(u.shape[1], p)
    z = x1 * long_conv(v, spec[:, 0], p['hy_bias'][0])
    return x2 * long_conv(z, spec[:, 1], p['hy_bias'][1])


PEER_SEL = PEER_HEADS * PEER_TOPK
PEER_HALF = D_MODEL // 2
PEER_CHUNK = 2048
PEER_FIRST_CHUNK = 512
PEER_TT = 16
SC_GATHER_ROWS = 64


def pack_bf16_halves(tab):
    bits = lax.bitcast_convert_type(tab.astype(jnp.bfloat16), jnp.uint16).astype(jnp.uint32)
    half = tab.shape[1] // 2
    return bits[:, :half] | (bits[:, half:] << 16)


def sc_gather_pair(u_tab, v_tab, idx):
    n = idx.shape[0]
    W = u_tab.shape[1]
    info = plsc.get_sparse_core_info()
    nc, ns = info.num_cores, info.num_subcores
    per_w = n // (nc * ns)
    G = SC_GATHER_ROWS
    assert per_w * nc * ns == n and per_w % G == 0
    nsteps = per_w // G
    mesh = plsc.VectorSubcoreMesh(core_axis_name="c", subcore_axis_name="s")
    out = jax.ShapeDtypeStruct((n, W), u_tab.dtype)

    def body(u_hbm, v_hbm, idx_hbm, uo_hbm, vo_hbm, idx_v, urows, vrows, usem, vsem):
        wid = lax.axis_index("s") * nc + lax.axis_index("c")
        base = wid * per_w
        pltpu.sync_copy(idx_hbm.at[pl.ds(base, per_w)], idx_v)

        @pl.loop(0, nsteps)
        def _(i):
            off = pl.multiple_of(i * G, G)
            ids = idx_v.at[pl.ds(off, G)]
            cu = pltpu.async_copy(u_hbm.at[ids], urows, usem)
            cv = pltpu.async_copy(v_hbm.at[ids], vrows, vsem)
            cu.wait()
            pltpu.sync_copy(urows, uo_hbm.at[pl.ds(base + off, G)])
            cv.wait()
            pltpu.sync_copy(vrows, vo_hbm.at[pl.ds(base + off, G)])

    fn = pl.kernel(body, out_type=(out, out), mesh=mesh,
                   scratch_types=[pltpu.VMEM((per_w,), jnp.int32),
                                  pltpu.VMEM((G, W), u_tab.dtype), pltpu.VMEM((G, W), u_tab.dtype),
                                  pltpu.SemaphoreType.DMA, pltpu.SemaphoreType.DMA],
                   name="peer_sc_gather")
    return fn(u_tab, v_tab, idx)


def _unpack_halves(w):
    lo = lax.bitcast_convert_type(w << 16, F32)
    hi = lax.bitcast_convert_type(w & jnp.uint32(0xFFFF0000), F32)
    return lo, hi


def _peer_apply_body(x_ref, g_ref, ug_ref, vg_ref, o_ref):
    TT = x_ref.shape[0]
    gpad = jnp.concatenate([g_ref[...], jnp.zeros((PEER_SEL - TT, PEER_SEL), F32)], axis=0)
    gT = gpad.T
    for t in range(TT):
        rows = pl.ds(t * PEER_SEL, PEER_SEL)
        ulo, uhi = _unpack_halves(ug_ref[rows, :])
        xl = x_ref[pl.ds(t, 1), 0:PEER_HALF]
        xh = x_ref[pl.ds(t, 1), PEER_HALF:D_MODEL]
        dots = jnp.sum(ulo * xl + uhi * xh, axis=-1, keepdims=True)
        w = jax.nn.gelu(dots) * gT[:, t:t + 1]
        vlo, vhi = _unpack_halves(vg_ref[rows, :])
        o_ref[pl.ds(t, 1), 0:PEER_HALF] = jnp.sum(vlo * w, axis=0, keepdims=True)
        o_ref[pl.ds(t, 1), PEER_HALF:D_MODEL] = jnp.sum(vhi * w, axis=0, keepdims=True)


def peer_apply(x, gate, ug, vg):
    T = x.shape[0]
    TT = PEER_TT
    return pl.pallas_call(
        _peer_apply_body,
        grid=(T // TT,),
        in_specs=[pl.BlockSpec((TT, D_MODEL), lambda i: (i, 0)),
                  pl.BlockSpec((TT, PEER_SEL), lambda i: (i, 0)),
                  pl.BlockSpec((TT * PEER_SEL, PEER_HALF), lambda i: (i, 0)),
                  pl.BlockSpec((TT * PEER_SEL, PEER_HALF), lambda i: (i, 0))],
        out_specs=pl.BlockSpec((TT, D_MODEL), lambda i: (i, 0)),
        out_shape=jax.ShapeDtypeStruct((T, D_MODEL), F32),
        compiler_params=pltpu.CompilerParams(dimension_semantics=("parallel",),
                                             vmem_limit_bytes=BIG_BLOCK_VMEM_LIMIT),
        name="peer_apply",
    )(x, gate, ug, vg)


ROUTE_TT = 1024
ROUTE_PAIR_ROWS = sum(-(-(PEER_TOPK // (i + 1)) // V7X_SUBLANES) * V7X_SUBLANES
                      for i in range(PEER_TOPK // 2)) + PEER_TOPK // 2


def _peer_route_body(s_ref, e_ref, g_ref, xs_ref, sv_ref, si_ref, cand_ref, cidx_ref, pf_ref, ts_ref):
    K, NK, TT = PEER_TOPK, PEER_NKEYS, s_ref.shape[-1]
    NEG = -jnp.inf
    xs_ref[...] = s_ref[0]
    kiota = lax.broadcasted_iota(jnp.int32, (NK, TT), 0).astype(F32)

    def half_topk(it, carry):
        for c in range(2):
            x = xs_ref[c]
            m = jnp.max(x, axis=0, keepdims=True)
            idx = jnp.min(jnp.where(x == m, kiota, float(NK)), axis=0, keepdims=True)
            xs_ref[c] = jnp.where(kiota == idx, NEG, x)
            sv_ref[c, pl.ds(it, 1), :] = m
            si_ref[c, pl.ds(it, 1), :] = idx
        return carry

    lax.fori_loop(0, K, half_topk, 0)

    jiota = lambda rows: lax.broadcasted_iota(jnp.int32, (rows, TT), 0).astype(F32)
    off = 0
    for i in range(K // 2):
        n = K // (i + 1)
        rows = -(-n // V7X_SUBLANES) * V7X_SUBLANES
        ok = jiota(rows) < float(n)
        cand_ref[pl.ds(off, rows), :] = jnp.where(ok, sv_ref[0, i:i + 1, :] + sv_ref[1, 0:rows, :], NEG)
        cidx_ref[pl.ds(off, rows), :] = si_ref[0, i:i + 1, :] * float(NK) + si_ref[1, 0:rows, :]
        pf_ref[pl.ds(off, rows), :] = jnp.where(ok, float(i * K) + jiota(rows), float(K * K))
        off += rows
    cand_ref[pl.ds(off, K // 2), :] = sv_ref[0, K // 2:K, :] + sv_ref[1, 0:1, :]
    cidx_ref[pl.ds(off, K // 2), :] = si_ref[0, K // 2:K, :] * float(NK) + si_ref[1, 0:1, :]
    pf_ref[pl.ds(off, K // 2), :] = (float(K // 2) + jiota(K // 2)) * float(K)
    piota = pf_ref[...]

    def pair_topk(it, carry):
        x = cand_ref[...]
        m = jnp.max(x, axis=0, keepdims=True)
        pos = jnp.min(jnp.where(x == m, piota, float(K * K)), axis=0, keepdims=True)
        sel = piota == pos
        cand_ref[...] = jnp.where(sel, NEG, x)
        ts_ref[pl.ds(it, 1), :] = m
        e_ref[0, pl.ds(it, 1), :] = jnp.max(jnp.where(sel, cidx_ref[...], -1.0), axis=0,
                                            keepdims=True).astype(jnp.int32)
        return carry

    lax.fori_loop(0, K, pair_topk, 0)
    ts = ts_ref[...]
    ex = jnp.exp(ts - ts[0:1, :])
    g_ref[0] = ex / jnp.sum(ex, axis=0, keepdims=True)


def peer_route_topk(sT):
    H, _, NK, N = sT.shape
    TT, K = ROUTE_TT, PEER_TOPK
    out_spec = pl.BlockSpec((1, K, TT), lambda h, i: (h, 0, i))
    return pl.pallas_call(
        _peer_route_body,
        grid=(H, N // TT),
        in_specs=[pl.BlockSpec((1, 2, NK, TT), lambda h, i: (h, 0, 0, i))],
        out_specs=[out_spec, out_spec],
        out_shape=[jax.ShapeDtypeStruct((H, K, N), jnp.int32), jax.ShapeDtypeStruct((H, K, N), F32)],
        scratch_shapes=[pltpu.VMEM((2, NK, TT), F32), pltpu.VMEM((2, K, TT), F32), pltpu.VMEM((2, K, TT), F32),
                        pltpu.VMEM((ROUTE_PAIR_ROWS, TT), F32), pltpu.VMEM((ROUTE_PAIR_ROWS, TT), F32),
                        pltpu.VMEM((ROUTE_PAIR_ROWS, TT), F32), pltpu.VMEM((K, TT), F32)],
        compiler_params=pltpu.CompilerParams(dimension_semantics=("parallel", "parallel")),
        name="peer_route_topk",
    )(sT)


def peer_route(xf, p):
    N = xf.shape[0]
    wq, keys = p['peer_wq'], p['peer_keys'].astype(F32)
    q = (xf @ wq).astype(F32).reshape(N, PEER_HEADS, 2, PEER_DQ // 2)
    sT = jnp.einsum('thcq,hckq->hckt', q, keys)
    eT, gT = peer_route_topk(sT)
    return (eT.transpose(2, 0, 1).reshape(N, PEER_SEL), gT.transpose(2, 0, 1).reshape(N, PEER_SEL))


def peer_ffn(h, p):
    Bn, L, D = h.shape
    N = Bn * L
    xf = h.reshape(N, D)
    eidx, gate = peer_route(xf, p)
    u_pk, v_pk = p['peer_u_pk'], p['peer_v_pk']
    outs, c0, size = [], 0, PEER_FIRST_CHUNK
    while c0 < N:
        rows = slice(c0, min(N, c0 + size))
        ug, vg = sc_gather_pair(u_pk, v_pk, eidx[rows].reshape(-1))
        outs.append(peer_apply(xf[rows], gate[rows], ug, vg))
        c0, size = rows.stop, min(2 * size, PEER_CHUNK)
    return jnp.concatenate(outs, axis=0).reshape(Bn, L, D)


def merge_groups(ys, g, dtype):
    outs, off = [], 0
    for y, w in zip(ys, GROUP_WIDTHS):
        outs.append(rmsnorm(y, g[off:off + w]).astype(dtype))
        off += w
    return jnp.concatenate(outs, axis=-1)


def token_mixers(h_c, h_l, p, need_ctx):
    zc = split_cols(h_c @ p['w_in'])
    zl = split_cols(h_l @ p['w_in'])
    a_c, a_l = mixer_rglru(zc[0:2], zl[0:2], p, need_ctx)
    b_c, b_l = mixer_rwkv(zc[2:8], zl[2:8], p, need_ctx)
    m_c, m_l = mixer_mlstm(zc[8:13], zl[8:13], p, need_ctx)
    d_l = mixer_hyena(zl[13:16], p)
    o_l = merge_groups([a_l, b_l, m_l, d_l], p['grp_g'], h_l.dtype) @ p['w_out']
    if not need_ctx:
        return None, o_l
    d_c = mixer_hyena(zc[13:16], p)
    o_c = merge_groups([a_c, b_c, m_c, d_c], p['grp_g'], h_c.dtype) @ p['w_out']
    return o_c, o_l


def trunk_layer(x_l, x_c, c, c_ctx, p, need_ctx):
    Bn = c.shape[0]
    mod_l = (jax.nn.silu(c) @ p['ada_w'] + p['ada_b']).reshape(Bn, 6, 1, D_MODEL)
    mod_c = (jax.nn.silu(c_ctx) @ p['ada_w'] + p['ada_b']).reshape(6, 1, 1, D_MODEL)
    h_l = modulate(rmsnorm(x_l, p['norm1_g']), mod_l[:, 0], mod_l[:, 1])
    h_c = modulate(rmsnorm(x_c, p['norm1_g']), mod_c[0], mod_c[1])
    o_c, o_l = token_mixers(h_c, h_l, p, need_ctx)
    x_l = x_l + mod_l[:, 2] * o_l
    x_l = x_l + mod_l[:, 5] * peer_ffn(modulate(rmsnorm(x_l, p['norm2_g']), mod_l[:, 3], mod_l[:, 4]), p)
    if need_ctx:
        x_c = x_c + mod_c[2] * o_c
        x_c = x_c + mod_c[5] * peer_ffn(modulate(rmsnorm(x_c, p['norm2_g']), mod_c[3], mod_c[4]), p)
    return x_l, x_c


def _final_norm_body(x_ref, g_ref, o_ref):
    xf = x_ref[...]
    y = xf * lax.rsqrt(jnp.mean(xf * xf, axis=-1, keepdims=True) + EPS)
    o_ref[...] = y * g_ref[...]


def final_rmsnorm(x, g):
    Bn, L, D = x.shape
    rows = Bn * L
    tile = 1024
    out = pl.pallas_call(
        _final_norm_body,
        grid=(rows // tile,),
        in_specs=[pl.BlockSpec((tile, D), lambda i: (i, 0)), pl.BlockSpec((1, D), lambda i: (0, 0))],
        out_specs=pl.BlockSpec((tile, D), lambda i: (i, 0)),
        out_shape=jax.ShapeDtypeStruct((rows, D), x.dtype),
        name="final_rmsnorm",
    )(x.reshape(rows, D), g.reshape(1, D))
    return out.reshape(Bn, L, D)


def kernel(x, c, ctx, c_ctx, ada_w, ada_b, norm1_g, norm2_g, w_in, w_out, grp_g,
           lru_conv_w, lru_conv_b, lru_wr, lru_br, lru_wi, lru_bi, lru_lam,
           rwkv_mu, rwkv_w0, rwkv_w2, rwkv_a0, rwkv_a2, rwkv_g2, rwkv_kk, rwkv_ka, rwkv_rk, rwkv_ln_g, rwkv_ln_b,
           mlstm_conv_w, mlstm_conv_b, mlstm_gate_b,
           hy_conv_w, hy_conv_b, hy_w1, hy_b1, hy_w2, hy_b2, hy_w3, hy_freq, hy_bias,
           peer_wq, peer_keys, peer_u, peer_v, final_g):
    x_l, x_c = x, ctx
    for i in range(DEPTH):
        p = dict(ada_w=ada_w[i], ada_b=ada_b[i], norm1_g=norm1_g[i], norm2_g=norm2_g[i], w_in=w_in[i],
                 w_out=w_out[i], grp_g=grp_g[i],
                 lru_conv_w=lru_conv_w[i], lru_conv_b=lru_conv_b[i], lru_wr=lru_wr[i], lru_br=lru_br[i],
                 lru_wi=lru_wi[i], lru_bi=lru_bi[i], lru_lam=lru_lam[i],
                 rwkv_mu=rwkv_mu[i], rwkv_w0=rwkv_w0[i], rwkv_w2=rwkv_w2[i], rwkv_a0=rwkv_a0[i],
                 rwkv_a2=rwkv_a2[i], rwkv_g2=rwkv_g2[i], rwkv_kk=rwkv_kk[i], rwkv_ka=rwkv_ka[i],
                 rwkv_rk=rwkv_rk[i], rwkv_ln_g=rwkv_ln_g[i], rwkv_ln_b=rwkv_ln_b[i],
                 mlstm_conv_w=mlstm_conv_w[i], mlstm_conv_b=mlstm_conv_b[i], mlstm_gate_b=mlstm_gate_b[i],
                 hy_conv_w=hy_conv_w[i], hy_conv_b=hy_conv_b[i], hy_w1=hy_w1[i], hy_b1=hy_b1[i],
                 hy_w2=hy_w2[i], hy_b2=hy_b2[i], hy_w3=hy_w3[i], hy_freq=hy_freq[i], hy_bias=hy_bias[i],
                 peer_wq=peer_wq[i], peer_keys=peer_keys[i],
                 peer_u_pk=pack_bf16_halves(peer_u[i]), peer_v_pk=pack_bf16_halves(peer_v[i]))
        x_l, x_c = trunk_layer(x_l, x_c, c, c_ctx, p, i < DEPTH - 1)
    return final_rmsnorm(x_l, final_g)
```

```python
import functools
import math
import jax, jax.numpy as jnp
from jax import lax
import numpy as np
from jax.experimental import pallas as pl
from jax.experimental.pallas import tpu as pltpu
from jax.experimental.pallas import tpu_sc as plsc

D_MODEL = 1024
DEPTH = 2

GRID_W = 64
F32 = jnp.float32
V7X_VMEM_BYTES = 64 * 1024 * 1024
V7X_SUBLANES = 8
BIG_BLOCK_VMEM_LIMIT = V7X_VMEM_BYTES * 5 // 8
EPS = 1e-6
HEAD_DIM = 64
D_MIX = D_MODEL
W_LRU = D_MIX // 4
W_RWKV = D_MIX // 4
W_MLSTM = D_MIX // 4
W_HYENA = D_MIX - W_LRU - W_RWKV - W_MLSTM
GROUP_WIDTHS = (W_LRU, W_RWKV, W_MLSTM, W_HYENA)
H_RWKV = W_RWKV // HEAD_DIM
H_MLSTM = W_MLSTM // HEAD_DIM
CONV_W = 4
LRU_C = 8.0
RWKV_LORA_W = 32
RWKV_LORA_A = 32
RWKV_LORA_G = 64
RWKV_GN_EPS = 64e-5
MLSTM_CHUNK = 64
HYENA_ORDER = 2
HYENA_SHORT = 3
HYENA_EMB = 33
HYENA_TARGET = 1e-2
HYENA_FAST = 0.3
HYENA_SLOW = 1.5
PEER_HEADS = 8
PEER_NKEYS = 128
PEER_TOPK = 16
PEER_DQ = 256
IN_SPLITS = (W_LRU, W_LRU, W_RWKV, W_RWKV, W_RWKV, RWKV_LORA_W, RWKV_LORA_A, RWKV_LORA_G, W_MLSTM, W_MLSTM, W_MLSTM, W_MLSTM, 4 * H_MLSTM, W_HYENA, W_HYENA, W_HYENA)


def rmsnorm(x, g):
    xf = x.astype(F32)
    y = xf * lax.rsqrt(jnp.mean(xf * xf, axis=-1, keepdims=True) + EPS)
    return (y * g.astype(F32)).astype(x.dtype)


def modulate(h, shift, scale):
    return h * (1 + scale) + shift


def split_cols(z):
    offs = np.cumsum(IN_SPLITS)[:-1].tolist()
    return jnp.split(z, offs, axis=-1)


def dwconv(x, w, b, pad_left):
    K = w.shape[0]
    y = lax.conv_general_dilated(x, w[:, None, :].astype(x.dtype), (1,), [(pad_left, K - 1 - pad_left)],
                                 dimension_numbers=('NWC', 'WIO', 'NWC'), feature_group_count=x.shape[-1])
    return y + b.astype(x.dtype)


def token_shift(x, mu):
    prev = jnp.pad(x, ((0, 0), (1, 0), (0, 0)))[:, :-1]
    nxt = jnp.pad(x, ((0, 0), (0, 1), (0, 0)))[:, 1:]
    return x + mu[0] * (prev - x) + mu[1] * (nxt - x)


def split_heads(t, h):
    Bn, L, W = t.shape
    return t.reshape(Bn, L, h, W // h)


def to_colmajor(a):
    Bn, L, C = a.shape
    rows = L // GRID_W
    return a.reshape(Bn, rows, GRID_W, C).transpose(0, 2, 1, 3).reshape(Bn, L, C)


def from_colmajor(a):
    Bn, L, C = a.shape
    rows = L // GRID_W
    return a.reshape(Bn, GRID_W, rows, C).transpose(0, 2, 1, 3).reshape(Bn, L, C)


LRU_TC = 256


def _expm1(z):
    u = jnp.exp(z)
    return jnp.where(u == 1.0, z, (u - 1.0) * z / jnp.where(u == 1.0, 1.0, jnp.log(u)))


def _lru_body(x_ref, h0_ref, wr_ref, wi_ref, br_ref, bi_ref, sp_ref, h_ref, hT_ref, st_ref, *, reverse):
    TC, W = x_ref.shape[1], x_ref.shape[2]

    @pl.when(pl.program_id(1) == 0)
    def _():
        st_ref[...] = h0_ref[0]

    x = x_ref[0]
    r = jax.nn.sigmoid(jnp.dot(x, wr_ref[...], preferred_element_type=F32) + br_ref[...])
    i = jax.nn.sigmoid(jnp.dot(x, wi_ref[...], preferred_element_type=F32) + bi_ref[...])
    log_a = -LRU_C * r * sp_ref[...]
    a = jnp.exp(log_a)
    b = jnp.sqrt(-_expm1(2.0 * log_a)) * (i * x)
    row = lax.broadcasted_iota(jnp.int32, (TC, W), 0)
    s = 1
    while s < TC:
        if reverse:
            keep = row < TC - s
            a_sh = jnp.where(keep, pltpu.roll(a, TC - s, 0), 1.0)
            b_sh = jnp.where(keep, pltpu.roll(b, TC - s, 0), 0.0)
        else:
            keep = row >= s
            a_sh = jnp.where(keep, pltpu.roll(a, s, 0), 1.0)
            b_sh = jnp.where(keep, pltpu.roll(b, s, 0), 0.0)
        b = b + a * b_sh
        a = a * a_sh
        s *= 2
    h = b + a * st_ref[...]
    h_ref[0] = h
    last = h[0:1, :] if reverse else h[TC - 1:TC, :]
    st_ref[...] = last
    hT_ref[0] = last


def lru_scan_dir(xc, h0, wr_bd, wi_bd, br, bi, sp, reverse):
    Bn, L, W = xc.shape
    TC = min(LRU_TC, L)
    n = L // TC
    tmap = (lambda b, c: (b, n - 1 - c, 0)) if reverse else (lambda b, c: (b, c, 0))
    wspec = pl.BlockSpec((W, W), lambda b, c: (0, 0))
    vspec = pl.BlockSpec((1, W), lambda b, c: (0, 0))
    sspec = pl.BlockSpec((1, 1, W), lambda b, c: (b, 0, 0))
    return pl.pallas_call(
        functools.partial(_lru_body, reverse=reverse),
        grid=(Bn, n),
        in_specs=[pl.BlockSpec((1, TC, W), tmap), sspec, wspec, wspec, vspec, vspec, vspec],
        out_specs=[pl.BlockSpec((1, TC, W), tmap), sspec],
        out_shape=[jax.ShapeDtypeStruct((Bn, L, W), F32), jax.ShapeDtypeStruct((Bn, 1, W), F32)],
        scratch_shapes=[pltpu.VMEM((1, W), F32)],
        compiler_params=pltpu.CompilerParams(dimension_semantics=("parallel", "arbitrary")),
        name="rglru_scan",
    )(xc, h0, wr_bd, wi_bd, br, bi, sp)


def head_block_diag(w):
    D2, H, N, _ = w.shape
    return jnp.einsum('dhij,hg->dhigj', w, jnp.eye(H, dtype=w.dtype)).reshape(D2, H * N, H * N)


def mixer_rglru(f_c, f_l, p, need_ctx):
    (x_c, g_c), (x_l, g_l) = f_c, f_l
    xc_c = dwconv(x_c, p['lru_conv_w'], p['lru_conv_b'], CONV_W // 2).astype(F32)
    xc_l = dwconv(x_l, p['lru_conv_w'], p['lru_conv_b'], CONV_W // 2).astype(F32)
    h0 = jnp.zeros((xc_l.shape[0], 1, W_LRU), F32)
    wr, wi = head_block_diag(p['lru_wr']), head_block_diag(p['lru_wi'])
    sp = jax.nn.softplus(-p['lru_lam'])
    hs_c, hs_l = [], []
    for d in range(2):
        gates = (wr[d], wi[d], p['lru_br'][d][None], p['lru_bi'][d][None], sp[d][None])
        hc, st = lru_scan_dir(xc_c, h0, *gates, reverse=(d == 1))
        hl, _ = lru_scan_dir(xc_l, st, *gates, reverse=(d == 1))
        hs_c.append(hc)
        hs_l.append(hl)
    y_l = jax.nn.gelu(g_l.astype(F32)) * (hs_l[0] + hs_l[1])
    y_c = jax.nn.gelu(g_c.astype(F32)) * (hs_c[0] + hs_c[1]) if need_ctx else None
    return y_c, y_l


def rwkv_shift(f, p):
    r, k, v, zw, za, zg = [t.astype(F32) for t in f]
    mu = p['rwkv_mu']
    return (token_shift(r, mu[0]), token_shift(k, mu[1]), token_shift(v, mu[2]), zw, za, zg)


def rwkv_dir_inputs(f, p, d):
    r, k, v, zw, za, _ = f
    w_log = -jax.nn.softplus(-(p['rwkv_w0'][d] + jnp.tanh(zw) @ p['rwkv_w2'][d])) - 0.5
    decay = jnp.exp(-jnp.exp(w_log))
    a = jax.nn.sigmoid(p['rwkv_a0'][d] + za @ p['rwkv_a2'][d])
    kk = split_heads(k * p['rwkv_kk'], H_RWKV)
    kk = kk / jnp.maximum(jnp.linalg.norm(kk, axis=-1, keepdims=True), 1e-12)
    kd = k * (1 + (a - 1) * p['rwkv_ka'])
    return [split_heads(r, H_RWKV), split_heads(decay, H_RWKV), split_heads(kd, H_RWKV),
            split_heads(v, H_RWKV), kk, kk * split_heads(a, H_RWKV)]


RWKV_TC = HEAD_DIM
RWKV_PAIRS = 32


def _rwkv_body(kk_ref, v_ref, r_ref, w_ref, kka_ref, k_ref, y_ref, s_ref):
    NP, N = RWKV_PAIRS, HEAD_DIM
    rev = pl.program_id(0)

    @pl.when(pl.program_id(1) == 0)
    def _():
        s_ref[...] = jnp.zeros_like(s_ref)

    y_ref[...] = jnp.zeros_like(y_ref)
    lane = lax.broadcasted_iota(jnp.int32, (N, 2 * N), 1)
    row = lax.broadcasted_iota(jnp.int32, (N, 2 * N), 0)
    diag = (lane % N) == row
    seg4 = lax.broadcasted_iota(jnp.int32, (4 * N, 4 * N), 0) // N == \
        lax.broadcasted_iota(jnp.int32, (4 * N, 4 * N), 1) // N
    ones4 = jnp.where(seg4, 1.0, 0.0).astype(jnp.bfloat16)

    def segsum_bcast(x):
        H = x.shape[0] // 2
        xb = x.astype(jnp.bfloat16)
        lhs = jnp.concatenate([xb[:H].reshape(H * N, 2 * N), xb[H:].reshape(H * N, 2 * N)], axis=-1)
        out = jnp.dot(lhs, ones4, preferred_element_type=F32)
        return jnp.concatenate([out[:, :2 * N].reshape(H, N, 2 * N), out[:, 2 * N:].reshape(H, N, 2 * N)], axis=0)

    def rowv(ref, t):
        x = ref[:, pl.ds(t, 1), :] if len(ref.shape) == 3 else ref[0, :, pl.ds(t, 1), :]
        return jnp.concatenate([x[:, :, :2 * N], x[:, :, 2 * N:]], axis=0)

    def step(i, carry):
        t = i + rev * (RWKV_TC - 1 - 2 * i)
        tp = jnp.clip(t - 1 + 2 * rev, 0, RWKV_TC - 1)
        S = s_ref[...]
        R = segsum_bcast(jnp.concatenate([S * rowv(kk_ref, t), jnp.where(diag, rowv(v_ref, t), 0.0),
                                          S * rowv(r_ref, tp)], axis=0))
        y_ref[0, :, 0] = jnp.where(((lane % N) == tp) & (i > 0), R[2 * NP:], y_ref[0, :, 0])
        s_ref[...] = S * rowv(w_ref, t) - R[:NP] * rowv(kka_ref, t) + R[NP:2 * NP] * rowv(k_ref, t)
        return carry

    lax.fori_loop(0, RWKV_TC, step, 0)
    t_last = (RWKV_TC - 1) * (1 - rev)
    Y = segsum_bcast(s_ref[...] * rowv(r_ref, t_last))
    y_ref[0, :, 0] = jnp.where((lane % N) == t_last, Y, y_ref[0, :, 0])


def rwkv_scan_bidir(kk, v, r, w, kka, k, n_ctx_blocks):
    Bn, L, C = kk.shape
    TC, N = RWKV_TC, HEAD_DIM
    n = L // TC
    assert RWKV_PAIRS == Bn * C // (2 * N)

    def blk(d, c):
        back = jnp.where(c < n_ctx_blocks, n_ctx_blocks - 1 - c, n + n_ctx_blocks - 1 - c)
        return jnp.where(d == 0, c, back)

    shared = pl.BlockSpec((Bn, TC, C), lambda d, c: (0, blk(d, c), 0))
    per_dir = pl.BlockSpec((1, Bn, TC, C), lambda d, c: (d, 0, blk(d, c), 0))
    return pl.pallas_call(
        _rwkv_body,
        grid=(2, n),
        in_specs=[shared, shared, shared, per_dir, per_dir, per_dir],
        out_specs=pl.BlockSpec((1, RWKV_PAIRS, 1, N, 2 * N), lambda d, c: (d, 0, blk(d, c), 0, 0)),
        out_shape=jax.ShapeDtypeStruct((2, RWKV_PAIRS, n, N, 2 * N), F32),
        scratch_shapes=[pltpu.VMEM((RWKV_PAIRS, N, 2 * N), F32)],
        compiler_params=pltpu.CompilerParams(dimension_semantics=("parallel", "arbitrary"),
                                             vmem_limit_bytes=BIG_BLOCK_VMEM_LIMIT),
        name="rwkv7_scan",
    )(kk, v, r, w, kka, k)


def rwkv7_scan_bidir(ins_c, ins_l):
    Bn, Lc = ins_c[0][0].shape[:2]
    Ll = ins_l[0][0].shape[1]
    L = Lc + Ll
    cat = lambda d, j: jnp.concatenate([ins_c[d][j], ins_l[d][j]], axis=1).reshape(Bn, L, W_RWKV)
    per_dir = lambda j: jnp.stack([cat(0, j), cat(1, j)])
    yT = rwkv_scan_bidir(cat(0, 4), cat(0, 3), cat(0, 0), per_dir(1), per_dir(5), per_dir(2), Lc // RWKV_TC)
    halves = W_RWKV // (2 * HEAD_DIM)
    y = yT.reshape(2, halves, Bn, L // HEAD_DIM, HEAD_DIM, 2, HEAD_DIM)
    y = y.transpose(0, 2, 3, 6, 1, 5, 4).reshape(2, Bn, L, H_RWKV, HEAD_DIM)
    return ([y[d, :, :Lc] for d in range(2)], [y[d, :, Lc:] for d in range(2)])


def rwkv_bonus(ins, p):
    r, _, kd, v = ins[:4]
    return jnp.sum(r * kd * p['rwkv_rk'], axis=-1, keepdims=True) * v


def rwkv_out(y, bonus, zg, p):
    Bn, L, H, N = y.shape
    mu = jnp.mean(y, axis=-1, keepdims=True)
    var = jnp.mean(jnp.square(y - mu), axis=-1, keepdims=True)
    yn = ((y - mu) * lax.rsqrt(var + RWKV_GN_EPS)).reshape(Bn, L, H * N) * p['rwkv_ln_g'] + p['rwkv_ln_b']
    g = jax.nn.sigmoid(zg) @ p['rwkv_g2']
    return (yn + bonus.reshape(Bn, L, H * N)) * g


def mixer_rwkv(f_c, f_l, p, need_ctx):
    fc, fl = rwkv_shift(f_c, p), rwkv_shift(f_l, p)
    ins_c = [rwkv_dir_inputs(fc, p, d) for d in range(2)]
    ins_l = [rwkv_dir_inputs(fl, p, d) for d in range(2)]
    o_c, o_l = rwkv7_scan_bidir(ins_c, ins_l)
    y_c, y_l = [], []
    for d in range(2):
        y_l.append((o_l[d], rwkv_bonus(ins_l[d], p)))
        if need_ctx:
            y_c.append((o_c[d], rwkv_bonus(ins_c[d], p)))
    out_l = rwkv_out(y_l[0][0] + y_l[1][0], y_l[0][1] + y_l[1][1], fl[5], p)
    out_c = rwkv_out(y_c[0][0] + y_c[1][0], y_c[0][1] + y_c[1][1], fc[5], p) if need_ctx else None
    return out_c, out_l


def mlstm_prep(f, p):
    q, k, v, o, gz = f
    Bn, L, _ = q.shape
    qk = jax.nn.silu(dwconv(jnp.concatenate([q, k], axis=-1), p['mlstm_conv_w'], p['mlstm_conv_b'], CONV_W // 2)).astype(F32)
    q, k = jnp.split(qk, 2, axis=-1)
    gates = gz.astype(F32).reshape(Bn, L, 2, 2, H_MLSTM) + p['mlstm_gate_b']
    return q * HEAD_DIM ** -0.5, k, v.astype(F32), o.astype(F32), gates


MLSTM_ROWS = 2


def _mlstm_body(q_ref, k_ref, v_ref, gc_ref, gr_ref, h_ref, c_ref, n_ref, m_ref):
    T, N, H = MLSTM_CHUNK, HEAD_DIM, H_MLSTM
    rev = pl.program_id(0)

    @pl.when(pl.program_id(2) == 0)
    def _():
        c_ref[...] = jnp.zeros_like(c_ref)
        n_ref[...] = jnp.zeros_like(n_ref)
        m_ref[...] = jnp.zeros_like(m_ref)

    ti = lax.broadcasted_iota(jnp.int32, (T, T), 0)
    si = lax.broadcasted_iota(jnp.int32, (T, T), 1)
    mask = (si - ti) * (1 - 2 * rev) <= 0
    tri = jnp.where(mask, 1.0, 0.0)
    hp = lax.Precision.HIGHEST
    gcs = [gc_ref[0, r, 0] for r in range(MLSTM_ROWS)]
    grs = [gr_ref[0, r, 0] for r in range(MLSTM_ROWS)]
    b_cols = [jnp.dot(tri, g[:, H:], precision=hp, preferred_element_type=F32) for g in gcs]
    b_rows = [lax.dot_general(g[H:], tri, (((1,), (1,)), ((), ())), precision=hp, preferred_element_type=F32)
              for g in grs]
    b_tots = [jnp.sum(g[:, H:], axis=0, keepdims=True) for g in gcs]
    units = [(r, h) for r in range(MLSTM_ROWS) for h in range(H)]
    hs = range(len(units))
    sl = [slice(h * N, (h + 1) * N) for _, h in units]
    qh = [q_ref[units[h][0], :, sl[h]] for h in hs]
    kh = [k_ref[units[h][0], :, sl[h]] for h in hs]
    vh = [v_ref[units[h][0], :, sl[h]] for h in hs]
    C = [c_ref[h] for h in hs]
    n = [n_ref[h] for h in hs]
    m = [m_ref[h][:, 0:1] for h in hs]
    bc = [b_cols[r][:, h:h + 1] for r, h in units]
    br = [b_rows[r][h:h + 1, :] for r, h in units]
    ic = [gcs[r][:, h:h + 1] for r, h in units]
    ir = [grs[r][h:h + 1, :] for r, h in units]
    bT = [b_tots[r][:, h:h + 1] for r, h in units]
    nt = (((1,), (1,)), ((), ()))
    qk = [lax.dot_general(qh[h], kh[h], nt, preferred_element_type=F32) for h in hs]
    qc = [lax.dot_general(qh[h], C[h], nt, preferred_element_type=F32) for h in hs]
    logd = [jnp.where(mask, bc[h] - br[h] + ir[h], -jnp.inf) for h in hs]
    inter = [bc[h] + m[h] for h in hs]
    mt = [jnp.maximum(inter[h], jnp.max(logd[h], axis=-1, keepdims=True)) for h in hs]
    s = [qk[h] * jnp.exp(logd[h] - mt[h]) for h in hs]
    e_inter = [jnp.exp(inter[h] - mt[h]) for h in hs]
    num = [jnp.dot(s[h], vh[h], preferred_element_type=F32) + e_inter[h] * qc[h] for h in hs]
    den = [jnp.sum(s[h], axis=-1, keepdims=True) + e_inter[h] * jnp.sum(qh[h] * n[h], axis=-1, keepdims=True)
           for h in hs]
    out = [num[h] / jnp.maximum(jnp.abs(den[h]), jnp.exp(-mt[h])) for h in hs]
    for r in range(MLSTM_ROWS):
        h_ref[0, r] = jnp.concatenate(out[r * H:(r + 1) * H], axis=-1)
    m_new = [jnp.maximum(bT[h] + m[h], jnp.max(bT[h] - br[h] + ir[h], axis=-1, keepdims=True)) for h in hs]
    w_col = [jnp.exp(bT[h] - bc[h] + ic[h] - m_new[h]) for h in hs]
    dec = [jnp.exp(bT[h] + m[h] - m_new[h]) for h in hs]
    vk = [lax.dot_general(vh[h] * w_col[h], kh[h], (((0,), (0,)), ((), ())), preferred_element_type=F32) for h in hs]
    for h in hs:
        c_ref[h] = dec[h] * C[h] + vk[h]
        n_ref[h] = dec[h] * n[h] + jnp.sum(kh[h] * w_col[h], axis=0, keepdims=True)
        m_ref[h] = jnp.broadcast_to(m_new[h], (1, 128))


def mlstm_scan_bidir(q, k, v, gcol, grow, n_ctx_blocks):
    Bn, L, W = q.shape
    T = MLSTM_CHUNK
    n = L // T

    def blk(d, c):
        back = jnp.where(c < n_ctx_blocks, n_ctx_blocks - 1 - c, n + n_ctx_blocks - 1 - c)
        return jnp.where(d == 0, c, back)

    R = MLSTM_ROWS
    xspec = pl.BlockSpec((R, T, W), lambda d, b, c: (b, blk(d, c), 0))
    gcs = pl.BlockSpec((1, R, 1, T, 2 * H_MLSTM), lambda d, b, c: (d, b, blk(d, c), 0, 0))
    grs = pl.BlockSpec((1, R, 1, 2 * H_MLSTM, T), lambda d, b, c: (d, b, blk(d, c), 0, 0))
    return pl.pallas_call(
        _mlstm_body,
        grid=(2, Bn // R, n),
        in_specs=[xspec, xspec, xspec, gcs, grs],
        out_specs=pl.BlockSpec((1, R, T, W), lambda d, b, c: (d, b, blk(d, c), 0)),
        out_shape=jax.ShapeDtypeStruct((2, Bn, L, W), F32),
        scratch_shapes=[pltpu.VMEM((R * H_MLSTM, HEAD_DIM, HEAD_DIM), F32), pltpu.VMEM((R * H_MLSTM, 1, HEAD_DIM), F32),
                        pltpu.VMEM((R * H_MLSTM, 1, 128), F32)],
        compiler_params=pltpu.CompilerParams(dimension_semantics=("parallel", "parallel", "arbitrary")),
        name="mlstm_chunkwise",
    )(q, k, v, gcol, grow)


def mixer_mlstm(f_c, f_l, p, need_ctx):
    f_l = [to_colmajor(t) for t in f_l]
    qc, kc, vc, oc, gc = mlstm_prep(f_c, p)
    ql, kl, vl, ol, gl = mlstm_prep(f_l, p)
    Bn, Lc = qc.shape[:2]
    cat = lambda a, b: jnp.concatenate([a, b], axis=1)
    g = cat(gc, gl)
    L = g.shape[1]
    g = jnp.concatenate([g[:, :, :, 0], jax.nn.log_sigmoid(g[:, :, :, 1])], axis=-1)
    gcol = g.transpose(2, 0, 1, 3).reshape(2, Bn, L // MLSTM_CHUNK, MLSTM_CHUNK, 2 * H_MLSTM)
    h = mlstm_scan_bidir(cat(qc, ql), cat(kc, kl), cat(vc, vl), gcol, gcol.transpose(0, 1, 2, 4, 3),
                         Lc // MLSTM_CHUNK)
    hs = h[0] + h[1]
    y_l = from_colmajor(jax.nn.sigmoid(ol) * hs[:, Lc:])
    y_c = jax.nn.sigmoid(oc) * hs[:, :Lc] if need_ctx else None
    return y_c, y_l


def hyena_spectrum(L, p):
    pos = jnp.arange(L, dtype=F32)
    t = pos / (L - 1)
    bands = (HYENA_EMB - 1) // 2
    freqs = jnp.linspace(1e-4, bands - 1, bands, dtype=F32)
    ang = (2 * math.pi / L) * pos[:, None] * freqs[None, :]
    z = jnp.concatenate([t[:, None], jnp.cos(ang), -jnp.sin(ang)], axis=-1)
    h = jnp.sin(p['hy_freq'][0] * (z @ p['hy_w1'] + p['hy_b1']))
    h = jnp.sin(p['hy_freq'][1] * (h @ p['hy_w2'] + p['hy_b2']))
    h = (h @ p['hy_w3']).astype(F32).reshape(L, HYENA_ORDER, 2, W_HYENA)
    deltas = jnp.abs(jnp.linspace(math.log(HYENA_TARGET) / HYENA_SLOW, math.log(HYENA_TARGET) / HYENA_FAST,
                                  W_HYENA, dtype=F32))
    h = h * jnp.exp(-t[:, None, None, None] * deltas)
    fwd, bwd = h[:, :, 0], h[:, :, 1]
    two = jnp.concatenate([fwd, jnp.zeros_like(fwd[:1]), jnp.flip(bwd[1:], axis=0)], axis=0)
    two = two / (jnp.sum(jnp.abs(two), axis=0, keepdims=True) + EPS)
    return jnp.fft.rfft(two, axis=0)


def long_conv(u, spec, bias):
    L = u.shape[1]
    y = jnp.fft.irfft(jnp.fft.rfft(u, n=2 * L, axis=1) * spec, n=2 * L, axis=1)[:, :L]
    return y + u * bias


def mixer_hyena(f, p):
    u = dwconv(jnp.concatenate(f, axis=-1), p['hy_conv_w'], p['hy_conv_b'], HYENA_SHORT // 2).astype(F32)
    v, x1, x2 = jnp.split(u, 3, axis=-1)
    spec = hyena_spectrum(u.shape[1], p)
    z = x1 * long_conv(v, spec[:, 0], p['hy_bias'][0])
    return x2 * long_conv(z, spec[:, 1], p['hy_bias'][1])


PEER_SEL = PEER_HEADS * PEER_TOPK
PEER_HALF = D_MODEL // 2
PEER_CHUNK = 4096
PEER_FIRST_CHUNK = 512
PEER_TT = 16
SC_GATHER_ROWS = 64


def pack_bf16_halves(tab):
    bits = lax.bitcast_convert_type(tab.astype(jnp.bfloat16), jnp.uint16).astype(jnp.uint32)
    half = tab.shape[1] // 2
    return bits[:, :half] | (bits[:, half:] << 16)


def sc_gather_pair(u_tab, v_tab, idx):
    n = idx.shape[0]
    W = u_tab.shape[1]
    info = plsc.get_sparse_core_info()
    nc, ns = info.num_cores, info.num_subcores
    per_w = n // (nc * ns)
    G = SC_GATHER_ROWS
    assert per_w * nc * ns == n and per_w % G == 0
    nsteps = per_w // G
    mesh = plsc.VectorSubcoreMesh(core_axis_name="c", subcore_axis_name="s")
    out = jax.ShapeDtypeStruct((n, W), u_tab.dtype)

    def body(u_hbm, v_hbm, idx_hbm, uo_hbm, vo_hbm, idx_v, urows, vrows, usem, vsem):
        wid = lax.axis_index("s") * nc + lax.axis_index("c")
        base = wid * per_w
        pltpu.sync_copy(idx_hbm.at[pl.ds(base, per_w)], idx_v)

        @pl.loop(0, nsteps)
        def _(i):
            off = pl.multiple_of(i * G, G)
            ids = idx_v.at[pl.ds(off, G)]
            cu = pltpu.async_copy(u_hbm.at[ids], urows, usem)
            cv = pltpu.async_copy(v_hbm.at[ids], vrows, vsem)
            cu.wait()
            pltpu.sync_copy(urows, uo_hbm.at[pl.ds(base + off, G)])
            cv.wait()
            pltpu.sync_copy(vrows, vo_hbm.at[pl.ds(base + off, G)])

    fn = pl.kernel(body, out_type=(out, out), mesh=mesh,
                   scratch_types=[pltpu.VMEM((per_w,), jnp.int32),
                                  pltpu.VMEM((G, W), u_tab.dtype), pltpu.VMEM((G, W), u_tab.dtype),
                                  pltpu.SemaphoreType.DMA, pltpu.SemaphoreType.DMA],
                   name="peer_sc_gather")
    return fn(u_tab, v_tab, idx)


def _unpack_halves(w):
    lo = lax.bitcast_convert_type(w << 16, F32)
    hi = lax.bitcast_convert_type(w & jnp.uint32(0xFFFF0000), F32)
    return lo, hi


def _peer_apply_body(x_ref, g_ref, ug_ref, vg_ref, o_ref):
    TT = x_ref.shape[0]
    gpad = jnp.concatenate([g_ref[...], jnp.zeros((PEER_SEL - TT, PEER_SEL), F32)], axis=0)
    gT = gpad.T
    for t in range(TT):
        rows = pl.ds(t * PEER_SEL, PEER_SEL)
        ulo, uhi = _unpack_halves(ug_ref[rows, :])
        xl = x_ref[pl.ds(t, 1), 0:PEER_HALF]
        xh = x_ref[pl.ds(t, 1), PEER_HALF:D_MODEL]
        dots = jnp.sum(ulo * xl + uhi * xh, axis=-1, keepdims=True)
        w = jax.nn.gelu(dots) * gT[:, t:t + 1]
        vlo, vhi = _unpack_halves(vg_ref[rows, :])
        o_ref[pl.ds(t, 1), 0:PEER_HALF] = jnp.sum(vlo * w, axis=0, keepdims=True)
        o_ref[pl.ds(t, 1), PEER_HALF:D_MODEL] = jnp.sum(vhi * w, axis=0, keepdims=True)


def peer_apply(x, gate, ug, vg):
    T = x.shape[0]
    TT = PEER_TT
    return pl.pallas_call(
        _peer_apply_body,
        grid=(T // TT,),
        in_specs=[pl.BlockSpec((TT, D_MODEL), lambda i: (i, 0)),
                  pl.BlockSpec((TT, PEER_SEL), lambda i: (i, 0)),
                  pl.BlockSpec((TT * PEER_SEL, PEER_HALF), lambda i: (i, 0)),
                  pl.BlockSpec((TT * PEER_SEL, PEER_HALF), lambda i: (i, 0))],
        out_specs=pl.BlockSpec((TT, D_MODEL), lambda i: (i, 0)),
        out_shape=jax.ShapeDtypeStruct((T, D_MODEL), F32),
        compiler_params=pltpu.CompilerParams(dimension_semantics=("parallel",),
                                             vmem_limit_bytes=BIG_BLOCK_VMEM_LIMIT),
        name="peer_apply",
    )(x, gate, ug, vg)


ROUTE_TT = 1024
ROUTE_PAIR_ROWS = sum(-(-(PEER_TOPK // (i + 1)) // V7X_SUBLANES) * V7X_SUBLANES
                      for i in range(PEER_TOPK // 2)) + PEER_TOPK // 2


def _peer_route_body(s_ref, e_ref, g_ref, xs_ref, sv_ref, si_ref, cand_ref, cidx_ref, pf_ref, ts_ref):
    K, NK, TT = PEER_TOPK, PEER_NKEYS, s_ref.shape[-1]
    NEG = -jnp.inf
    xs_ref[...] = s_ref[0]
    kiota = lax.broadcasted_iota(jnp.int32, (NK, TT), 0).astype(F32)

    def half_topk(it, carry):
        for c in range(2):
            x = xs_ref[c]
            m = jnp.max(x, axis=0, keepdims=True)
            idx = jnp.min(jnp.where(x == m, kiota, float(NK)), axis=0, keepdims=True)
            xs_ref[c] = jnp.where(kiota == idx, NEG, x)
            sv_ref[c, pl.ds(it, 1), :] = m
            si_ref[c, pl.ds(it, 1), :] = idx
        return carry

    lax.fori_loop(0, K, half_topk, 0)

    jiota = lambda rows: lax.broadcasted_iota(jnp.int32, (rows, TT), 0).astype(F32)
    off = 0
    for i in range(K // 2):
        n = K // (i + 1)
        rows = -(-n // V7X_SUBLANES) * V7X_SUBLANES
        ok = jiota(rows) < float(n)
        cand_ref[pl.ds(off, rows), :] = jnp.where(ok, sv_ref[0, i:i + 1, :] + sv_ref[1, 0:rows, :], NEG)
        cidx_ref[pl.ds(off, rows), :] = si_ref[0, i:i + 1, :] * float(NK) + si_ref[1, 0:rows, :]
        pf_ref[pl.ds(off, rows), :] = jnp.where(ok, float(i * K) + jiota(rows), float(K * K))
        off += rows
    cand_ref[pl.ds(off, K // 2), :] = sv_ref[0, K // 2:K, :] + sv_ref[1, 0:1, :]
    cidx_ref[pl.ds(off, K // 2), :] = si_ref[0, K // 2:K, :] * float(NK) + si_ref[1, 0:1, :]
    pf_ref[pl.ds(off, K // 2), :] = (float(K // 2) + jiota(K // 2)) * float(K)
    piota = pf_ref[...]

    def pair_topk(it, carry):
        x = cand_ref[...]
        m = jnp.max(x, axis=0, keepdims=True)
        pos = jnp.min(jnp.where(x == m, piota, float(K * K)), axis=0, keepdims=True)
        sel = piota == pos
        cand_ref[...] = jnp.where(sel, NEG, x)
        ts_ref[pl.ds(it, 1), :] = m
        e_ref[0, pl.ds(it, 1), :] = jnp.max(jnp.where(sel, cidx_ref[...], -1.0), axis=0,
                                            keepdims=True).astype(jnp.int32)
        return carry

    lax.fori_loop(0, K, pair_topk, 0)
    ts = ts_ref[...]
    ex = jnp.exp(ts - ts[0:1, :])
    g_ref[0] = ex / jnp.sum(ex, axis=0, keepdims=True)


def peer_route_topk(sT):
    H, _, NK, N = sT.shape
    TT, K = ROUTE_TT, PEER_TOPK
    out_spec = pl.BlockSpec((1, K, TT), lambda h, i: (h, 0, i))
    return pl.pallas_call(
        _peer_route_body,
        grid=(H, N // TT),
        in_specs=[pl.BlockSpec((1, 2, NK, TT), lambda h, i: (h, 0, 0, i))],
        out_specs=[out_spec, out_spec],
        out_shape=[jax.ShapeDtypeStruct((H, K, N), jnp.int32), jax.ShapeDtypeStruct((H, K, N), F32)],
        scratch_shapes=[pltpu.VMEM((2, NK, TT), F32), pltpu.VMEM((2, K, TT), F32), pltpu.VMEM((2, K, TT), F32),
                        pltpu.VMEM((ROUTE_PAIR_ROWS, TT), F32), pltpu.VMEM((ROUTE_PAIR_ROWS, TT), F32),
                        pltpu.VMEM((ROUTE_PAIR_ROWS, TT), F32), pltpu.VMEM((K, TT), F32)],
        compiler_params=pltpu.CompilerParams(dimension_semantics=("parallel", "parallel")),
        name="peer_route_topk",
    )(sT)


def peer_route(xf, p):
    N = xf.shape[0]
    wq, keys = p['peer_wq'], p['peer_keys'].astype(F32)
    q = (xf @ wq).astype(F32).reshape(N, PEER_HEADS, 2, PEER_DQ // 2)
    sT = jnp.einsum('thcq,hckq->hckt', q, keys)
    eT, gT = peer_route_topk(sT)
    return (eT.transpose(2, 0, 1).reshape(N, PEER_SEL), gT.transpose(2, 0, 1).reshape(N, PEER_SEL))


def peer_ffn(h, p):
    Bn, L, D = h.shape
    N = Bn * L
    xf = h.reshape(N, D)
    eidx, gate = peer_route(xf, p)
    u_pk, v_pk = p['peer_u_pk'], p['peer_v_pk']
    outs, c0, size = [], 0, PEER_FIRST_CHUNK
    while c0 < N:
        rows = slice(c0, min(N, c0 + size))
        ug, vg = sc_gather_pair(u_pk, v_pk, eidx[rows].reshape(-1))
        outs.append(peer_apply(xf[rows], gate[rows], ug, vg))
        c0, size = rows.stop, min(2 * size, PEER_CHUNK)
    return jnp.concatenate(outs, axis=0).reshape(Bn, L, D)


def merge_groups(ys, g, dtype):
    outs, off = [], 0
    for y, w in zip(ys, GROUP_WIDTHS):
        outs.append(rmsnorm(y, g[off:off + w]).astype(dtype))
        off += w
    return jnp.concatenate(outs, axis=-1)


def token_mixers(h_c, h_l, p, need_ctx):
    zc = split_cols(h_c @ p['w_in'])
    zl = split_cols(h_l @ p['w_in'])
    a_c, a_l = mixer_rglru(zc[0:2], zl[0:2], p, need_ctx)
    b_c, b_l = mixer_rwkv(zc[2:8], zl[2:8], p, need_ctx)
    m_c, m_l = mixer_mlstm(zc[8:13], zl[8:13], p, need_ctx)
    d_l = mixer_hyena(zl[13:16], p)
    o_l = merge_groups([a_l, b_l, m_l, d_l], p['grp_g'], h_l.dtype) @ p['w_out']
    if not need_ctx:
        return None, o_l
    d_c = mixer_hyena(zc[13:16], p)
    o_c = merge_groups([a_c, b_c, m_c, d_c], p['grp_g'], h_c.dtype) @ p['w_out']
    return o_c, o_l


def trunk_layer(x_l, x_c, c, c_ctx, p, need_ctx):
    Bn = c.shape[0]
    mod_l = (jax.nn.silu(c) @ p['ada_w'] + p['ada_b']).reshape(Bn, 6, 1, D_MODEL)
    mod_c = (jax.nn.silu(c_ctx) @ p['ada_w'] + p['ada_b']).reshape(6, 1, 1, D_MODEL)
    h_l = modulate(rmsnorm(x_l, p['norm1_g']), mod_l[:, 0], mod_l[:, 1])
    h_c = modulate(rmsnorm(x_c, p['norm1_g']), mod_c[0], mod_c[1])
    o_c, o_l = token_mixers(h_c, h_l, p, need_ctx)
    x_l = x_l + mod_l[:, 2] * o_l
    x_l = x_l + mod_l[:, 5] * peer_ffn(modulate(rmsnorm(x_l, p['norm2_g']), mod_l[:, 3], mod_l[:, 4]), p)
    if need_ctx:
        x_c = x_c + mod_c[2] * o_c
        x_c = x_c + mod_c[5] * peer_ffn(modulate(rmsnorm(x_c, p['norm2_g']), mod_c[3], mod_c[4]), p)
    return x_l, x_c


def _final_norm_body(x_ref, g_ref, o_ref):
    xf = x_ref[...]
    y = xf * lax.rsqrt(jnp.mean(xf * xf, axis=-1, keepdims=True) + EPS)
    o_ref[...] = y * g_ref[...]


def final_rmsnorm(x, g):
    Bn, L, D = x.shape
    rows = Bn * L
    tile = 1024
    out = pl.pallas_call(
        _final_norm_body,
        grid=(rows // tile,),
        in_specs=[pl.BlockSpec((tile, D), lambda i: (i, 0)), pl.BlockSpec((1, D), lambda i: (0, 0))],
        out_specs=pl.BlockSpec((tile, D), lambda i: (i, 0)),
        out_shape=jax.ShapeDtypeStruct((rows, D), x.dtype),
        name="final_rmsnorm",
    )(x.reshape(rows, D), g.reshape(1, D))
    return out.reshape(Bn, L, D)


def kernel(x, c, ctx, c_ctx, ada_w, ada_b, norm1_g, norm2_g, w_in, w_out, grp_g,
           lru_conv_w, lru_conv_b, lru_wr, lru_br, lru_wi, lru_bi, lru_lam,
           rwkv_mu, rwkv_w0, rwkv_w2, rwkv_a0, rwkv_a2, rwkv_g2, rwkv_kk, rwkv_ka, rwkv_rk, rwkv_ln_g, rwkv_ln_b,
           mlstm_conv_w, mlstm_conv_b, mlstm_gate_b,
           hy_conv_w, hy_conv_b, hy_w1, hy_b1, hy_w2, hy_b2, hy_w3, hy_freq, hy_bias,
           peer_wq, peer_keys, peer_u, peer_v, final_g):
    x_l, x_c = x, ctx
    for i in range(DEPTH):
        p = dict(ada_w=ada_w[i], ada_b=ada_b[i], norm1_g=norm1_g[i], norm2_g=norm2_g[i], w_in=w_in[i],
                 w_out=w_out[i], grp_g=grp_g[i],
                 lru_conv_w=lru_conv_w[i], lru_conv_b=lru_conv_b[i], lru_wr=lru_wr[i], lru_br=lru_br[i],
                 lru_wi=lru_wi[i], lru_bi=lru_bi[i], lru_lam=lru_lam[i],
                 rwkv_mu=rwkv_mu[i], rwkv_w0=rwkv_w0[i], rwkv_w2=rwkv_w2[i], rwkv_a0=rwkv_a0[i],
                 rwkv_a2=rwkv_a2[i], rwkv_g2=rwkv_g2[i], rwkv_kk=rwkv_kk[i], rwkv_ka=rwkv_ka[i],
                 rwkv_rk=rwkv_rk[i], rwkv_ln_g=rwkv_ln_g[i], rwkv_ln_b=rwkv_ln_b[i],
                 mlstm_conv_w=mlstm_conv_w[i], mlstm_conv_b=mlstm_conv_b[i], mlstm_gate_b=mlstm_gate_b[i],
                 hy_conv_w=hy_conv_w[i], hy_conv_b=hy_conv_b[i], hy_w1=hy_w1[i], hy_b1=hy_b1[i],
                 hy_w2=hy_w2[i], hy_b2=hy_b2[i], hy_w3=hy_w3[i], hy_freq=hy_freq[i], hy_bias=hy_bias[i],
                 peer_wq=peer_wq[i], peer_keys=peer_keys[i],
                 peer_u_pk=pack_bf16_halves(peer_u[i]), peer_v_pk=pack_bf16_halves(peer_v[i]))
        x_l, x_c = trunk_layer(x_l, x_c, c, c_ctx, p, i < DEPTH - 1)
    return final_rmsnorm(x_l, final_g)
```

```python
import functools
import math
import jax, jax.numpy as jnp
from jax import lax
import numpy as np
from jax.experimental import pallas as pl
from jax.experimental.pallas import tpu as pltpu
from jax.experimental.pallas import tpu_sc as plsc

D_MODEL = 1024
DEPTH = 2

GRID_W = 64
F32 = jnp.float32
V7X_VMEM_BYTES = 64 * 1024 * 1024
V7X_SUBLANES = 8
BIG_BLOCK_VMEM_LIMIT = V7X_VMEM_BYTES * 5 // 8
EPS = 1e-6
HEAD_DIM = 64
D_MIX = D_MODEL
W_LRU = D_MIX // 4
W_RWKV = D_MIX // 4
W_MLSTM = D_MIX // 4
W_HYENA = D_MIX - W_LRU - W_RWKV - W_MLSTM
GROUP_WIDTHS = (W_LRU, W_RWKV, W_MLSTM, W_HYENA)
H_RWKV = W_RWKV // HEAD_DIM
H_MLSTM = W_MLSTM // HEAD_DIM
CONV_W = 4
LRU_C = 8.0
RWKV_LORA_W = 32
RWKV_LORA_A = 32
RWKV_LORA_G = 64
RWKV_GN_EPS = 64e-5
MLSTM_CHUNK = 64
HYENA_ORDER = 2
HYENA_SHORT = 3
HYENA_EMB = 33
HYENA_TARGET = 1e-2
HYENA_FAST = 0.3
HYENA_SLOW = 1.5
PEER_HEADS = 8
PEER_NKEYS = 128
PEER_TOPK = 16
PEER_DQ = 256
IN_SPLITS = (W_LRU, W_LRU, W_RWKV, W_RWKV, W_RWKV, RWKV_LORA_W, RWKV_LORA_A, RWKV_LORA_G, W_MLSTM, W_MLSTM, W_MLSTM, W_MLSTM, 4 * H_MLSTM, W_HYENA, W_HYENA, W_HYENA)


def rmsnorm(x, g):
    xf = x.astype(F32)
    y = xf * lax.rsqrt(jnp.mean(xf * xf, axis=-1, keepdims=True) + EPS)
    return (y * g.astype(F32)).astype(x.dtype)


def modulate(h, shift, scale):
    return h * (1 + scale) + shift


def split_cols(z):
    offs = np.cumsum(IN_SPLITS)[:-1].tolist()
    return jnp.split(z, offs, axis=-1)


def dwconv(x, w, b, pad_left):
    K = w.shape[0]
    y = lax.conv_general_dilated(x, w[:, None, :].astype(x.dtype), (1,), [(pad_left, K - 1 - pad_left)],
                                 dimension_numbers=('NWC', 'WIO', 'NWC'), feature_group_count=x.shape[-1])
    return y + b.astype(x.dtype)


def token_shift(x, mu):
    prev = jnp.pad(x, ((0, 0), (1, 0), (0, 0)))[:, :-1]
    nxt = jnp.pad(x, ((0, 0), (0, 1), (0, 0)))[:, 1:]
    return x + mu[0] * (prev - x) + mu[1] * (nxt - x)


def split_heads(t, h):
    Bn, L, W = t.shape
    return t.reshape(Bn, L, h, W // h)


def to_colmajor(a):
    Bn, L, C = a.shape
    rows = L // GRID_W
    return a.reshape(Bn, rows, GRID_W, C).transpose(0, 2, 1, 3).reshape(Bn, L, C)


def from_colmajor(a):
    Bn, L, C = a.shape
    rows = L // GRID_W
    return a.reshape(Bn, GRID_W, rows, C).transpose(0, 2, 1, 3).reshape(Bn, L, C)


LRU_TC = 256


def _expm1(z):
    u = jnp.exp(z)
    return jnp.where(u == 1.0, z, (u - 1.0) * z / jnp.where(u == 1.0, 1.0, jnp.log(u)))


def _lru_body(x_ref, h0_ref, wr_ref, wi_ref, br_ref, bi_ref, sp_ref, h_ref, hT_ref, st_ref, *, reverse):
    TC, W = x_ref.shape[1], x_ref.shape[2]

    @pl.when(pl.program_id(1) == 0)
    def _():
        st_ref[...] = h0_ref[0]

    x = x_ref[0]
    r = jax.nn.sigmoid(jnp.dot(x, wr_ref[...], preferred_element_type=F32) + br_ref[...])
    i = jax.nn.sigmoid(jnp.dot(x, wi_ref[...], preferred_element_type=F32) + bi_ref[...])
    log_a = -LRU_C * r * sp_ref[...]
    a = jnp.exp(log_a)
    b = jnp.sqrt(-_expm1(2.0 * log_a)) * (i * x)
    row = lax.broadcasted_iota(jnp.int32, (TC, W), 0)
    s = 1
    while s < TC:
        if reverse:
            keep = row < TC - s
            a_sh = jnp.where(keep, pltpu.roll(a, TC - s, 0), 1.0)
            b_sh = jnp.where(keep, pltpu.roll(b, TC - s, 0), 0.0)
        else:
            keep = row >= s
            a_sh = jnp.where(keep, pltpu.roll(a, s, 0), 1.0)
            b_sh = jnp.where(keep, pltpu.roll(b, s, 0), 0.0)
        b = b + a * b_sh
        a = a * a_sh
        s *= 2
    h = b + a * st_ref[...]
    h_ref[0] = h
    last = h[0:1, :] if reverse else h[TC - 1:TC, :]
    st_ref[...] = last
    hT_ref[0] = last


def lru_scan_dir(xc, h0, wr_bd, wi_bd, br, bi, sp, reverse):
    Bn, L, W = xc.shape
    TC = min(LRU_TC, L)
    n = L // TC
    tmap = (lambda b, c: (b, n - 1 - c, 0)) if reverse else (lambda b, c: (b, c, 0))
    wspec = pl.BlockSpec((W, W), lambda b, c: (0, 0))
    vspec = pl.BlockSpec((1, W), lambda b, c: (0, 0))
    sspec = pl.BlockSpec((1, 1, W), lambda b, c: (b, 0, 0))
    return pl.pallas_call(
        functools.partial(_lru_body, reverse=reverse),
        grid=(Bn, n),
        in_specs=[pl.BlockSpec((1, TC, W), tmap), sspec, wspec, wspec, vspec, vspec, vspec],
        out_specs=[pl.BlockSpec((1, TC, W), tmap), sspec],
        out_shape=[jax.ShapeDtypeStruct((Bn, L, W), F32), jax.ShapeDtypeStruct((Bn, 1, W), F32)],
        scratch_shapes=[pltpu.VMEM((1, W), F32)],
        compiler_params=pltpu.CompilerParams(dimension_semantics=("parallel", "arbitrary")),
        name="rglru_scan",
    )(xc, h0, wr_bd, wi_bd, br, bi, sp)


def head_block_diag(w):
    D2, H, N, _ = w.shape
    return jnp.einsum('dhij,hg->dhigj', w, jnp.eye(H, dtype=w.dtype)).reshape(D2, H * N, H * N)


def mixer_rglru(f_c, f_l, p, need_ctx):
    (x_c, g_c), (x_l, g_l) = f_c, f_l
    xc_c = dwconv(x_c, p['lru_conv_w'], p['lru_conv_b'], CONV_W // 2).astype(F32)
    xc_l = dwconv(x_l, p['lru_conv_w'], p['lru_conv_b'], CONV_W // 2).astype(F32)
    h0 = jnp.zeros((xc_l.shape[0], 1, W_LRU), F32)
    wr, wi = head_block_diag(p['lru_wr']), head_block_diag(p['lru_wi'])
    sp = jax.nn.softplus(-p['lru_lam'])
    hs_c, hs_l = [], []
    for d in range(2):
        gates = (wr[d], wi[d], p['lru_br'][d][None], p['lru_bi'][d][None], sp[d][None])
        hc, st = lru_scan_dir(xc_c, h0, *gates, reverse=(d == 1))
        hl, _ = lru_scan_dir(xc_l, st, *gates, reverse=(d == 1))
        hs_c.append(hc)
        hs_l.append(hl)
    y_l = jax.nn.gelu(g_l.astype(F32)) * (hs_l[0] + hs_l[1])
    y_c = jax.nn.gelu(g_c.astype(F32)) * (hs_c[0] + hs_c[1]) if need_ctx else None
    return y_c, y_l


def rwkv_shift(f, p):
    r, k, v, zw, za, zg = [t.astype(F32) for t in f]
    mu = p['rwkv_mu']
    return (token_shift(r, mu[0]), token_shift(k, mu[1]), token_shift(v, mu[2]), zw, za, zg)


def rwkv_dir_inputs(f, p, d):
    r, k, v, zw, za, _ = f
    w_log = -jax.nn.softplus(-(p['rwkv_w0'][d] + jnp.tanh(zw) @ p['rwkv_w2'][d])) - 0.5
    decay = jnp.exp(-jnp.exp(w_log))
    a = jax.nn.sigmoid(p['rwkv_a0'][d] + za @ p['rwkv_a2'][d])
    kk = split_heads(k * p['rwkv_kk'], H_RWKV)
    kk = kk / jnp.maximum(jnp.linalg.norm(kk, axis=-1, keepdims=True), 1e-12)
    kd = k * (1 + (a - 1) * p['rwkv_ka'])
    return [split_heads(r, H_RWKV), split_heads(decay, H_RWKV), split_heads(kd, H_RWKV),
            split_heads(v, H_RWKV), kk, kk * split_heads(a, H_RWKV)]


RWKV_TC = HEAD_DIM
RWKV_PAIRS = 32


def _rwkv_body(kk_ref, v_ref, r_ref, w_ref, kka_ref, k_ref, y_ref, s_ref):
    NP, N = RWKV_PAIRS, HEAD_DIM
    rev = pl.program_id(0)

    @pl.when(pl.program_id(1) == 0)
    def _():
        s_ref[...] = jnp.zeros_like(s_ref)

    y_ref[...] = jnp.zeros_like(y_ref)
    lane = lax.broadcasted_iota(jnp.int32, (N, 2 * N), 1)
    row = lax.broadcasted_iota(jnp.int32, (N, 2 * N), 0)
    diag = (lane % N) == row
    seg4 = lax.broadcasted_iota(jnp.int32, (4 * N, 4 * N), 0) // N == \
        lax.broadcasted_iota(jnp.int32, (4 * N, 4 * N), 1) // N
    ones4 = jnp.where(seg4, 1.0, 0.0).astype(jnp.bfloat16)

    def segsum_bcast(x):
        H = x.shape[0] // 2
        xb = x.astype(jnp.bfloat16)
        lhs = jnp.concatenate([xb[:H].reshape(H * N, 2 * N), xb[H:].reshape(H * N, 2 * N)], axis=-1)
        out = jnp.dot(lhs, ones4, preferred_element_type=F32)
        return jnp.concatenate([out[:, :2 * N].reshape(H, N, 2 * N), out[:, 2 * N:].reshape(H, N, 2 * N)], axis=0)

    def rowv(ref, t):
        x = ref[:, pl.ds(t, 1), :] if len(ref.shape) == 3 else ref[0, :, pl.ds(t, 1), :]
        return jnp.concatenate([x[:, :, :2 * N], x[:, :, 2 * N:]], axis=0)

    def step(i, carry):
        t = i + rev * (RWKV_TC - 1 - 2 * i)
        tp = jnp.clip(t - 1 + 2 * rev, 0, RWKV_TC - 1)
        S = s_ref[...]
        SA = segsum_bcast(S * rowv(kk_ref, t))
        VB = segsum_bcast(jnp.where(diag, rowv(v_ref, t), 0.0))
        YP = segsum_bcast(S * rowv(r_ref, tp))
        y_ref[0, :, 0] = jnp.where(((lane % N) == tp) & (i > 0), YP, y_ref[0, :, 0])
        s_ref[...] = S * rowv(w_ref, t) - SA * rowv(kka_ref, t) + VB * rowv(k_ref, t)
        return carry

    lax.fori_loop(0, RWKV_TC, step, 0)
    t_last = (RWKV_TC - 1) * (1 - rev)
    Y = segsum_bcast(s_ref[...] * rowv(r_ref, t_last))
    y_ref[0, :, 0] = jnp.where((lane % N) == t_last, Y, y_ref[0, :, 0])


def rwkv_scan_bidir(kk, v, r, w, kka, k, n_ctx_blocks):
    Bn, L, C = kk.shape
    TC, N = RWKV_TC, HEAD_DIM
    n = L // TC
    assert RWKV_PAIRS == Bn * C // (2 * N)

    def blk(d, c):
        back = jnp.where(c < n_ctx_blocks, n_ctx_blocks - 1 - c, n + n_ctx_blocks - 1 - c)
        return jnp.where(d == 0, c, back)

    shared = pl.BlockSpec((Bn, TC, C), lambda d, c: (0, blk(d, c), 0))
    per_dir = pl.BlockSpec((1, Bn, TC, C), lambda d, c: (d, 0, blk(d, c), 0))
    return pl.pallas_call(
        _rwkv_body,
        grid=(2, n),
        in_specs=[shared, shared, shared, per_dir, per_dir, per_dir],
        out_specs=pl.BlockSpec((1, RWKV_PAIRS, 1, N, 2 * N), lambda d, c: (d, 0, blk(d, c), 0, 0)),
        out_shape=jax.ShapeDtypeStruct((2, RWKV_PAIRS, n, N, 2 * N), F32),
        scratch_shapes=[pltpu.VMEM((RWKV_PAIRS, N, 2 * N), F32)],
        compiler_params=pltpu.CompilerParams(dimension_semantics=("parallel", "arbitrary"),
                                             vmem_limit_bytes=BIG_BLOCK_VMEM_LIMIT),
        name="rwkv7_scan",
    )(kk, v, r, w, kka, k)


def rwkv7_scan_bidir(ins_c, ins_l):
    Bn, Lc = ins_c[0][0].shape[:2]
    Ll = ins_l[0][0].shape[1]
    L = Lc + Ll
    cat = lambda d, j: jnp.concatenate([ins_c[d][j], ins_l[d][j]], axis=1).reshape(Bn, L, W_RWKV)
    per_dir = lambda j: jnp.stack([cat(0, j), cat(1, j)])
    yT = rwkv_scan_bidir(cat(0, 4), cat(0, 3), cat(0, 0), per_dir(1), per_dir(5), per_dir(2), Lc // RWKV_TC)
    halves = W_RWKV // (2 * HEAD_DIM)
    y = yT.reshape(2, halves, Bn, L // HEAD_DIM, HEAD_DIM, 2, HEAD_DIM)
    y = y.transpose(0, 2, 3, 6, 1, 5, 4).reshape(2, Bn, L, H_RWKV, HEAD_DIM)
    return ([y[d, :, :Lc] for d in range(2)], [y[d, :, Lc:] for d in range(2)])


def rwkv_bonus(ins, p):
    r, _, kd, v = ins[:4]
    return jnp.sum(r * kd * p['rwkv_rk'], axis=-1, keepdims=True) * v


def rwkv_out(y, bonus, zg, p):
    Bn, L, H, N = y.shape
    mu = jnp.mean(y, axis=-1, keepdims=True)
    var = jnp.mean(jnp.square(y - mu), axis=-1, keepdims=True)
    yn = ((y - mu) * lax.rsqrt(var + RWKV_GN_EPS)).reshape(Bn, L, H * N) * p['rwkv_ln_g'] + p['rwkv_ln_b']
    g = jax.nn.sigmoid(zg) @ p['rwkv_g2']
    return (yn + bonus.reshape(Bn, L, H * N)) * g


def mixer_rwkv(f_c, f_l, p, need_ctx):
    fc, fl = rwkv_shift(f_c, p), rwkv_shift(f_l, p)
    ins_c = [rwkv_dir_inputs(fc, p, d) for d in range(2)]
    ins_l = [rwkv_dir_inputs(fl, p, d) for d in range(2)]
    o_c, o_l = rwkv7_scan_bidir(ins_c, ins_l)
    y_c, y_l = [], []
    for d in range(2):
        y_l.append((o_l[d], rwkv_bonus(ins_l[d], p)))
        if need_ctx:
            y_c.append((o_c[d], rwkv_bonus(ins_c[d], p)))
    out_l = rwkv_out(y_l[0][0] + y_l[1][0], y_l[0][1] + y_l[1][1], fl[5], p)
    out_c = rwkv_out(y_c[0][0] + y_c[1][0], y_c[0][1] + y_c[1][1], fc[5], p) if need_ctx else None
    return out_c, out_l


def mlstm_prep(f, p):
    q, k, v, o, gz = f
    Bn, L, _ = q.shape
    qk = jax.nn.silu(dwconv(jnp.concatenate([q, k], axis=-1), p['mlstm_conv_w'], p['mlstm_conv_b'], CONV_W // 2)).astype(F32)
    q, k = jnp.split(qk, 2, axis=-1)
    gates = gz.astype(F32).reshape(Bn, L, 2, 2, H_MLSTM) + p['mlstm_gate_b']
    return q * HEAD_DIM ** -0.5, k, v.astype(F32), o.astype(F32), gates


MLSTM_ROWS = 4


def _mlstm_body(q_ref, k_ref, v_ref, gc_ref, gr_ref, h_ref, c_ref, n_ref, m_ref):
    T, N, H = MLSTM_CHUNK, HEAD_DIM, H_MLSTM
    rev = pl.program_id(0)

    @pl.when(pl.program_id(2) == 0)
    def _():
        c_ref[...] = jnp.zeros_like(c_ref)
        n_ref[...] = jnp.zeros_like(n_ref)
        m_ref[...] = jnp.zeros_like(m_ref)

    ti = lax.broadcasted_iota(jnp.int32, (T, T), 0)
    si = lax.broadcasted_iota(jnp.int32, (T, T), 1)
    mask = (si - ti) * (1 - 2 * rev) <= 0
    tri = jnp.where(mask, 1.0, 0.0)
    hp = lax.Precision.HIGHEST
    gcs = [gc_ref[0, r, 0] for r in range(MLSTM_ROWS)]
    grs = [gr_ref[0, r, 0] for r in range(MLSTM_ROWS)]
    b_cols = [jnp.dot(tri, g[:, H:], precision=hp, preferred_element_type=F32) for g in gcs]
    b_rows = [lax.dot_general(g[H:], tri, (((1,), (1,)), ((), ())), precision=hp, preferred_element_type=F32)
              for g in grs]
    b_tots = [jnp.sum(g[:, H:], axis=0, keepdims=True) for g in gcs]
    units = [(r, h) for r in range(MLSTM_ROWS) for h in range(H)]
    hs = range(len(units))
    sl = [slice(h * N, (h + 1) * N) for _, h in units]
    qh = [q_ref[units[h][0], :, sl[h]] for h in hs]
    kh = [k_ref[units[h][0], :, sl[h]] for h in hs]
    vh = [v_ref[units[h][0], :, sl[h]] for h in hs]
    C = [c_ref[h] for h in hs]
    n = [n_ref[h] for h in hs]
    m = [m_ref[h][:, 0:1] for h in hs]
    bc = [b_cols[r][:, h:h + 1] for r, h in units]
    br = [b_rows[r][h:h + 1, :] for r, h in units]
    ic = [gcs[r][:, h:h + 1] for r, h in units]
    ir = [grs[r][h:h + 1, :] for r, h in units]
    bT = [b_tots[r][:, h:h + 1] for r, h in units]
    nt = (((1,), (1,)), ((), ()))
    qk = [lax.dot_general(qh[h], kh[h], nt, preferred_element_type=F32) for h in hs]
    qc = [lax.dot_general(qh[h], C[h], nt, preferred_element_type=F32) for h in hs]
    logd = [jnp.where(mask, bc[h] - br[h] + ir[h], -jnp.inf) for h in hs]
    inter = [bc[h] + m[h] for h in hs]
    mt = [jnp.maximum(inter[h], jnp.max(logd[h], axis=-1, keepdims=True)) for h in hs]
    s = [qk[h] * jnp.exp(logd[h] - mt[h]) for h in hs]
    e_inter = [jnp.exp(inter[h] - mt[h]) for h in hs]
    num = [jnp.dot(s[h], vh[h], preferred_element_type=F32) + e_inter[h] * qc[h] for h in hs]
    den = [jnp.sum(s[h], axis=-1, keepdims=True) + e_inter[h] * jnp.sum(qh[h] * n[h], axis=-1, keepdims=True)
           for h in hs]
    out = [num[h] / jnp.maximum(jnp.abs(den[h]), jnp.exp(-mt[h])) for h in hs]
    for r in range(MLSTM_ROWS):
        h_ref[0, r] = jnp.concatenate(out[r * H:(r + 1) * H], axis=-1)
    m_new = [jnp.maximum(bT[h] + m[h], jnp.max(bT[h] - br[h] + ir[h], axis=-1, keepdims=True)) for h in hs]
    w_col = [jnp.exp(bT[h] - bc[h] + ic[h] - m_new[h]) for h in hs]
    dec = [jnp.exp(bT[h] + m[h] - m_new[h]) for h in hs]
    vk = [lax.dot_general(vh[h] * w_col[h], kh[h], (((0,), (0,)), ((), ())), preferred_element_type=F32) for h in hs]
    for h in hs:
        c_ref[h] = dec[h] * C[h] + vk[h]
        n_ref[h] = dec[h] * n[h] + jnp.sum(kh[h] * w_col[h], axis=0, keepdims=True)
        m_ref[h] = jnp.broadcast_to(m_new[h], (1, 128))


def mlstm_scan_bidir(q, k, v, gcol, grow, n_ctx_blocks):
    Bn, L, W = q.shape
    T = MLSTM_CHUNK
    n = L // T

    def blk(d, c):
        back = jnp.where(c < n_ctx_blocks, n_ctx_blocks - 1 - c, n + n_ctx_blocks - 1 - c)
        return jnp.where(d == 0, c, back)

    R = MLSTM_ROWS
    xspec = pl.BlockSpec((R, T, W), lambda d, b, c: (b, blk(d, c), 0))
    gcs = pl.BlockSpec((1, R, 1, T, 2 * H_MLSTM), lambda d, b, c: (d, b, blk(d, c), 0, 0))
    grs = pl.BlockSpec((1, R, 1, 2 * H_MLSTM, T), lambda d, b, c: (d, b, blk(d, c), 0, 0))
    return pl.pallas_call(
        _mlstm_body,
        grid=(2, Bn // R, n),
        in_specs=[xspec, xspec, xspec, gcs, grs],
        out_specs=pl.BlockSpec((1, R, T, W), lambda d, b, c: (d, b, blk(d, c), 0)),
        out_shape=jax.ShapeDtypeStruct((2, Bn, L, W), F32),
        scratch_shapes=[pltpu.VMEM((R * H_MLSTM, HEAD_DIM, HEAD_DIM), F32), pltpu.VMEM((R * H_MLSTM, 1, HEAD_DIM), F32),
                        pltpu.VMEM((R * H_MLSTM, 1, 128), F32)],
        compiler_params=pltpu.CompilerParams(dimension_semantics=("parallel", "parallel", "arbitrary")),
        name="mlstm_chunkwise",
    )(q, k, v, gcol, grow)


def mixer_mlstm(f_c, f_l, p, need_ctx):
    f_l = [to_colmajor(t) for t in f_l]
    qc, kc, vc, oc, gc = mlstm_prep(f_c, p)
    ql, kl, vl, ol, gl = mlstm_prep(f_l, p)
    Bn, Lc = qc.shape[:2]
    cat = lambda a, b: jnp.concatenate([a, b], axis=1)
    g = cat(gc, gl)
    L = g.shape[1]
    g = jnp.concatenate([g[:, :, :, 0], jax.nn.log_sigmoid(g[:, :, :, 1])], axis=-1)
    gcol = g.transpose(2, 0, 1, 3).reshape(2, Bn, L // MLSTM_CHUNK, MLSTM_CHUNK, 2 * H_MLSTM)
    h = mlstm_scan_bidir(cat(qc, ql), cat(kc, kl), cat(vc, vl), gcol, gcol.transpose(0, 1, 2, 4, 3),
                         Lc // MLSTM_CHUNK)
    hs = h[0] + h[1]
    y_l = from_colmajor(jax.nn.sigmoid(ol) * hs[:, Lc:])
    y_c = jax.nn.sigmoid(oc) * hs[:, :Lc] if need_ctx else None
    return y_c, y_l


def hyena_spectrum(L, p):
    pos = jnp.arange(L, dtype=F32)
    t = pos / (L - 1)
    bands = (HYENA_EMB - 1) // 2
    freqs = jnp.linspace(1e-4, bands - 1, bands, dtype=F32)
    ang = (2 * math.pi / L) * pos[:, None] * freqs[None, :]
    z = jnp.concatenate([t[:, None], jnp.cos(ang), -jnp.sin(ang)], axis=-1)
    h = jnp.sin(p['hy_freq'][0] * (z @ p['hy_w1'] + p['hy_b1']))
    h = jnp.sin(p['hy_freq'][1] * (h @ p['hy_w2'] + p['hy_b2']))
    h = (h @ p['hy_w3']).astype(F32).reshape(L, HYENA_ORDER, 2, W_HYENA)
    deltas = jnp.abs(jnp.linspace(math.log(HYENA_TARGET) / HYENA_SLOW, math.log(HYENA_TARGET) / HYENA_FAST,
                                  W_HYENA, dtype=F32))
    h = h * jnp.exp(-t[:, None, None, None] * deltas)
    fwd, bwd = h[:, :, 0], h[:, :, 1]
    two = jnp.concatenate([fwd, jnp.zeros_like(fwd[:1]), jnp.flip(bwd[1:], axis=0)], axis=0)
    two = two / (jnp.sum(jnp.abs(two), axis=0, keepdims=True) + EPS)
    return jnp.fft.rfft(two, axis=0)


def long_conv(u, spec, bias):
    L = u.shape[1]
    y = jnp.fft.irfft(jnp.fft.rfft(u, n=2 * L, axis=1) * spec, n=2 * L, axis=1)[:, :L]
    return y + u * bias


def mixer_hyena(f, p):
    u = dwconv(jnp.concatenate(f, axis=-1), p['hy_conv_w'], p['hy_conv_b'], HYENA_SHORT // 2).astype(F32)
    v, x1, x2 = jnp.split(u, 3, axis=-1)
    spec = hyena_spectrum(u.shape[1], p)
    z = x1 * long_conv(v, spec[:, 0], p['hy_bias'][0])
    return x2 * long_conv(z, spec[:, 1], p['hy_bias'][1])


PEER_SEL = PEER_HEADS * PEER_TOPK
PEER_HALF = D_MODEL // 2
PEER_CHUNK = 4096
PEER_FIRST_CHUNK = 512
PEER_TT = 16
SC_GATHER_ROWS = 64


def pack_bf16_halves(tab):
    bits = lax.bitcast_convert_type(tab.astype(jnp.bfloat16), jnp.uint16).astype(jnp.uint32)
    half = tab.shape[1] // 2
    return bits[:, :half] | (bits[:, half:] << 16)


def sc_gather_pair(u_tab, v_tab, idx):
    n = idx.shape[0]
    W = u_tab.shape[1]
    info = plsc.get_sparse_core_info()
    nc, ns = info.num_cores, info.num_subcores
    per_w = n // (nc * ns)
    G = SC_GATHER_ROWS
    assert per_w * nc * ns == n and per_w % G == 0
    nsteps = per_w // G
    mesh = plsc.VectorSubcoreMesh(core_axis_name="c", subcore_axis_name="s")
    out = jax.ShapeDtypeStruct((n, W), u_tab.dtype)

    def body(u_hbm, v_hbm, idx_hbm, uo_hbm, vo_hbm, idx_v, urows, vrows, usem, vsem):
        wid = lax.axis_index("s") * nc + lax.axis_index("c")
        base = wid * per_w
        pltpu.sync_copy(idx_hbm.at[pl.ds(base, per_w)], idx_v)

        @pl.loop(0, nsteps)
        def _(i):
            off = pl.multiple_of(i * G, G)
            ids = idx_v.at[pl.ds(off, G)]
            cu = pltpu.async_copy(u_hbm.at[ids], urows, usem)
            cv = pltpu.async_copy(v_hbm.at[ids], vrows, vsem)
            cu.wait()
            pltpu.sync_copy(urows, uo_hbm.at[pl.ds(base + off, G)])
            cv.wait()
            pltpu.sync_copy(vrows, vo_hbm.at[pl.ds(base + off, G)])

    fn = pl.kernel(body, out_type=(out, out), mesh=mesh,
                   scratch_types=[pltpu.VMEM((per_w,), jnp.int32),
                                  pltpu.VMEM((G, W), u_tab.dtype), pltpu.VMEM((G, W), u_tab.dtype),
                                  pltpu.SemaphoreType.DMA, pltpu.SemaphoreType.DMA],
                   name="peer_sc_gather")
    return fn(u_tab, v_tab, idx)


def _unpack_halves(w):
    lo = lax.bitcast_convert_type(w << 16, F32)
    hi = lax.bitcast_convert_type(w & jnp.uint32(0xFFFF0000), F32)
    return lo, hi


def _peer_apply_body(x_ref, g_ref, ug_ref, vg_ref, o_ref):
    TT = x_ref.shape[0]
    gpad = jnp.concatenate([g_ref[...], jnp.zeros((PEER_SEL - TT, PEER_SEL), F32)], axis=0)
    gT = gpad.T
    for t in range(TT):
        rows = pl.ds(t * PEER_SEL, PEER_SEL)
        ulo, uhi = _unpack_halves(ug_ref[rows, :])
        xl = x_ref[pl.ds(t, 1), 0:PEER_HALF]
        xh = x_ref[pl.ds(t, 1), PEER_HALF:D_MODEL]
        dots = jnp.sum(ulo * xl + uhi * xh, axis=-1, keepdims=True)
        w = jax.nn.gelu(dots) * gT[:, t:t + 1]
        vlo, vhi = _unpack_halves(vg_ref[rows, :])
        o_ref[pl.ds(t, 1), 0:PEER_HALF] = jnp.sum(vlo * w, axis=0, keepdims=True)
        o_ref[pl.ds(t, 1), PEER_HALF:D_MODEL] = jnp.sum(vhi * w, axis=0, keepdims=True)


def peer_apply(x, gate, ug, vg):
    T = x.shape[0]
    TT = PEER_TT
    return pl.pallas_call(
        _peer_apply_body,
        grid=(T // TT,),
        in_specs=[pl.BlockSpec((TT, D_MODEL), lambda i: (i, 0)),
                  pl.BlockSpec((TT, PEER_SEL), lambda i: (i, 0)),
                  pl.BlockSpec((TT * PEER_SEL, PEER_HALF), lambda i: (i, 0)),
                  pl.BlockSpec((TT * PEER_SEL, PEER_HALF), lambda i: (i, 0))],
        out_specs=pl.BlockSpec((TT, D_MODEL), lambda i: (i, 0)),
        out_shape=jax.ShapeDtypeStruct((T, D_MODEL), F32),
        compiler_params=pltpu.CompilerParams(dimension_semantics=("parallel",),
                                             vmem_limit_bytes=BIG_BLOCK_VMEM_LIMIT),
        name="peer_apply",
    )(x, gate, ug, vg)


ROUTE_TT = 1024
ROUTE_PAIR_ROWS = sum(-(-(PEER_TOPK // (i + 1)) // V7X_SUBLANES) * V7X_SUBLANES
                      for i in range(PEER_TOPK // 2)) + PEER_TOPK // 2


def _peer_route_body(s_ref, e_ref, g_ref, xs_ref, sv_ref, si_ref, cand_ref, cidx_ref, pf_ref, ts_ref):
    K, NK, TT = PEER_TOPK, PEER_NKEYS, s_ref.shape[-1]
    NEG = -jnp.inf
    xs_ref[...] = s_ref[0]
    kiota = lax.broadcasted_iota(jnp.int32, (NK, TT), 0).astype(F32)

    def half_topk(it, carry):
        for c in range(2):
            x = xs_ref[c]
            m = jnp.max(x, axis=0, keepdims=True)
            idx = jnp.min(jnp.where(x == m, kiota, float(NK)), axis=0, keepdims=True)
            xs_ref[c] = jnp.where(kiota == idx, NEG, x)
            sv_ref[c, pl.ds(it, 1), :] = m
            si_ref[c, pl.ds(it, 1), :] = idx
        return carry

    lax.fori_loop(0, K, half_topk, 0)

    jiota = lambda rows: lax.broadcasted_iota(jnp.int32, (rows, TT), 0).astype(F32)
    off = 0
    for i in range(K // 2):
        n = K // (i + 1)
        rows = -(-n // V7X_SUBLANES) * V7X_SUBLANES
        ok = jiota(rows) < float(n)
        cand_ref[pl.ds(off, rows), :] = jnp.where(ok, sv_ref[0, i:i + 1, :] + sv_ref[1, 0:rows, :], NEG)
        cidx_ref[pl.ds(off, rows), :] = si_ref[0, i:i + 1, :] * float(NK) + si_ref[1, 0:rows, :]
        pf_ref[pl.ds(off, rows), :] = jnp.where(ok, float(i * K) + jiota(rows), float(K * K))
        off += rows
    cand_ref[pl.ds(off, K // 2), :] = sv_ref[0, K // 2:K, :] + sv_ref[1, 0:1, :]
    cidx_ref[pl.ds(off, K // 2), :] = si_ref[0, K // 2:K, :] * float(NK) + si_ref[1, 0:1, :]
    pf_ref[pl.ds(off, K // 2), :] = (float(K // 2) + jiota(K // 2)) * float(K)
    piota = pf_ref[...]

    def pair_topk(it, carry):
        x = cand_ref[...]
        m = jnp.max(x, axis=0, keepdims=True)
        pos = jnp.min(jnp.where(x == m, piota, float(K * K)), axis=0, keepdims=True)
        sel = piota == pos
        cand_ref[...] = jnp.where(sel, NEG, x)
        ts_ref[pl.ds(it, 1), :] = m
        e_ref[0, pl.ds(it, 1), :] = jnp.max(jnp.where(sel, cidx_ref[...], -1.0), axis=0,
                                            keepdims=True).astype(jnp.int32)
        return carry

    lax.fori_loop(0, K, pair_topk, 0)
    ts = ts_ref[...]
    ex = jnp.exp(ts - ts[0:1, :])
    g_ref[0] = ex / jnp.sum(ex, axis=0, keepdims=True)


def peer_route_topk(sT):
    H, _, NK, N = sT.shape
    TT, K = ROUTE_TT, PEER_TOPK
    out_spec = pl.BlockSpec((1, K, TT), lambda h, i: (h, 0, i))
    return pl.pallas_call(
        _peer_route_body,
        grid=(H, N // TT),
        in_specs=[pl.BlockSpec((1, 2, NK, TT), lambda h, i: (h, 0, 0, i))],
        out_specs=[out_spec, out_spec],
        out_shape=[jax.ShapeDtypeStruct((H, K, N), jnp.int32), jax.ShapeDtypeStruct((H, K, N), F32)],
        scratch_shapes=[pltpu.VMEM((2, NK, TT), F32), pltpu.VMEM((2, K, TT), F32), pltpu.VMEM((2, K, TT), F32),
                        pltpu.VMEM((ROUTE_PAIR_ROWS, TT), F32), pltpu.VMEM((ROUTE_PAIR_ROWS, TT), F32),
                        pltpu.VMEM((ROUTE_PAIR_ROWS, TT), F32), pltpu.VMEM((K, TT), F32)],
        compiler_params=pltpu.CompilerParams(dimension_semantics=("parallel", "parallel")),
        name="peer_route_topk",
    )(sT)


def peer_route(xf, p):
    N = xf.shape[0]
    wq, keys = p['peer_wq'], p['peer_keys'].astype(F32)
    q = (xf @ wq).astype(F32).reshape(N, PEER_HEADS, 2, PEER_DQ // 2)
    sT = jnp.einsum('thcq,hckq->hckt', q, keys)
    eT, gT = peer_route_topk(sT)
    return (eT.transpose(2, 0, 1).reshape(N, PEER_SEL), gT.transpose(2, 0, 1).reshape(N, PEER_SEL))


def peer_ffn(h, p):
    Bn, L, D = h.shape
    N = Bn * L
    xf = h.reshape(N, D)
    eidx, gate = peer_route(xf, p)
    u_pk, v_pk = p['peer_u_pk'], p['peer_v_pk']
    outs, c0, size = [], 0, PEER_FIRST_CHUNK
    while c0 < N:
        rows = slice(c0, min(N, c0 + size))
        ug, vg = sc_gather_pair(u_pk, v_pk, eidx[rows].reshape(-1))
        outs.append(peer_apply(xf[rows], gate[rows], ug, vg))
        c0, size = rows.stop, min(2 * size, PEER_CHUNK)
    return jnp.concatenate(outs, axis=0).reshape(Bn, L, D)


def merge_groups(ys, g, dtype):
    outs, off = [], 0
    for y, w in zip(ys, GROUP_WIDTHS):
        outs.append(rmsnorm(y, g[off:off + w]).astype(dtype))
        off += w
    return jnp.concatenate(outs, axis=-1)


def token_mixers(h_c, h_l, p, need_ctx):
    zc = split_cols(h_c @ p['w_in'])
    zl = split_cols(h_l @ p['w_in'])
    a_c, a_l = mixer_rglru(zc[0:2], zl[0:2], p, need_ctx)
    b_c, b_l = mixer_rwkv(zc[2:8], zl[2:8], p, need_ctx)
    m_c, m_l = mixer_mlstm(zc[8:13], zl[8:13], p, need_ctx)
    d_l = mixer_hyena(zl[13:16], p)
    o_l = merge_groups([a_l, b_l, m_l, d_l], p['grp_g'], h_l.dtype) @ p['w_out']
    if not need_ctx:
        return None, o_l
    d_c = mixer_hyena(zc[13:16], p)
    o_c = merge_groups([a_c, b_c, m_c, d_c], p['grp_g'], h_c.dtype) @ p['w_out']
    return o_c, o_l


def trunk_layer(x_l, x_c, c, c_ctx, p, need_ctx):
    Bn = c.shape[0]
    mod_l = (jax.nn.silu(c) @ p['ada_w'] + p['ada_b']).reshape(Bn, 6, 1, D_MODEL)
    mod_c = (jax.nn.silu(c_ctx) @ p['ada_w'] + p['ada_b']).reshape(6, 1, 1, D_MODEL)
    h_l = modulate(rmsnorm(x_l, p['norm1_g']), mod_l[:, 0], mod_l[:, 1])
    h_c = modulate(rmsnorm(x_c, p['norm1_g']), mod_c[0], mod_c[1])
    o_c, o_l = token_mixers(h_c, h_l, p, need_ctx)
    x_l = x_l + mod_l[:, 2] * o_l
    x_l = x_l + mod_l[:, 5] * peer_ffn(modulate(rmsnorm(x_l, p['norm2_g']), mod_l[:, 3], mod_l[:, 4]), p)
    if need_ctx:
        x_c = x_c + mod_c[2] * o_c
        x_c = x_c + mod_c[5] * peer_ffn(modulate(rmsnorm(x_c, p['norm2_g']), mod_c[3], mod_c[4]), p)
    return x_l, x_c


def _final_norm_body(x_ref, g_ref, o_ref):
    xf = x_ref[...]
    y = xf * lax.rsqrt(jnp.mean(xf * xf, axis=-1, keepdims=True) + EPS)
    o_ref[...] = y * g_ref[...]


def final_rmsnorm(x, g):
    Bn, L, D = x.shape
    rows = Bn * L
    tile = 1024
    out = pl.pallas_call(
        _final_norm_body,
        grid=(rows // tile,),
        in_specs=[pl.BlockSpec((tile, D), lambda i: (i, 0)), pl.BlockSpec((1, D), lambda i: (0, 0))],
        out_specs=pl.BlockSpec((tile, D), lambda i: (i, 0)),
        out_shape=jax.ShapeDtypeStruct((rows, D), x.dtype),
        name="final_rmsnorm",
    )(x.reshape(rows, D), g.reshape(1, D))
    return out.reshape(Bn, L, D)


def kernel(x, c, ctx, c_ctx, ada_w, ada_b, norm1_g, norm2_g, w_in, w_out, grp_g,
           lru_conv_w, lru_conv_b, lru_wr, lru_br, lru_wi, lru_bi, lru_lam,
           rwkv_mu, rwkv_w0, rwkv_w2, rwkv_a0, rwkv_a2, rwkv_g2, rwkv_kk, rwkv_ka, rwkv_rk, rwkv_ln_g, rwkv_ln_b,
           mlstm_conv_w, mlstm_conv_b, mlstm_gate_b,
           hy_conv_w, hy_conv_b, hy_w1, hy_b1, hy_w2, hy_b2, hy_w3, hy_freq, hy_bias,
           peer_wq, peer_keys, peer_u, peer_v, final_g):
    x_l, x_c = x, ctx
    for i in range(DEPTH):
        p = dict(ada_w=ada_w[i], ada_b=ada_b[i], norm1_g=norm1_g[i], norm2_g=norm2_g[i], w_in=w_in[i],
                 w_out=w_out[i], grp_g=grp_g[i],
                 lru_conv_w=lru_conv_w[i], lru_conv_b=lru_conv_b[i], lru_wr=lru_wr[i], lru_br=lru_br[i],
                 lru_wi=lru_wi[i], lru_bi=lru_bi[i], lru_lam=lru_lam[i],
                 rwkv_mu=rwkv_mu[i], rwkv_w0=rwkv_w0[i], rwkv_w2=rwkv_w2[i], rwkv_a0=rwkv_a0[i],
                 rwkv_a2=rwkv_a2[i], rwkv_g2=rwkv_g2[i], rwkv_kk=rwkv_kk[i], rwkv_ka=rwkv_ka[i],
                 rwkv_rk=rwkv_rk[i], rwkv_ln_g=rwkv_ln_g[i], rwkv_ln_b=rwkv_ln_b[i],
                 mlstm_conv_w=mlstm_conv_w[i], mlstm_conv_b=mlstm_conv_b[i], mlstm_gate_b=mlstm_gate_b[i],
                 hy_conv_w=hy_conv_w[i], hy_conv_b=hy_conv_b[i], hy_w1=hy_w1[i], hy_b1=hy_b1[i],
                 hy_w2=hy_w2[i], hy_b2=hy_b2[i], hy_w3=hy_w3[i], hy_freq=hy_freq[i], hy_bias=hy_bias[i],
                 peer_wq=peer_wq[i], peer_keys=peer_keys[i],
                 peer_u_pk=pack_bf16_halves(peer_u[i]), peer_v_pk=pack_bf16_halves(peer_v[i]))
        x_l, x_c = trunk_layer(x_l, x_c, c, c_ctx, p, i < DEPTH - 1)
    return final_rmsnorm(x_l, final_g)
```
